```python
import math, functools
import jax, jax.numpy as jnp
from jax import lax
import numpy as np

D_MODEL = 1024
BATCH = 1
SEQ = 16384
DEPTH = 2
DEC_BATCH = 128
DEC_SEQ = 8
PAST_LEN = 16384
PAGE_SIZE = 128

N_MEM = 256
D_FF = 2816
RMS_EPS = 1e-6
MLA_HEADS = 8
MLA_Q_RANK = 384
MLA_KV_RANK = 256
MLA_NOPE = 64
MLA_ROPE = 32
MLA_V = 64
ROPE_THETA = 10000.0
Q_BLOCK = 128
GDN_HEADS = 4
GDN_DK = 64
GDN_DV = 64
GDN_CONV = 4
GDN_CHUNK = 64
GDN_CONV_DIM = GDN_HEADS * (2 * GDN_DK + GDN_DV)
S5_GROUP = 16
S5_GROUPS = 16
S5_WIDTH = S5_GROUP * S5_GROUPS
S5_STATE = 64
XA_HEADS = 4
XA_HEAD_DIM = D_MODEL // XA_HEADS
IN_SPLITS = (MLA_Q_RANK, MLA_KV_RANK, MLA_ROPE, GDN_CONV_DIM, GDN_HEADS * GDN_DV, GDN_HEADS, GDN_HEADS, S5_WIDTH)
D_IN = sum(IN_SPLITS)
D_MIX = MLA_HEADS * MLA_V + GDN_HEADS * GDN_DV + S5_WIDTH

kernel_name = "hybrid_mla_gdn_s5_decoder_step"


def rmsnorm(x, g):
    xf = x.astype(jnp.float32)
    y = xf * lax.rsqrt(jnp.mean(xf * xf, axis=-1, keepdims=True) + RMS_EPS)
    return (y * g.astype(jnp.float32)).astype(x.dtype)


def l2norm(x):
    xf = x.astype(jnp.float32)
    return xf * lax.rsqrt(jnp.sum(xf * xf, axis=-1, keepdims=True) + 1e-6)


def swiglu(x, w_gate, w_up, w_down):
    return (jax.nn.silu(x @ w_gate) * (x @ w_up)) @ w_down


def half_ffn(x, g_pre, g_post, w_gate, w_up, w_down):
    return x + 0.5 * rmsnorm(swiglu(rmsnorm(x, g_pre), w_gate, w_up, w_down), g_post)


def split_cols(t, sizes):
    out, off = [], 0
    for s in sizes:
        out.append(t[..., off:off + s])
        off += s
    return out


def rope(x, pos):
    half = MLA_ROPE // 2
    inv = ROPE_THETA ** (-jnp.arange(half, dtype=jnp.float32) / half)
    ang = pos.astype(jnp.float32)[:, None] * inv[None, :]
    ang = ang.reshape(ang.shape[:1] + (1,) * (x.ndim - 3) + ang.shape[1:])
    c, s = jnp.cos(ang), jnp.sin(ang)
    xf = x.astype(jnp.float32)
    x1, x2 = xf[..., :half], xf[..., half:]
    return jnp.concatenate([x1 * c - x2 * s, x1 * s + x2 * c], axis=-1).astype(x.dtype)


def mla_prompt_attention(q_nope, q_pe, c_kv, k_pe, w_uk, w_uv):
    bsz, seqlen, n_h, _ = q_nope.shape
    k_nope = jnp.einsum('blr,rhn->blhn', c_kv, w_uk)
    v = jnp.einsum('blr,rhv->blhv', c_kv, w_uv)
    q = jnp.concatenate([q_nope, q_pe], axis=-1)
    k = jnp.concatenate([k_nope, jnp.broadcast_to(k_pe[:, :, None, :], (bsz, seqlen, n_h, MLA_ROPE))], axis=-1)
    scale = (MLA_NOPE + MLA_ROPE) ** -0.5
    n_blk = seqlen // Q_BLOCK
    qb = q.reshape(bsz, n_blk, Q_BLOCK, n_h, MLA_NOPE + MLA_ROPE).swapaxes(0, 1)
    kpos = jnp.arange(seqlen)

    def block(args):
        qi, i = args
        s = jnp.einsum('bqhd,bkhd->bhqk', qi, k).astype(jnp.float32) * scale
        qpos = i * Q_BLOCK + jnp.arange(Q_BLOCK)
        s = jnp.where(kpos[None, :] <= qpos[:, None], s, -jnp.inf)
        pr = jax.nn.softmax(s, axis=-1).astype(v.dtype)
        return jnp.einsum('bhqk,bkhv->bqhv', pr, v)

    o = lax.map(block, (qb, jnp.arange(n_blk)))
    return o.swapaxes(0, 1).reshape(bsz, seqlen, n_h, MLA_V)


def mla_sample_attention(q_nope, q_pe, c_kv, k_pe, w_uk, w_uv, pool_ckv, pool_kpe, page_table):
    _, t_new, _, _ = q_nope.shape
    q_lat = jnp.einsum('bthn,rhn->bthr', q_nope, w_uk)
    scale = (MLA_NOPE + MLA_ROPE) ** -0.5
    tpos = jnp.arange(t_new)
    new_mask = tpos[None, :] <= tpos[:, None]
    past_len = page_table.shape[1] * PAGE_SIZE
    mask = jnp.concatenate([jnp.ones((t_new, past_len), dtype=bool), new_mask], axis=1)

    def one(args):
        ql, qp, ckv_new, kpe_new, pages = args
        ckv_all = jnp.concatenate([pool_ckv[pages].reshape(past_len, MLA_KV_RANK), ckv_new.astype(pool_ckv.dtype)], axis=0)
        kpe_all = jnp.concatenate([pool_kpe[pages].reshape(past_len, MLA_ROPE), kpe_new.astype(pool_kpe.dtype)], axis=0)
        s = (jnp.einsum('thr,kr->htk', ql, ckv_all) + jnp.einsum('thp,kp->htk', qp, kpe_all)).astype(jnp.float32) * scale
        s = jnp.where(mask[None], s, -jnp.inf)
        pr = jax.nn.softmax(s, axis=-1).astype(ckv_all.dtype)
        return jnp.einsum('htk,kr->thr', pr, ckv_all)

    o_lat = lax.map(one, (q_lat, q_pe, c_kv, k_pe, page_table))
    return jnp.einsum('bthr,rhv->bthv', o_lat, w_uv)


def causal_dwconv(xin, buf, w):
    seqlen = xin.shape[1]
    xp = jnp.concatenate([buf.astype(xin.dtype), xin], axis=1)
    out = xp[:, 0:seqlen] * w[0]
    for j in range(1, GDN_CONV):
        out = out + xp[:, j:j + seqlen] * w[j]
    return jax.nn.silu(out), xp[:, -(GDN_CONV - 1):]


def gated_delta_chunked(q, k, v, g, beta, s0):
    bsz, seqlen, n_h, _ = q.shape
    c = GDN_CHUNK if seqlen % GDN_CHUNK == 0 else seqlen
    n_c = seqlen // c

    def chunks(t):
        return t.reshape((bsz, n_c, c) + t.shape[2:]).swapaxes(2, 3)

    qc, kc, vc, gc, bc = chunks(q), chunks(k), chunks(v), chunks(g), chunks(beta)
    gcum = jnp.cumsum(gc, axis=-1)
    idx = jnp.arange(c)
    lower_incl = idx[:, None] >= idx[None, :]
    strict = idx[:, None] > idx[None, :]
    decay = jnp.exp(jnp.where(lower_incl, gcum[..., :, None] - gcum[..., None, :], -jnp.inf))
    kk = jnp.einsum('bnhid,bnhjd->bnhij', kc, kc)
    lmat = jnp.where(strict, bc[..., None] * kk * decay, 0.0)
    amat = lmat + jnp.eye(c, dtype=lmat.dtype)
    rhs = jnp.concatenate([vc * bc[..., None], kc * (bc * jnp.exp(gcum))[..., None]], axis=-1)
    sol = lax.linalg.triangular_solve(amat, rhs, left_side=True, lower=True, unit_diagonal=True)
    u, w = sol[..., :GDN_DV], sol[..., GDN_DV:]
    qk = jnp.einsum('bnhid,bnhjd->bnhij', qc, kc) * decay
    qg = qc * jnp.exp(gcum)[..., None]
    kdec = kc * jnp.exp(gcum[..., -1:] - gcum)[..., None]
    glast = jnp.exp(gcum[..., -1])

    def step(s, xs):
        u_n, w_n, qk_n, qg_n, kd_n, gl_n = xs
        v_new = u_n - jnp.einsum('bhck,bhkv->bhcv', w_n, s)
        o = jnp.einsum('bhck,bhkv->bhcv', qg_n, s) + jnp.einsum('bhij,bhjv->bhiv', qk_n, v_new)
        s = s * gl_n[..., None, None] + jnp.einsum('bhck,bhcv->bhkv', kd_n, v_new)
        return s, o

    xs = tuple(jnp.moveaxis(t, 1, 0) for t in (u, w, qk, qg, kdec, glast))
    s_fin, o = lax.scan(step, s0.astype(jnp.float32), xs)
    o = jnp.transpose(o, (1, 0, 3, 2, 4)).reshape(bsz, seqlen, n_h, GDN_DV)
    return o, s_fin


def gdn_mixer(qkv, z, b, a, conv_w, a_log, dt_bias, g_norm, conv_buf, s0):
    bsz, seqlen, _ = qkv.shape
    qkv, new_buf = causal_dwconv(qkv, conv_buf, conv_w)
    q, k, v = split_cols(qkv, (GDN_HEADS * GDN_DK, GDN_HEADS * GDN_DK, GDN_HEADS * GDN_DV))
    q = l2norm(q.reshape(bsz, seqlen, GDN_HEADS, GDN_DK)) * (GDN_DK ** -0.5)
    k = l2norm(k.reshape(bsz, seqlen, GDN_HEADS, GDN_DK))
    v = v.reshape(bsz, seqlen, GDN_HEADS, GDN_DV).astype(jnp.float32)
    beta = jax.nn.sigmoid(b.astype(jnp.float32))
    g = -jnp.exp(a_log.astype(jnp.float32)) * jax.nn.softplus(a.astype(jnp.float32) + dt_bias.astype(jnp.float32))
    o, s_fin = gated_delta_chunked(q, k, v, g, beta, s0)
    o = rmsnorm(o, g_norm) * jax.nn.silu(z.reshape(bsz, seqlen, GDN_HEADS, GDN_DV).astype(jnp.float32))
    return o.reshape(bsz, seqlen, GDN_HEADS * GDN_DV), s_fin, new_buf


def s5_mixer(u, a_re, a_im, log_dt, b_re, b_im, c_re, c_im, d, w_glu, b_glu, h0_re, h0_im):
    bsz, seqlen, _ = u.shape
    f32 = jnp.float32
    uf = u.astype(f32)
    ug = uf.reshape(bsz, seqlen, S5_GROUPS, S5_GROUP)
    a_re, a_im = a_re.astype(f32), a_im.astype(f32)
    dt = jnp.exp(log_dt.astype(f32))[:, None]
    mag = jnp.exp(a_re * dt)
    ab_re, ab_im = mag * jnp.cos(a_im * dt), mag * jnp.sin(a_im * dt)
    den = a_re * a_re + a_im * a_im
    nr, ni = ab_re - 1.0, ab_im
    coef_re = (nr * a_re + ni * a_im) / den
    coef_im = (ni * a_re - nr * a_im) / den
    b_re, b_im = b_re.astype(f32), b_im.astype(f32)
    bb_re = coef_re[..., None] * b_re - coef_im[..., None] * b_im
    bb_im = coef_re[..., None] * b_im + coef_im[..., None] * b_re
    bu_re = jnp.einsum('blgp,gnp->blgn', ug, bb_re)
    bu_im = jnp.einsum('blgp,gnp->blgn', ug, bb_im)
    h0_re, h0_im = h0_re.astype(f32), h0_im.astype(f32)
    bu_re = bu_re.at[:, 0].add(ab_re * h0_re - ab_im * h0_im)
    bu_im = bu_im.at[:, 0].add(ab_re * h0_im + ab_im * h0_re)
    ar = jnp.broadcast_to(ab_re, bu_re.shape)
    ai = jnp.broadcast_to(ab_im, bu_re.shape)

    def combine(e1, e2):
        a1r, a1i, b1r, b1i = e1
        a2r, a2i, b2r, b2i = e2
        return (a2r * a1r - a2i * a1i, a2r * a1i + a2i * a1r,
                a2r * b1r - a2i * b1i + b2r, a2r * b1i + a2i * b1r + b2i)

    _, _, h_re, h_im = lax.associative_scan(combine, (ar, ai, bu_re, bu_im), axis=1)
    y = jnp.einsum('gpn,blgn->blgp', c_re.astype(f32), h_re) - jnp.einsum('gpn,blgn->blgp', c_im.astype(f32), h_im)
    y = y.reshape(bsz, seqlen, S5_WIDTH) + d.astype(f32) * uf
    zg = jax.nn.gelu(y)
    out = zg * jax.nn.sigmoid(zg @ w_glu.astype(f32) + b_glu.astype(f32))
    return out, h_re[:, -1], h_im[:, -1]


def mem_xattn(h, mem_k, mem_v, w_q, w_o):
    bsz, seqlen, _ = h.shape
    q = (h @ w_q).reshape(bsz, seqlen, XA_HEADS, XA_HEAD_DIM)
    s = jnp.einsum('blhd,bmhd->bhlm', q, mem_k).astype(jnp.float32) * (XA_HEAD_DIM ** -0.5)
    pr = jax.nn.softmax(s, axis=-1).astype(mem_v.dtype)
    o = jnp.einsum('bhlm,bmhd->blhd', pr, mem_v).reshape(bsz, seqlen, XA_HEADS * XA_HEAD_DIM)
    return o @ w_o


def layer_forward(x, pos, p, attn_core, conv_buf, gdn_s0, s5_h0_re, s5_h0_im, mem_k, mem_v):
    bsz, seqlen, _ = x.shape
    x = half_ffn(x, p["ffn1_g_pre"], p["ffn1_g_post"], p["ffn1_w_gate"], p["ffn1_w_up"], p["ffn1_w_down"])
    u = rmsnorm(x, p["mix_g_pre"])
    cq, ckv, kpe, qkv, z, b, a, su = split_cols(u @ p["w_in"], IN_SPLITS)
    q = (rmsnorm(cq, p["mla_g_q"]) @ p["mla_w_uq"]).reshape(bsz, seqlen, MLA_HEADS, MLA_NOPE + MLA_ROPE)
    q_nope = q[..., :MLA_NOPE]
    q_pe = rope(q[..., MLA_NOPE:], pos)
    c_kv = rmsnorm(ckv, p["mla_g_kv"])
    k_pe = rope(kpe, pos)
    o_mla = attn_core(q_nope, q_pe, c_kv, k_pe, p["mla_w_uk"], p["mla_w_uv"]).reshape(bsz, seqlen, MLA_HEADS * MLA_V)
    o_gdn, gdn_s, conv_new = gdn_mixer(qkv, z, b, a, p["gdn_conv_w"], p["gdn_a_log"], p["gdn_dt_bias"],
                                       p["gdn_g_norm"], conv_buf, gdn_s0)
    o_s5, h_re, h_im = s5_mixer(su, p["s5_a_re"], p["s5_a_im"], p["s5_log_dt"], p["s5_b_re"], p["s5_b_im"],
                                p["s5_c_re"], p["s5_c_im"], p["s5_d"], p["s5_w_glu"], p["s5_b_glu"], s5_h0_re, s5_h0_im)
    mixed = jnp.concatenate([o_mla, o_gdn, o_s5], axis=-1) @ p["w_out"]
    x = x + rmsnorm(mixed, p["mix_g_post"]).astype(x.dtype)
    xa = mem_xattn(rmsnorm(x, p["xa_g_pre"]), mem_k, mem_v, p["xa_w_q"], p["xa_w_o"])
    x = x + rmsnorm(xa, p["xa_g_post"]).astype(x.dtype)
    x = half_ffn(x, p["ffn2_g_pre"], p["ffn2_g_post"], p["ffn2_w_gate"], p["ffn2_w_up"], p["ffn2_w_down"])
    return x, (c_kv, k_pe, gdn_s, conv_new, h_re, h_im)


def setup_inputs(seed: int = 0) -> dict:
    key = jax.random.key(seed)
    ks = iter(jax.random.split(key, 80))
    f32 = jnp.float32

    def nrm(shape, scale=1.0):
        return jax.random.normal(next(ks), shape, f32) * scale

    def gain(n):
        return 1.0 + 0.01 * nrm((DEPTH, n))

    def log_uniform(shape, lo, hi):
        return jax.random.uniform(next(ks), shape, f32, math.log(lo), math.log(hi))

    n_pages = PAST_LEN // PAGE_SIZE
    n_pool = (5 * DEC_BATCH * n_pages) // 4
    inp = {}
    inp["x_prompt"] = nrm((BATCH, SEQ, D_MODEL))
    inp["x_sample"] = nrm((DEC_BATCH, DEC_SEQ, D_MODEL))
    inp["mem_prompt"] = nrm((BATCH, N_MEM, D_MODEL))
    inp["cache_ckv"] = nrm((DEPTH, n_pool, PAGE_SIZE, MLA_KV_RANK))
    inp["cache_kpe"] = nrm((DEPTH, n_pool, PAGE_SIZE, MLA_ROPE))
    inp["page_table"] = jax.random.permutation(next(ks), n_pool)[:DEC_BATCH * n_pages].reshape(DEC_BATCH, n_pages).astype(jnp.int32)
    inp["cache_mem_k"] = nrm((DEPTH, DEC_BATCH, N_MEM, XA_HEADS, XA_HEAD_DIM))
    inp["cache_mem_v"] = nrm((DEPTH, DEC_BATCH, N_MEM, XA_HEADS, XA_HEAD_DIM))
    inp["state_gdn"] = nrm((DEPTH, DEC_BATCH, GDN_HEADS, GDN_DK, GDN_DV), 0.1)
    inp["state_gdn_conv"] = nrm((DEPTH, DEC_BATCH, GDN_CONV - 1, GDN_CONV_DIM))
    inp["state_s5_re"] = nrm((DEPTH, DEC_BATCH, S5_GROUPS, S5_STATE), 0.1)
    inp["state_s5_im"] = nrm((DEPTH, DEC_BATCH, S5_GROUPS, S5_STATE), 0.1)
    for name in ("ffn1",):
        inp[name + "_g_pre"] = gain(D_MODEL)
        inp[name + "_g_post"] = gain(D_MODEL)
        inp[name + "_w_gate"] = nrm((DEPTH, D_MODEL, D_FF), D_MODEL ** -0.5)
        inp[name + "_w_up"] = nrm((DEPTH, D_MODEL, D_FF), D_MODEL ** -0.5)
        inp[name + "_w_down"] = nrm((DEPTH, D_FF, D_MODEL), D_FF ** -0.5)
    inp["mix_g_pre"] = gain(D_MODEL)
    inp["mix_g_post"] = gain(D_MODEL)
    inp["w_in"] = nrm((DEPTH, D_MODEL, D_IN), D_MODEL ** -0.5)
    inp["w_out"] = nrm((DEPTH, D_MIX, D_MODEL), D_MIX ** -0.5)
    inp["mla_g_q"] = gain(MLA_Q_RANK)
    inp["mla_w_uq"] = nrm((DEPTH, MLA_Q_RANK, MLA_HEADS * (MLA_NOPE + MLA_ROPE)), MLA_Q_RANK ** -0.5)
    inp["mla_g_kv"] = gain(MLA_KV_RANK)
    inp["mla_w_uk"] = nrm((DEPTH, MLA_KV_RANK, MLA_HEADS, MLA_NOPE), MLA_KV_RANK ** -0.5)
    inp["mla_w_uv"] = nrm((DEPTH, MLA_KV_RANK, MLA_HEADS, MLA_V), MLA_KV_RANK ** -0.5)
    inp["gdn_conv_w"] = nrm((DEPTH, GDN_CONV, GDN_CONV_DIM), GDN_CONV ** -0.5)
    inp["gdn_a_log"] = jnp.log(jax.random.uniform(next(ks), (DEPTH, GDN_HEADS), f32, 1.0, 16.0))
    dt = jnp.exp(log_uniform((DEPTH, GDN_HEADS), 1e-3, 1e-1))
    inp["gdn_dt_bias"] = dt + jnp.log(-jnp.expm1(-dt))
    inp["gdn_g_norm"] = gain(GDN_DV)
    inp["s5_a_re"] = -0.5 + 0.01 * nrm((DEPTH, S5_GROUPS, S5_STATE))
    inp["s5_a_im"] = math.pi * jnp.arange(S5_STATE, dtype=f32) + 0.01 * nrm((DEPTH, S5_GROUPS, S5_STATE))
    inp["s5_log_dt"] = log_uniform((DEPTH, S5_GROUPS), 1e-3, 1e-1)
    inp["s5_b_re"] = nrm((DEPTH, S5_GROUPS, S5_STATE, S5_GROUP), (2 * S5_GROUP) ** -0.5)
    inp["s5_b_im"] = nrm((DEPTH, S5_GROUPS, S5_STATE, S5_GROUP), (2 * S5_GROUP) ** -0.5)
    inp["s5_c_re"] = nrm((DEPTH, S5_GROUPS, S5_GROUP, S5_STATE), (2 * S5_STATE) ** -0.5)
    inp["s5_c_im"] = nrm((DEPTH, S5_GROUPS, S5_GROUP, S5_STATE), (2 * S5_STATE) ** -0.5)
    inp["s5_d"] = nrm((DEPTH, S5_WIDTH))
    inp["s5_w_glu"] = nrm((DEPTH, S5_WIDTH, S5_WIDTH), S5_WIDTH ** -0.5)
    inp["s5_b_glu"] = nrm((DEPTH, S5_WIDTH), 0.01)
    inp["xa_g_pre"] = gain(D_MODEL)
    inp["xa_g_post"] = gain(D_MODEL)
    inp["xa_w_q"] = nrm((DEPTH, D_MODEL, XA_HEADS * XA_HEAD_DIM), D_MODEL ** -0.5)
    inp["xa_w_k"] = nrm((DEPTH, D_MODEL, XA_HEADS * XA_HEAD_DIM), D_MODEL ** -0.5)
    inp["xa_w_v"] = nrm((DEPTH, D_MODEL, XA_HEADS * XA_HEAD_DIM), D_MODEL ** -0.5)
    inp["xa_w_o"] = nrm((DEPTH, XA_HEADS * XA_HEAD_DIM, D_MODEL), D_MODEL ** -0.5)
    for name in ("ffn2",):
        inp[name + "_g_pre"] = gain(D_MODEL)
        inp[name + "_g_post"] = gain(D_MODEL)
        inp[name + "_w_gate"] = nrm((DEPTH, D_MODEL, D_FF), D_MODEL ** -0.5)
        inp[name + "_w_up"] = nrm((DEPTH, D_MODEL, D_FF), D_MODEL ** -0.5)
        inp[name + "_w_down"] = nrm((DEPTH, D_FF, D_MODEL), D_FF ** -0.5)
    return inp


def reference(x_prompt, x_sample, mem_prompt, cache_ckv, cache_kpe, page_table, cache_mem_k, cache_mem_v,
              state_gdn, state_gdn_conv, state_s5_re, state_s5_im,
              ffn1_g_pre, ffn1_g_post, ffn1_w_gate, ffn1_w_up, ffn1_w_down,
              mix_g_pre, mix_g_post, w_in, w_out,
              mla_g_q, mla_w_uq, mla_g_kv, mla_w_uk, mla_w_uv,
              gdn_conv_w, gdn_a_log, gdn_dt_bias, gdn_g_norm,
              s5_a_re, s5_a_im, s5_log_dt, s5_b_re, s5_b_im, s5_c_re, s5_c_im, s5_d, s5_w_glu, s5_b_glu,
              xa_g_pre, xa_g_post, xa_w_q, xa_w_k, xa_w_v, xa_w_o,
              ffn2_g_pre, ffn2_g_post, ffn2_w_gate, ffn2_w_up, ffn2_w_down):
    bsz, seqlen, _ = x_prompt.shape
    dec_b, dec_t, _ = x_sample.shape
    past_len = page_table.shape[1] * PAGE_SIZE
    pos_p = jnp.arange(seqlen)
    pos_s = past_len + jnp.arange(dec_t)
    f32 = jnp.float32
    zero_conv = jnp.zeros((bsz, GDN_CONV - 1, GDN_CONV_DIM), x_prompt.dtype)
    zero_gdn = jnp.zeros((bsz, GDN_HEADS, GDN_DK, GDN_DV), f32)
    zero_s5 = jnp.zeros((bsz, S5_GROUPS, S5_STATE), f32)
    yp, ys = x_prompt, x_sample
    prompt_states, sample_states, mem_ks, mem_vs = [], [], [], []
    for l in range(DEPTH):
        p = {
            "ffn1_g_pre": ffn1_g_pre[l], "ffn1_g_post": ffn1_g_post[l], "ffn1_w_gate": ffn1_w_gate[l],
            "ffn1_w_up": ffn1_w_up[l], "ffn1_w_down": ffn1_w_down[l],
            "mix_g_pre": mix_g_pre[l], "mix_g_post": mix_g_post[l], "w_in": w_in[l], "w_out": w_out[l],
            "mla_g_q": mla_g_q[l], "mla_w_uq": mla_w_uq[l], "mla_g_kv": mla_g_kv[l],
            "mla_w_uk": mla_w_uk[l], "mla_w_uv": mla_w_uv[l],
            "gdn_conv_w": gdn_conv_w[l], "gdn_a_log": gdn_a_log[l], "gdn_dt_bias": gdn_dt_bias[l],
            "gdn_g_norm": gdn_g_norm[l],
            "s5_a_re": s5_a_re[l], "s5_a_im": s5_a_im[l], "s5_log_dt": s5_log_dt[l], "s5_b_re": s5_b_re[l],
            "s5_b_im": s5_b_im[l], "s5_c_re": s5_c_re[l], "s5_c_im": s5_c_im[l], "s5_d": s5_d[l],
            "s5_w_glu": s5_w_glu[l], "s5_b_glu": s5_b_glu[l],
            "xa_g_pre": xa_g_pre[l], "xa_g_post": xa_g_post[l], "xa_w_q": xa_w_q[l], "xa_w_o": xa_w_o[l],
            "ffn2_g_pre": ffn2_g_pre[l], "ffn2_g_post": ffn2_g_post[l], "ffn2_w_gate": ffn2_w_gate[l],
            "ffn2_w_up": ffn2_w_up[l], "ffn2_w_down": ffn2_w_down[l],
        }
        n_mem = mem_prompt.shape[1]
        mk = (mem_prompt @ xa_w_k[l]).reshape(bsz, n_mem, XA_HEADS, XA_HEAD_DIM)
        mv = (mem_prompt @ xa_w_v[l]).reshape(bsz, n_mem, XA_HEADS, XA_HEAD_DIM)
        mem_ks.append(mk)
        mem_vs.append(mv)
        yp, st_p = layer_forward(yp, pos_p, p, mla_prompt_attention, zero_conv, zero_gdn, zero_s5, zero_s5, mk, mv)
        prompt_states.append(st_p)
        sample_core = functools.partial(mla_sample_attention, pool_ckv=cache_ckv[l], pool_kpe=cache_kpe[l],
                                        page_table=page_table)
        ys, st_s = layer_forward(ys, pos_s, p, sample_core, state_gdn_conv[l], state_gdn[l],
                                 state_s5_re[l], state_s5_im[l], cache_mem_k[l], cache_mem_v[l])
        sample_states.append(st_s)
    p_ckv, p_kpe, p_gdn, p_conv, p_s5_re, p_s5_im = [jnp.stack(c) for c in zip(*prompt_states)]
    s_ckv, s_kpe, s_gdn, s_conv, s_s5_re, s_s5_im = [jnp.stack(c) for c in zip(*sample_states)]
    p_mem_k = jnp.stack(mem_ks)
    p_mem_v = jnp.stack(mem_vs)
    y_prompt = yp.astype(x_prompt.dtype)
    y_sample = ys.astype(x_sample.dtype)
    return (y_prompt, y_sample, p_ckv, p_kpe, p_gdn, p_conv, p_s5_re, p_s5_im, p_mem_k, p_mem_v,
            s_ckv, s_kpe, s_gdn, s_conv, s_s5_re, s_s5_im)
```

```python
import functools
import math

import jax
import jax.numpy as jnp
import numpy as np
from jax import lax
from jax.experimental import pallas as pl
from jax.experimental.pallas import tpu as pltpu

F32 = jnp.float32
BF16 = jnp.bfloat16

RMS_EPS = 1e-6
MLA_HEADS = 8
MLA_NOPE = 64
MLA_ROPE = 32
MLA_V = 64
MLA_Q_RANK = 384
MLA_KV_RANK = 256
ROPE_THETA = 10000.0
HEAD_BLOCK = 128
PAGE_SIZE = 128
GDN_HEADS = 4
GDN_DK = 64
GDN_DV = 64
GDN_CONV = 4
GDN_CHUNK = 64
GDN_QKV = GDN_HEADS * (2 * GDN_DK + GDN_DV)
S5_GROUPS = 16
S5_GROUP = 16
S5_STATE = 64
S5_WIDTH = S5_GROUPS * S5_GROUP
S5_LANES = S5_GROUPS * S5_STATE
XA_HEADS = 4
XA_HEAD_DIM = 256

VMEM_LIMIT_BYTES = 56 * 1024 * 1024

MLA_SCALE = (MLA_NOPE + MLA_ROPE) ** -0.5
XA_SCALE = XA_HEAD_DIM ** -0.5


def _cparams(*sem):
    return pltpu.CompilerParams(dimension_semantics=tuple(sem), vmem_limit_bytes=VMEM_LIMIT_BYTES)


def _rms(x, g):
    return x * lax.rsqrt(jnp.mean(x * x, axis=-1, keepdims=True) + RMS_EPS) * g


def _bdot(a, b):
    return jnp.dot(a.astype(BF16), b.astype(BF16), preferred_element_type=F32)


def _bdot_nt(a, b):
    return lax.dot_general(a.astype(BF16), b.astype(BF16), (((1,), (1,)), ((), ())),
                           preferred_element_type=F32)


def _split3(a):
    hi = a.astype(BF16)
    r1 = a - hi.astype(F32)
    mid = r1.astype(BF16)
    lo = (r1 - mid.astype(F32)).astype(BF16)
    return hi, mid, lo


def _dot_f32(a, b):
    a0, a1, a2 = _split3(a)
    b0, b1, b2 = _split3(b)
    d = functools.partial(jnp.dot, preferred_element_type=F32)
    return (d(a0, b0) + (d(a0, b1) + d(a1, b0)) + (d(a0, b2) + d(a1, b1) + d(a2, b0)))


def _silu(x):
    return x * jax.nn.sigmoid(x)


def _softplus(x):
    return jnp.maximum(x, 0.0) + jnp.log(1.0 + jnp.exp(-jnp.abs(x)))


def _ffn_kernel(x_ref, gpre_ref, gpost_ref, wg_ref, wu_ref, wd_ref, o_ref, xn_ref, acc_ref):
    j = pl.program_id(1)

    @pl.when(j == 0)
    def _():
        xn_ref[...] = _rms(x_ref[...], gpre_ref[...]).astype(BF16)
        acc_ref[...] = jnp.zeros_like(acc_ref)

    xn = xn_ref[...]
    g = jnp.dot(xn, wg_ref[...], preferred_element_type=F32)
    u = jnp.dot(xn, wu_ref[...], preferred_element_type=F32)
    h = (_silu(g) * u).astype(BF16)
    acc_ref[...] += jnp.dot(h, wd_ref[...], preferred_element_type=F32)

    @pl.when(j == pl.num_programs(1) - 1)
    def _():
        o_ref[...] = x_ref[...] + 0.5 * _rms(acc_ref[...], gpost_ref[...])


def _half_ffn(x, g_pre, g_post, wg, wu, wd):
    rows, d = x.shape
    f = wg.shape[1]
    tm = min(512, rows)
    tf = f // 2
    return pl.pallas_call(
        _ffn_kernel,
        grid=(rows // tm, f // tf),
        in_specs=[
            pl.BlockSpec((tm, d), lambda i, j: (i, 0)),
            pl.BlockSpec((1, d), lambda i, j: (0, 0)),
            pl.BlockSpec((1, d), lambda i, j: (0, 0)),
            pl.BlockSpec((d, tf), lambda i, j: (0, j)),
            pl.BlockSpec((d, tf), lambda i, j: (0, j)),
            pl.BlockSpec((tf, d), lambda i, j: (j, 0)),
        ],
        out_specs=pl.BlockSpec((tm, d), lambda i, j: (i, 0)),
        out_shape=jax.ShapeDtypeStruct((rows, d), F32),
        scratch_shapes=[pltpu.VMEM((tm, d), BF16), pltpu.VMEM((tm, d), F32)],
        compiler_params=_cparams("parallel", "arbitrary"),
        name="half_ffn",
    )(x, g_pre, g_post, wg, wu, wd)


C_CQ = (0, 384)
C_CKV = (384, 640)
C_KPL = (640, 768)
C_KPLS = (768, 896)
C_MISC = (896, 1024)
C_QKV = (1024, 1792)
C_Z = (1792, 2048)
C_SU = (2048, 2304)
W_BIG = 2304


def _mixprep_body(sample, x_ref, g_ref, wbig_ref, gq_ref, wqa_ref, wqb_ref, gkv_ref, tc_ref, ts_ref,
                  wk_ref, wv_ref, ckv_ref, kpe_ref, qkv_ref, z_ref, misc_ref, su_ref, a_ref, b_ref, c_ref):
    u = _rms(x_ref[...], g_ref[...]).astype(BF16)
    y = jnp.dot(u, wbig_ref[...], preferred_element_type=F32)
    sl = lambda c: y[:, c[0]:c[1]]
    tc = tc_ref[...]
    ts = ts_ref[...]
    kpe_pl = sl(C_KPL) * tc + sl(C_KPLS) * ts
    kpe_ref[...] = kpe_pl[:, MLA_NOPE:MLA_NOPE + MLA_ROPE]
    ckv_n = _rms(sl(C_CKV), gkv_ref[...])
    ckv_ref[...] = ckv_n
    qkv_ref[...] = sl(C_QKV)
    z_ref[...] = sl(C_Z)
    misc_ref[...] = sl(C_MISC)
    su_ref[...] = sl(C_SU)

    cqn = _rms(sl(C_CQ), gq_ref[...]).astype(BF16)
    qa = jnp.dot(cqn, wqa_ref[...], preferred_element_type=F32)
    qb = jnp.dot(cqn, wqb_ref[...], preferred_element_type=F32)
    lane = lax.broadcasted_iota(jnp.int32, (1, HEAD_BLOCK), 1)
    qmul = jnp.where(lane < MLA_NOPE, 1.0, 0.0) + tc
    ckb = ckv_n.astype(BF16)
    if sample:
        qlat_ref, qpe_ref = a_ref, b_ref
        for h in range(MLA_HEADS):
            hs = slice(h * HEAD_BLOCK, (h + 1) * HEAD_BLOCK)
            qh = qa[:, hs] * qmul + qb[:, hs] * ts
            qlat_ref[h] = jnp.dot(qa[:, hs].astype(BF16), wk_ref[h], preferred_element_type=F32) * MLA_SCALE
            qpe_ref[h] = qh[:, MLA_NOPE:MLA_NOPE + MLA_ROPE] * MLA_SCALE
    else:
        q_ref, k_ref, v_ref = a_ref, b_ref, c_ref
        kn = jnp.dot(ckb, wk_ref[...], preferred_element_type=F32)
        for h in range(MLA_HEADS):
            hs = slice(h * HEAD_BLOCK, (h + 1) * HEAD_BLOCK)
            qh = qa[:, hs] * qmul + qb[:, hs] * ts
            q_ref[h] = (qh * MLA_SCALE).astype(BF16)
            k_ref[h] = (kn[:, hs] + kpe_pl).astype(BF16)
        v_ref[...] = jnp.dot(ckb, wv_ref[...], preferred_element_type=F32).astype(BF16)


def _mixprep_prompt_kernel(x_ref, g_ref, wbig_ref, gq_ref, wqa_ref, wqb_ref, gkv_ref, tc_ref, ts_ref,
                           wk_ref, wv_ref, ckv_ref, kpe_ref, qkv_ref, z_ref, misc_ref, su_ref,
                           q_ref, k_ref, v_ref):
    _mixprep_body(False, x_ref, g_ref, wbig_ref, gq_ref, wqa_ref, wqb_ref, gkv_ref, tc_ref, ts_ref,
                  wk_ref, wv_ref, ckv_ref, kpe_ref, qkv_ref, z_ref, misc_ref, su_ref, q_ref, k_ref, v_ref)


def _mixprep_sample_kernel(x_ref, g_ref, wbig_ref, gq_ref, wqa_ref, wqb_ref, gkv_ref, tc_ref, ts_ref,
                           wk_ref, ckv_ref, kpe_ref, qkv_ref, z_ref, misc_ref, su_ref, qlat_ref, qpe_ref):
    _mixprep_body(True, x_ref, g_ref, wbig_ref, gq_ref, wqa_ref, wqb_ref, gkv_ref, tc_ref, ts_ref,
                  wk_ref, None, ckv_ref, kpe_ref, qkv_ref, z_ref, misc_ref, su_ref, qlat_ref, qpe_ref, None)


def _mixprep(sample, x, w, tc, ts):
    rows, d = x.shape
    tm = min(512, rows)
    row = lambda n: pl.BlockSpec((tm, n), lambda i: (i, 0))
    full = lambda a: pl.BlockSpec(a.shape, lambda i: (0,) * a.ndim)
    hrow = lambda n: pl.BlockSpec((MLA_HEADS, tm, n), lambda i: (0, i, 0))
    common_in = [x, w["mix_g_pre"], w["w_big"], w["mla_g_q"], w["wq_a"], w["wq_b"], w["mla_g_kv"], tc, ts]
    common_specs = [row(d)] + [full(a) for a in common_in[1:7]] + [row(HEAD_BLOCK), row(HEAD_BLOCK)]
    common_out = [
        (jax.ShapeDtypeStruct((rows, MLA_KV_RANK), F32), row(MLA_KV_RANK)),
        (jax.ShapeDtypeStruct((rows, MLA_ROPE), F32), row(MLA_ROPE)),
        (jax.ShapeDtypeStruct((rows, GDN_QKV), F32), row(GDN_QKV)),
        (jax.ShapeDtypeStruct((rows, GDN_HEADS * GDN_DV), F32), row(GDN_HEADS * GDN_DV)),
        (jax.ShapeDtypeStruct((rows, HEAD_BLOCK), F32), row(HEAD_BLOCK)),
        (jax.ShapeDtypeStruct((rows, S5_WIDTH), F32), row(S5_WIDTH)),
    ]
    if sample:
        ins = common_in + [w["wuk_t"]]
        specs = common_specs + [full(w["wuk_t"])]
        outs = common_out + [
            (jax.ShapeDtypeStruct((MLA_HEADS, rows, MLA_KV_RANK), F32), hrow(MLA_KV_RANK)),
            (jax.ShapeDtypeStruct((MLA_HEADS, rows, MLA_ROPE), F32), hrow(MLA_ROPE)),
        ]
        body = _mixprep_sample_kernel
    else:
        ins = common_in + [w["wuk_pad"], w["wuv"]]
        specs = common_specs + [full(w["wuk_pad"]), full(w["wuv"])]
        outs = common_out + [
            (jax.ShapeDtypeStruct((MLA_HEADS, rows, HEAD_BLOCK), BF16), hrow(HEAD_BLOCK)),
            (jax.ShapeDtypeStruct((MLA_HEADS, rows, HEAD_BLOCK), BF16), hrow(HEAD_BLOCK)),
            (jax.ShapeDtypeStruct((rows, MLA_HEADS * MLA_V), BF16), row(MLA_HEADS * MLA_V)),
        ]
        body = _mixprep_prompt_kernel
    return pl.pallas_call(
        body,
        grid=(rows // tm,),
        in_specs=specs,
        out_specs=[o[1] for o in outs],
        out_shape=[o[0] for o in outs],
        compiler_params=_cparams("parallel"),
        name="mixprep_sample" if sample else "mixprep_prompt",
    )(*ins)


def _attn_kernel(qi_ref, ki_ref, q_ref, k_ref, v_ref, o_ref, m_ref, l_ref, acc_ref):
    p = pl.program_id(1)
    i = qi_ref[p]
    j = ki_ref[p]
    tq, tk = q_ref.shape[1], k_ref.shape[1]

    @pl.when(j == 0)
    def _():
        m_ref[...] = jnp.full(m_ref.shape, -jnp.inf, F32)
        l_ref[...] = jnp.zeros_like(l_ref)
        acc_ref[...] = jnp.zeros_like(acc_ref)

    def step(masked):
        for hh in range(2):
            s = lax.dot_general(q_ref[hh], k_ref[hh], (((1,), (1,)), ((), ())),
                                preferred_element_type=F32)
            if masked:
                r = lax.broadcasted_iota(jnp.int32, (tq, tk), 0)
                c = lax.broadcasted_iota(jnp.int32, (tq, tk), 1)
                s = jnp.where(c <= r, s, -jnp.inf)
            m_prev = m_ref[hh]
            m_new = jnp.maximum(m_prev, jnp.max(s, axis=-1, keepdims=True))
            alpha = jnp.exp(m_prev - m_new)
            pm = jnp.exp(s - m_new[:, :1])
            l_ref[hh] = alpha * l_ref[hh] + jnp.sum(pm, axis=-1, keepdims=True)
            acc_ref[hh] = alpha * acc_ref[hh] + jnp.dot(pm.astype(BF16), v_ref[...],
                                                        preferred_element_type=F32)
            m_ref[hh] = m_new

    @pl.when(j < i)
    def _():
        step(False)

    @pl.when(j == i)
    def _():
        step(True)
        lane = lax.broadcasted_iota(jnp.int32, (tq, HEAD_BLOCK), 1)
        o0 = acc_ref[0] / l_ref[0]
        o1 = acc_ref[1] / l_ref[1]
        o_ref[...] = jnp.where(lane < MLA_V, o0, o1).astype(o_ref.dtype)


def _prompt_attention(q, k, v):
    rows = q.shape[1]
    t = min(512, rows)
    n = rows // t
    qi = np.array([i for i in range(n) for _ in range(i + 1)], np.int32)
    ki = np.array([j for i in range(n) for j in range(i + 1)], np.int32)
    grid_spec = pltpu.PrefetchScalarGridSpec(
        num_scalar_prefetch=2,
        grid=(MLA_HEADS // 2, len(qi)),
        in_specs=[
            pl.BlockSpec((2, t, HEAD_BLOCK), lambda hp, p, qi, ki: (hp, qi[p], 0)),
            pl.BlockSpec((2, t, HEAD_BLOCK), lambda hp, p, qi, ki: (hp, ki[p], 0)),
            pl.BlockSpec((t, HEAD_BLOCK), lambda hp, p, qi, ki: (ki[p], hp)),
        ],
        out_specs=pl.BlockSpec((t, HEAD_BLOCK), lambda hp, p, qi, ki: (qi[p], hp)),
        scratch_shapes=[pltpu.VMEM((2, t, HEAD_BLOCK), F32)] * 3,
    )
    return pl.pallas_call(
        _attn_kernel,
        grid_spec=grid_spec,
        out_shape=jax.ShapeDtypeStruct((rows, MLA_HEADS * MLA_V), BF16),
        compiler_params=_cparams("parallel", "arbitrary"),
        name="mla_prompt_attention",
    )(jnp.asarray(qi), jnp.asarray(ki), q, k, v)


def _paged_kernel(n_pp, pt_ref, qlat_ref, qpe_ref, cnew_ref, knew_ref, *rest):
    ckv_refs = rest[:n_pp]
    kpe_refs = rest[n_pp:2 * n_pp]
    o_ref, m_ref, l_ref, acc_ref = rest[2 * n_pp:]
    j = pl.program_id(1)
    t_new = cnew_ref.shape[0]
    rows = MLA_HEADS * t_new
    ql = qlat_ref[...].reshape(rows, MLA_KV_RANK).astype(BF16)
    qp = qpe_ref[...].reshape(rows, MLA_ROPE).astype(BF16)

    @pl.when(j == 0)
    def _():
        cn = cnew_ref[...].astype(BF16)
        s = _bdot_nt(ql, cn) + _bdot_nt(qp, knew_ref[...])
        tok = lax.broadcasted_iota(jnp.int32, (rows, t_new), 0) % t_new
        key = lax.broadcasted_iota(jnp.int32, (rows, t_new), 1)
        s = jnp.where(key <= tok, s, -jnp.inf)
        m = jnp.max(s, axis=-1, keepdims=True)
        pm = jnp.exp(s - m)
        m_ref[...] = jnp.broadcast_to(m, m_ref.shape)
        l_ref[...] = jnp.broadcast_to(jnp.sum(pm, axis=-1, keepdims=True), l_ref.shape)
        acc_ref[...] = jnp.dot(pm.astype(BF16), cn, preferred_element_type=F32)

    pages = [r[...].astype(BF16) for r in ckv_refs]
    s = jnp.concatenate(
        [_bdot_nt(ql, pages[k]) + _bdot_nt(qp, kpe_refs[k][...]) for k in range(n_pp)], axis=-1)
    m_prev = m_ref[...]
    m_new = jnp.maximum(m_prev, jnp.max(s, axis=-1, keepdims=True))
    alpha = jnp.exp(m_prev - m_new)
    pm = jnp.exp(s - m_new[:, :1])
    l_ref[...] = alpha * l_ref[...] + jnp.sum(pm, axis=-1, keepdims=True)
    pb = pm.astype(BF16)
    pv = jnp.dot(pb[:, 0:PAGE_SIZE], pages[0], preferred_element_type=F32)
    for k in range(1, n_pp):
        pv += jnp.dot(pb[:, k * PAGE_SIZE:(k + 1) * PAGE_SIZE], pages[k], preferred_element_type=F32)
    acc_ref[...] = alpha[:, :1] * acc_ref[...] + pv
    m_ref[...] = m_new

    @pl.when(j == pl.num_programs(1) - 1)
    def _():
        o = acc_ref[...] / l_ref[...][:, :1]
        o_ref[...] = o.reshape(MLA_HEADS, t_new, MLA_KV_RANK)


def _paged_attention(layer, qlat, qpe, ckv_new, kpe_new, cache_ckv, cache_kpe, page_table, t_new):
    n_b, n_pages = page_table.shape
    n_pp = min(16, n_pages)
    rows = MLA_HEADS * t_new
    pt = page_table.reshape(-1)

    def page_spec(width, k):
        return pl.BlockSpec((None, None, PAGE_SIZE, width),
                            lambda b, j, pt: (layer, pt[b * n_pages + j * n_pp + k], 0, 0))

    grid_spec = pltpu.PrefetchScalarGridSpec(
        num_scalar_prefetch=1,
        grid=(n_b, n_pages // n_pp),
        in_specs=[
            pl.BlockSpec((MLA_HEADS, t_new, MLA_KV_RANK), lambda b, j, pt: (0, b, 0)),
            pl.BlockSpec((MLA_HEADS, t_new, MLA_ROPE), lambda b, j, pt: (0, b, 0)),
            pl.BlockSpec((t_new, MLA_KV_RANK), lambda b, j, pt: (b, 0)),
            pl.BlockSpec((t_new, MLA_ROPE), lambda b, j, pt: (b, 0)),
        ] + [page_spec(MLA_KV_RANK, k) for k in range(n_pp)]
          + [page_spec(MLA_ROPE, k) for k in range(n_pp)],
        out_specs=pl.BlockSpec((MLA_HEADS, t_new, MLA_KV_RANK), lambda b, j, pt: (0, b, 0)),
        scratch_shapes=[pltpu.VMEM((rows, HEAD_BLOCK), F32), pltpu.VMEM((rows, HEAD_BLOCK), F32),
                        pltpu.VMEM((rows, MLA_KV_RANK), F32)],
    )
    return pl.pallas_call(
        functools.partial(_paged_kernel, n_pp),
        grid_spec=grid_spec,
        out_shape=jax.ShapeDtypeStruct((MLA_HEADS, n_b * t_new, MLA_KV_RANK), F32),
        compiler_params=_cparams("parallel", "arbitrary"),
        name="mla_paged_attention",
    )(pt, qlat, qpe, ckv_new, kpe_new, *([cache_ckv] * n_pp), *([cache_kpe] * n_pp))


def _gdn_gates(misc, alog, dtb):
    beta = jax.nn.sigmoid(misc)
    g = -jnp.exp(alog) * _softplus(misc + dtb)
    return beta, g


def _l2n(x, scale):
    return x * (lax.rsqrt(jnp.sum(x * x, axis=-1, keepdims=True) + 1e-6) * scale)


def _gdn_prompt_kernel(qkv_ref, z_ref, misc_ref, convw_ref, alog_ref, dtb_ref, gnorm_ref,
                       o_ref, sfin_ref, s_ref, carry_ref):
    i = pl.program_id(0)

    @pl.when(i == 0)
    def _():
        s_ref[...] = jnp.zeros_like(s_ref)
        carry_ref[...] = jnp.zeros_like(carry_ref)

    x = qkv_ref[...]
    tm = x.shape[0]
    w = convw_ref[...]
    row8 = lax.broadcasted_iota(jnp.int32, (8, GDN_QKV), 0)
    cprev = carry_ref[...]
    acc = x * w[GDN_CONV - 1:GDN_CONV]
    for d in range(1, GDN_CONV):
        xr = pltpu.roll(x, d, 0)
        head = jnp.where(row8 < d, pltpu.roll(cprev, d, 0), xr[0:8])
        xs = jnp.concatenate([head, xr[8:]], axis=0)
        acc = acc + xs * w[GDN_CONV - 1 - d:GDN_CONV - d]
    carry_ref[...] = x[tm - 8:tm]
    conv = _silu(acc)

    nq = GDN_HEADS * GDN_DK
    beta_all, g_all = _gdn_gates(misc_ref[...], alog_ref[...], dtb_ref[...])
    z = z_ref[...]
    gnorm = gnorm_ref[...]
    c = GDN_CHUNK
    blk = 2 * c
    ri = lax.broadcasted_iota(jnp.int32, (blk, blk), 0)
    ci = lax.broadcasted_iota(jnp.int32, (blk, blk), 1)
    tri2 = jnp.where((ri >= ci) & ((ri // c) == (ci // c)), 1.0, 0.0).astype(BF16)
    r64 = lax.broadcasted_iota(jnp.int32, (c, c), 0)
    c64 = lax.broadcasted_iota(jnp.int32, (c, c), 1)
    incl = r64 >= c64
    strict = r64 > c64

    for b2 in range(tm // blk):
        gblk = g_all[b2 * blk:(b2 + 1) * blk]
        g0, g1, g2 = _split3(gblk)
        d = functools.partial(jnp.dot, preferred_element_type=F32)
        gcum = d(tri2, g0) + d(tri2, g1) + d(tri2, g2)
        gcum_t = gcum.T
        for c2 in range(2):
            r0 = b2 * blk + c2 * c
            outs = []
            for h in range(GDN_HEADS):
                qh = _l2n(conv[r0:r0 + c, h * GDN_DK:(h + 1) * GDN_DK], GDN_DK ** -0.5)
                kh = _l2n(conv[r0:r0 + c, nq + h * GDN_DK:nq + (h + 1) * GDN_DK], 1.0)
                vh = conv[r0:r0 + c, 2 * nq + h * GDN_DV:2 * nq + (h + 1) * GDN_DV]
                gc = gcum[c2 * c:(c2 + 1) * c, GDN_HEADS + h:GDN_HEADS + h + 1]
                gr = gcum_t[GDN_HEADS + h:GDN_HEADS + h + 1, c2 * c:(c2 + 1) * c]
                beta = beta_all[r0:r0 + c, h:h + 1]
                decay = jnp.exp(jnp.where(incl, gc - gr, -jnp.inf))
                eg = jnp.exp(gc)
                kk = _bdot_nt(kh, kh)
                lmat = jnp.where(strict, beta * kk * decay, 0.0)
                rhs = jnp.concatenate([vh * beta, kh * (beta * eg)], axis=-1)
                mk = -lmat
                n_sq = int(math.log2(c))
                for kq in range(n_sq):
                    rhs = rhs + _dot_f32(mk, rhs)
                    if kq < n_sq - 1:
                        mk = _dot_f32(mk, mk)
                uu = rhs[:, :GDN_DV]
                ww = rhs[:, GDN_DV:]
                qk = _bdot_nt(qh, kh) * decay
                glast = gc[c - 1:c, :]
                s_h = s_ref[h]
                v_new = uu - _bdot(ww, s_h)
                o_h = _bdot(qh * eg, s_h) + _bdot(qk, v_new)
                kdec = kh * jnp.exp(glast - gc)
                s_ref[h] = s_h * jnp.exp(glast) + lax.dot_general(
                    kdec.astype(BF16), v_new.astype(BF16), (((0,), (0,)), ((), ())),
                    preferred_element_type=F32)
                zh = z[r0:r0 + c, h * GDN_DV:(h + 1) * GDN_DV]
                outs.append(_rms(o_h, gnorm) * _silu(zh))
            o_ref[r0:r0 + c, :] = jnp.concatenate(outs, axis=-1).astype(o_ref.dtype)

    @pl.when(i == pl.num_programs(0) - 1)
    def _():
        sfin_ref[...] = s_ref[...]


def _gdn_prompt(qkv, z, misc, w):
    rows = qkv.shape[0]
    tm = min(256, rows)
    row = lambda n: pl.BlockSpec((tm, n), lambda i: (i, 0))
    full = lambda a: pl.BlockSpec(a.shape, lambda i: (0,) * a.ndim)
    st = (GDN_HEADS, GDN_DK, GDN_DV)
    ins = [qkv, z, misc, w["gdn_conv_w"], w["gdn_alog_pad"], w["gdn_dtb_pad"], w["gdn_g_norm"]]
    return pl.pallas_call(
        _gdn_prompt_kernel,
        grid=(rows // tm,),
        in_specs=[row(GDN_QKV), row(GDN_HEADS * GDN_DV), row(HEAD_BLOCK)] + [full(a) for a in ins[3:]],
        out_specs=[row(GDN_HEADS * GDN_DV), pl.BlockSpec(st, lambda i: (0, 0, 0))],
        out_shape=[jax.ShapeDtypeStruct((rows, GDN_HEADS * GDN_DV), BF16), jax.ShapeDtypeStruct(st, F32)],
        scratch_shapes=[pltpu.VMEM(st, F32), pltpu.VMEM((8, GDN_QKV), F32)],
        compiler_params=_cparams("arbitrary"),
        name="gdn_prompt",
    )(*ins)


def _gdn_sample_kernel(t_new, qkv_ref, st_ref, z_ref, misc_ref, convw_ref, alog_ref, dtb_ref, gcol_ref,
                       s0_ref, o_ref, sout_ref,
                       s_ref, rows_ref, zo_ref, gate_ref, qt_ref, kt_ref, vt_ref, gt_ref, ot_ref):
    n_rows = qkv_ref.shape[0]
    n_b = n_rows // t_new
    nq = GDN_HEADS * GDN_DK
    x = qkv_ref[...]
    st = st_ref[...]
    w = convw_ref[...]
    tpos = lax.broadcasted_iota(jnp.int32, (n_rows, GDN_QKV), 0) % t_new
    acc = x * w[GDN_CONV - 1:GDN_CONV]
    for d in range(1, GDN_CONV):
        xr = pltpu.roll(x, d, 0)
        back = GDN_CONV - 1 - d
        sr = st if back == 0 else pltpu.roll(st, n_rows - back, 0)
        acc = acc + jnp.where(tpos < d, sr, xr) * w[GDN_CONV - 1 - d:GDN_CONV - d]
    conv = _silu(acc)
    parts = []
    for h in range(GDN_HEADS):
        parts.append(_l2n(conv[:, h * GDN_DK:(h + 1) * GDN_DK], GDN_DK ** -0.5))
    for h in range(GDN_HEADS):
        parts.append(_l2n(conv[:, nq + h * GDN_DK:nq + (h + 1) * GDN_DK], 1.0))
    parts.append(conv[:, 2 * nq:])
    feats = jnp.concatenate(parts, axis=-1)
    n_chunk = GDN_QKV // HEAD_BLOCK
    per_part = nq // HEAD_BLOCK
    for cc in range(n_chunk):
        rows_ref[cc] = feats[:, cc * HEAD_BLOCK:(cc + 1) * HEAD_BLOCK]
    zsil = _silu(z_ref[...])
    for cc in range(per_part):
        zo_ref[cc] = zsil[:, cc * HEAD_BLOCK:(cc + 1) * HEAD_BLOCK]
    beta_all, g_all = _gdn_gates(misc_ref[...], alog_ref[...], dtb_ref[...])
    lane = lax.broadcasted_iota(jnp.int32, beta_all.shape, 1)
    gate_ref[...] = jnp.where(lane < GDN_HEADS, beta_all, jnp.exp(g_all))

    for t in range(t_new):
        for cc in range(n_chunk):
            blk_t = rows_ref[cc, pl.ds(t, n_b, stride=t_new), :].T
            dst = (qt_ref, kt_ref, vt_ref)[cc // per_part]
            lo = (cc % per_part) * HEAD_BLOCK
            dst[t, lo:lo + HEAD_BLOCK, :] = blk_t
        gt_ref[t] = gate_ref[pl.ds(t, n_b, stride=t_new), :].T

    n_blk = (GDN_HEADS * GDN_DK * GDN_DV) // HEAD_BLOCK
    per = HEAD_BLOCK // GDN_DV
    for cb in range(n_blk):
        s_ref[per * cb:per * (cb + 1)] = s0_ref[:, cb * HEAD_BLOCK:(cb + 1) * HEAD_BLOCK].T.reshape(
            per, GDN_DV, n_b)

    gcol = gcol_ref[...]
    for t in range(t_new):
        for h in range(GDN_HEADS):
            egr = gt_ref[t, GDN_HEADS + h:GDN_HEADS + h + 1, :]
            betar = gt_ref[t, h:h + 1, :]
            base = h * GDN_DK

            def p1(dk, racc):
                kb = kt_ref[t, pl.ds(base + dk, 1), :]
                return racc + s_ref[base + dk] * kb

            rr = lax.fori_loop(0, GDN_DK, p1, jnp.zeros((GDN_DV, n_b), F32), unroll=8) * egr
            dd = betar * (vt_ref[t, base:base + GDN_DV, :] - rr)

            def p2(dk, oacc):
                kb = kt_ref[t, pl.ds(base + dk, 1), :]
                qb = qt_ref[t, pl.ds(base + dk, 1), :]
                sn = s_ref[base + dk] * egr + kb * dd
                s_ref[base + dk] = sn
                return oacc + sn * qb

            oo = lax.fori_loop(0, GDN_DK, p2, jnp.zeros((GDN_DV, n_b), F32), unroll=8)
            on = oo * lax.rsqrt(jnp.mean(oo * oo, axis=0, keepdims=True) + RMS_EPS) * gcol
            ot_ref[base:base + GDN_DV, :] = on
        on_rows = ot_ref[...].T
        for cc in range(per_part):
            zt = zo_ref[cc, pl.ds(t, n_b, stride=t_new), :]
            zo_ref[cc, pl.ds(t, n_b, stride=t_new), :] = on_rows[:, cc * HEAD_BLOCK:(cc + 1) * HEAD_BLOCK] * zt
    for cc in range(per_part):
        o_ref[:, cc * HEAD_BLOCK:(cc + 1) * HEAD_BLOCK] = zo_ref[cc]

    for cb in range(n_blk):
        sout_ref[:, cb * HEAD_BLOCK:(cb + 1) * HEAD_BLOCK] = s_ref[per * cb:per * (cb + 1)].reshape(
            HEAD_BLOCK, n_b).T


def _gdn_sample(qkv, st_rows, z, misc, w, s0, t_new):
    rows = qkv.shape[0]
    n_b = rows // t_new
    feat = GDN_HEADS * GDN_DK
    n_state = GDN_HEADS * GDN_DK * GDN_DV
    ins = [qkv, st_rows, z, misc, w["gdn_conv_w"], w["gdn_alog_pad"], w["gdn_dtb_pad"], w["gdn_g_col"], s0]
    return pl.pallas_call(
        functools.partial(_gdn_sample_kernel, t_new),
        out_shape=[jax.ShapeDtypeStruct((rows, GDN_HEADS * GDN_DV), F32),
                   jax.ShapeDtypeStruct((n_b, n_state), F32)],
        scratch_shapes=[
            pltpu.VMEM((GDN_HEADS * GDN_DK, GDN_DV, n_b), F32),
            pltpu.VMEM((GDN_QKV // HEAD_BLOCK, rows, HEAD_BLOCK), F32),
            pltpu.VMEM((GDN_HEADS * GDN_DV // HEAD_BLOCK, rows, HEAD_BLOCK), F32),
            pltpu.VMEM((rows, HEAD_BLOCK), F32),
            pltpu.VMEM((t_new, feat, n_b), F32),
            pltpu.VMEM((t_new, feat, n_b), F32),
            pltpu.VMEM((t_new, GDN_HEADS * GDN_DV, n_b), F32),
            pltpu.VMEM((t_new, HEAD_BLOCK, n_b), F32),
            pltpu.VMEM((GDN_HEADS * GDN_DV, n_b), F32),
        ],
        compiler_params=pltpu.CompilerParams(vmem_limit_bytes=VMEM_LIMIT_BYTES),
        name="gdn_sample",
    )(*ins)


def _cmul(ar, ai, br, bi):
    return ar * br - ai * bi, ar * bi + ai * br


def _s5_kernel(per_group_state, u_ref, bbr_ref, bbi_ref, ar_ref, ai_ref, cr_ref, ci_ref, d_ref,
               wglu_ref, bglu_ref, h0r_ref, h0i_ref, o_ref, hr_out_ref, hi_out_ref,
               br_ref, bi_ref, cr_carry_ref, ci_carry_ref):
    i = pl.program_id(0)
    tm = u_ref.shape[0]
    n_grp = tm // 8

    if not per_group_state:
        @pl.when(i == 0)
        def _():
            cr_carry_ref[...] = jnp.zeros_like(cr_carry_ref)
            ci_carry_ref[...] = jnp.zeros_like(ci_carry_ref)

    u = u_ref[...]
    ub = u.astype(BF16)
    br_ref[...] = jnp.dot(ub, bbr_ref[...], preferred_element_type=F32)
    bi_ref[...] = jnp.dot(ub, bbi_ref[...], preferred_element_type=F32)

    ar = ar_ref[...]
    ai = ai_ref[...]
    p1 = (ar, ai)
    p2 = _cmul(*p1, *p1)
    p3 = _cmul(*p2, *p1)
    p4 = _cmul(*p2, *p2)
    p5 = _cmul(*p4, *p1)
    p6 = _cmul(*p4, *p2)
    p7 = _cmul(*p4, *p3)
    p8 = _cmul(*p4, *p4)
    row8 = lax.broadcasted_iota(jnp.int32, (8, S5_LANES), 0)
    pw_r = jnp.zeros((8, S5_LANES), F32)
    pw_i = jnp.zeros((8, S5_LANES), F32)
    for t, pw in enumerate((p1, p2, p3, p4, p5, p6, p7, p8)):
        pw_r = jnp.where(row8 == t, pw[0], pw_r)
        pw_i = jnp.where(row8 == t, pw[1], pw_i)

    def body(gi, carry):
        r0 = pl.multiple_of(gi * 8, 8)
        a = br_ref[pl.ds(r0, 8), :]
        b = bi_ref[pl.ds(r0, 8), :]
        for d, pw in ((1, p1), (2, p2), (4, p4)):
            a_s = jnp.where(row8 >= d, pltpu.roll(a, d, 0), 0.0)
            b_s = jnp.where(row8 >= d, pltpu.roll(b, d, 0), 0.0)
            da, db = _cmul(pw[0], pw[1], a_s, b_s)
            a = a + da
            b = b + db
        if per_group_state:
            c_r = h0r_ref[pl.ds(gi, 1), :]
            c_i = h0i_ref[pl.ds(gi, 1), :]
        else:
            c_r, c_i = carry
        da, db = _cmul(pw_r, pw_i, c_r, c_i)
        a = a + da
        b = b + db
        br_ref[pl.ds(r0, 8), :] = a
        bi_ref[pl.ds(r0, 8), :] = b
        if per_group_state:
            hr_out_ref[pl.ds(gi, 1), :] = a[7:8]
            hi_out_ref[pl.ds(gi, 1), :] = b[7:8]
            return carry
        return a[7:8], b[7:8]

    if per_group_state:
        zero = jnp.zeros((1, S5_LANES), F32)
        lax.fori_loop(0, n_grp, body, (zero, zero))
    else:
        c_fin = lax.fori_loop(0, n_grp, body, (cr_carry_ref[...], ci_carry_ref[...]))
        cr_carry_ref[...] = c_fin[0]
        ci_carry_ref[...] = c_fin[1]
        hr_out_ref[...] = c_fin[0]
        hi_out_ref[...] = c_fin[1]

    y = (jnp.dot(br_ref[...].astype(BF16), cr_ref[...], preferred_element_type=F32)
         - jnp.dot(bi_ref[...].astype(BF16), ci_ref[...], preferred_element_type=F32)
         + d_ref[...] * u)
    zg = jax.nn.gelu(y)
    gate = jax.nn.sigmoid(jnp.dot(zg.astype(BF16), wglu_ref[...], preferred_element_type=F32) + bglu_ref[...])
    o_ref[...] = (zg * gate).astype(o_ref.dtype)


def _s5(su, w, h0r, h0i, per_group_state, out_dtype):
    rows = su.shape[0]
    tm = min(256, rows)
    n_grp = tm // 8
    row = lambda n: pl.BlockSpec((tm, n), lambda i: (i, 0))
    full = lambda a: pl.BlockSpec(a.shape, lambda i: (0,) * a.ndim)
    wlist = [w["s5_bb_re"], w["s5_bb_im"], w["s5_ab_re"], w["s5_ab_im"], w["s5_c_re"], w["s5_c_im"],
             w["s5_d"], w["s5_w_glu"], w["s5_b_glu"]]
    if per_group_state:
        st_spec = pl.BlockSpec((n_grp, S5_LANES), lambda i: (i, 0))
        st_shape = jax.ShapeDtypeStruct((rows // 8, S5_LANES), F32)
    else:
        st_spec = pl.BlockSpec((1, S5_LANES), lambda i: (0, 0))
        st_shape = jax.ShapeDtypeStruct((1, S5_LANES), F32)
    return pl.pallas_call(
        functools.partial(_s5_kernel, per_group_state),
        grid=(rows // tm,),
        in_specs=[row(S5_WIDTH)] + [full(a) for a in wlist] + [st_spec, st_spec],
        out_specs=[row(S5_WIDTH), st_spec, st_spec],
        out_shape=[jax.ShapeDtypeStruct((rows, S5_WIDTH), out_dtype), st_shape, st_shape],
        scratch_shapes=[pltpu.VMEM((tm, S5_LANES), F32), pltpu.VMEM((tm, S5_LANES), F32),
                        pltpu.VMEM((1, S5_LANES), F32), pltpu.VMEM((1, S5_LANES), F32)],
        compiler_params=_cparams("arbitrary"),
        name="s5_sample" if per_group_state else "s5_prompt",
    )(su, *wlist, h0r, h0i)


def _mixout_kernel(latent, x_ref, oa_ref, og_ref, os_ref, wuv_ref, woa_ref, wob_ref, woc_ref,
                   gpost_ref, gxa_ref, wq_ref, x2_ref, q_ref):
    if latent:
        mixed = None
        for h in range(MLA_HEADS):
            o_h = jnp.dot(oa_ref[h].astype(BF16), wuv_ref[h], preferred_element_type=F32)
            t = jnp.dot(o_h.astype(BF16), woa_ref[h * MLA_V:(h + 1) * MLA_V, :], preferred_element_type=F32)
            mixed = t if mixed is None else mixed + t
    else:
        mixed = jnp.dot(oa_ref[...].astype(BF16), woa_ref[...], preferred_element_type=F32)
    mixed = mixed + jnp.dot(og_ref[...].astype(BF16), wob_ref[...], preferred_element_type=F32)
    mixed = mixed + jnp.dot(os_ref[...].astype(BF16), woc_ref[...], preferred_element_type=F32)
    x2 = x_ref[...] + _rms(mixed, gpost_ref[...])
    x2_ref[...] = x2
    hq = _rms(x2, gxa_ref[...]).astype(BF16)
    q_ref[...] = jnp.dot(hq, wq_ref[...], preferred_element_type=F32) * XA_SCALE


def _mixout(latent, x, o_mla, o_gdn, o_s5, w):
    rows, d = x.shape
    tm = min(512, rows)
    row = lambda n: pl.BlockSpec((tm, n), lambda i: (i, 0))
    full = lambda a: pl.BlockSpec(a.shape, lambda i: (0,) * a.ndim)
    if latent:
        oa_spec = pl.BlockSpec((MLA_HEADS, tm, MLA_KV_RANK), lambda i: (0, i, 0))
    else:
        oa_spec = row(MLA_HEADS * MLA_V)
    wl = [w["wuv_h"], w["wo_a"], w["wo_b"], w["wo_c"], w["mix_g_post"], w["xa_g_pre"], w["xa_w_q"]]
    return pl.pallas_call(
        functools.partial(_mixout_kernel, latent),
        grid=(rows // tm,),
        in_specs=[row(d), oa_spec, row(o_gdn.shape[1]), row(o_s5.shape[1])] + [full(a) for a in wl],
        out_specs=[row(d), row(d)],
        out_shape=[jax.ShapeDtypeStruct((rows, d), F32), jax.ShapeDtypeStruct((rows, d), F32)],
        compiler_params=_cparams("parallel"),
        name="mixout_sample" if latent else "mixout_prompt",
    )(x, o_mla, o_gdn, o_s5, *wl)


def _xattn_kernel(n_b, q_ref, mk_ref, mv_ref, o_ref):
    rows = q_ref.shape[0] // n_b
    for bi in range(n_b):
        q = q_ref[bi * rows:(bi + 1) * rows, :]
        outs = []
        for h in range(XA_HEADS):
            hs = slice(h * XA_HEAD_DIM, (h + 1) * XA_HEAD_DIM)
            s = _bdot_nt(q[:, hs], mk_ref[bi, :, hs])
            m = jnp.max(s, axis=-1, keepdims=True)
            pm = jnp.exp(s - m)
            pr = pm / jnp.sum(pm, axis=-1, keepdims=True)
            outs.append(_bdot(pr, mv_ref[bi, :, hs]))
        o_ref[bi * rows:(bi + 1) * rows, :] = jnp.concatenate(outs, axis=-1).astype(o_ref.dtype)


def _xattn(q, mem_k, mem_v, layer, shared, t_new):
    rows, d = q.shape
    n_mem = mem_k.shape[-2]
    if shared:
        tm, n_b = min(512, rows), 1
        mspec = pl.BlockSpec((None, 1, n_mem, d), lambda i: (layer, 0, 0, 0))
    else:
        n_b = 4
        tm = n_b * t_new
        mspec = pl.BlockSpec((None, n_b, n_mem, d), lambda i: (layer, i, 0, 0))
    return pl.pallas_call(
        functools.partial(_xattn_kernel, n_b),
        grid=(rows // tm,),
        in_specs=[pl.BlockSpec((tm, d), lambda i: (i, 0)), mspec, mspec],
        out_specs=pl.BlockSpec((tm, d), lambda i: (i, 0)),
        out_shape=jax.ShapeDtypeStruct((rows, d), BF16),
        compiler_params=_cparams("parallel"),
        name="xattn_prompt" if shared else "xattn_sample",
    )(q, mem_k, mem_v)


def _xaout_kernel(x_ref, o_ref, wo_ref, g_ref, y_ref):
    xa = jnp.dot(o_ref[...], wo_ref[...], preferred_element_type=F32)
    y_ref[...] = x_ref[...] + _rms(xa, g_ref[...])


def _xaout(x, o, w):
    rows, d = x.shape
    tm = min(512, rows)
    row = pl.BlockSpec((tm, d), lambda i: (i, 0))
    return pl.pallas_call(
        _xaout_kernel,
        grid=(rows // tm,),
        in_specs=[row, row, pl.BlockSpec((d, d), lambda i: (0, 0)), pl.BlockSpec((1, d), lambda i: (0, 0))],
        out_specs=row,
        out_shape=jax.ShapeDtypeStruct((rows, d), F32),
        compiler_params=_cparams("parallel"),
        name="xattn_out",
    )(x, o, w["xa_w_o"], w["xa_g_post"])


def _memproj_kernel(m_ref, wk_ref, wv_ref, k_ref, v_ref):
    m = m_ref[...].astype(BF16)
    for l in range(wk_ref.shape[0]):
        k_ref[l] = jnp.dot(m, wk_ref[l], preferred_element_type=F32)
        v_ref[l] = jnp.dot(m, wv_ref[l], preferred_element_type=F32)


def _memproj(mem, wk, wv):
    depth = wk.shape[0]
    shp = jax.ShapeDtypeStruct((depth,) + mem.shape, F32)
    return pl.pallas_call(
        _memproj_kernel,
        out_shape=[shp, shp],
        compiler_params=pltpu.CompilerParams(vmem_limit_bytes=VMEM_LIMIT_BYTES),
        name="mem_kv_proj",
    )(mem, wk, wv)


def _layer_weights(l, p):
    d_model = p["w_in"].shape[1]
    w = {}
    r1 = lambda a: a[l].reshape(1, -1).astype(F32)
    for name in ("ffn1", "ffn2"):
        w[name + "_g_pre"] = r1(p[name + "_g_pre"])
        w[name + "_g_post"] = r1(p[name + "_g_post"])
        for s in ("_w_gate", "_w_up", "_w_down"):
            w[name + s] = p[name + s][l].astype(BF16)
    for name in ("mix_g_pre", "mix_g_post", "mla_g_q", "mla_g_kv", "xa_g_pre", "xa_g_post", "gdn_g_norm",
                 "s5_d", "s5_b_glu"):
        w[name] = r1(p[name])
    w["gdn_g_col"] = p["gdn_g_norm"][l].reshape(-1, 1).astype(F32)

    w_in = p["w_in"][l]
    offs = np.cumsum([0, MLA_Q_RANK, MLA_KV_RANK, MLA_ROPE, GDN_QKV, GDN_HEADS * GDN_DV, GDN_HEADS, GDN_HEADS,
                      S5_WIDTH])
    w_cq, w_ckv, w_kpe, w_qkv, w_z, w_b, w_a, w_su = [w_in[:, offs[i]:offs[i + 1]] for i in range(8)]
    half = MLA_ROPE // 2
    zeros = lambda n: jnp.zeros((d_model, n), w_in.dtype)
    w_kpe_sw = jnp.concatenate([-w_kpe[:, half:], w_kpe[:, :half]], axis=1)
    tail = HEAD_BLOCK - MLA_NOPE - MLA_ROPE
    w_kpl = jnp.concatenate([zeros(MLA_NOPE), w_kpe, zeros(tail)], axis=1)
    w_kpls = jnp.concatenate([zeros(MLA_NOPE), w_kpe_sw, zeros(tail)], axis=1)
    w_misc = jnp.concatenate([w_b, w_a, zeros(HEAD_BLOCK - 2 * GDN_HEADS)], axis=1)
    w["w_big"] = jnp.concatenate([w_cq, w_ckv, w_kpl, w_kpls, w_misc, w_qkv, w_z, w_su], axis=1).astype(BF16)

    w_uq = p["mla_w_uq"][l].reshape(MLA_Q_RANK, MLA_HEADS, MLA_NOPE + MLA_ROPE)
    nope, x1, x2 = w_uq[..., :MLA_NOPE], w_uq[..., MLA_NOPE:MLA_NOPE + half], w_uq[..., MLA_NOPE + half:]
    zq = lambda n: jnp.zeros((MLA_Q_RANK, MLA_HEADS, n), w_uq.dtype)
    w["wq_a"] = jnp.concatenate([nope, x1, x2, zq(tail)], axis=-1).reshape(MLA_Q_RANK, -1).astype(BF16)
    w["wq_b"] = jnp.concatenate([zq(MLA_NOPE), -x2, x1, zq(tail)], axis=-1).reshape(MLA_Q_RANK, -1).astype(BF16)

    w_uk = p["mla_w_uk"][l]
    w_uv = p["mla_w_uv"][l]
    zk = jnp.zeros((MLA_KV_RANK, MLA_HEADS, HEAD_BLOCK - MLA_NOPE), w_uk.dtype)
    w["wuk_pad"] = jnp.concatenate([w_uk, zk], axis=-1).reshape(MLA_KV_RANK, -1).astype(BF16)
    wuk_t = jnp.transpose(w_uk, (1, 2, 0))
    w["wuk_t"] = jnp.concatenate(
        [wuk_t, jnp.zeros((MLA_HEADS, HEAD_BLOCK - MLA_NOPE, MLA_KV_RANK), w_uk.dtype)], axis=1).astype(BF16)
    w["wuv"] = w_uv.reshape(MLA_KV_RANK, -1).astype(BF16)
    w["wuv_h"] = jnp.transpose(w_uv, (1, 0, 2)).astype(BF16)

    w_out = p["w_out"][l]
    n_a = MLA_HEADS * MLA_V
    n_b = n_a + GDN_HEADS * GDN_DV
    w["wo_a"] = w_out[:n_a].astype(BF16)
    w["wo_b"] = w_out[n_a:n_b].astype(BF16)
    w["wo_c"] = w_out[n_b:].astype(BF16)
    w["xa_w_q"] = p["xa_w_q"][l].astype(BF16)
    w["xa_w_o"] = p["xa_w_o"][l].astype(BF16)

    w["gdn_conv_w"] = p["gdn_conv_w"][l].astype(F32)
    pad_gate = lambda v: jnp.zeros((1, HEAD_BLOCK), F32).at[0, GDN_HEADS:2 * GDN_HEADS].set(v.astype(F32))
    w["gdn_alog_pad"] = pad_gate(p["gdn_a_log"][l])
    w["gdn_dtb_pad"] = pad_gate(p["gdn_dt_bias"][l])

    a_re, a_im = p["s5_a_re"][l].astype(F32), p["s5_a_im"][l].astype(F32)
    dt = jnp.exp(p["s5_log_dt"][l].astype(F32))[:, None]
    mag = jnp.exp(a_re * dt)
    ab_re, ab_im = mag * jnp.cos(a_im * dt), mag * jnp.sin(a_im * dt)
    den = a_re * a_re + a_im * a_im
    nr, ni = ab_re - 1.0, ab_im
    coef_re = (nr * a_re + ni * a_im) / den
    coef_im = (ni * a_re - nr * a_im) / den
    b_re, b_im = p["s5_b_re"][l].astype(F32), p["s5_b_im"][l].astype(F32)
    bb_re = coef_re[..., None] * b_re - coef_im[..., None] * b_im
    bb_im = coef_re[..., None] * b_im + coef_im[..., None] * b_re
    eye = jnp.eye(S5_GROUPS, dtype=F32)
    bd_in = lambda bb: jnp.einsum("gnp,gh->gphn", bb, eye).reshape(S5_WIDTH, S5_LANES).astype(BF16)
    bd_out = lambda cc: jnp.einsum("gpn,gh->gnhp", cc.astype(F32), eye).reshape(S5_LANES, S5_WIDTH).astype(BF16)
    w["s5_bb_re"], w["s5_bb_im"] = bd_in(bb_re), bd_in(bb_im)
    w["s5_c_re"], w["s5_c_im"] = bd_out(p["s5_c_re"][l]), bd_out(p["s5_c_im"][l])
    w["s5_ab_re"] = ab_re.reshape(1, S5_LANES)
    w["s5_ab_im"] = ab_im.reshape(1, S5_LANES)
    w["s5_w_glu"] = p["s5_w_glu"][l].astype(BF16)
    return w


def _rope_tables(pos):
    half = MLA_ROPE // 2
    inv = ROPE_THETA ** (-jnp.arange(half, dtype=F32) / half)
    ang = pos.astype(F32)[:, None] * inv[None, :]
    c, s = jnp.cos(ang), jnp.sin(ang)
    n = pos.shape[0]
    z0 = jnp.zeros((n, MLA_NOPE), F32)
    z1 = jnp.zeros((n, HEAD_BLOCK - MLA_NOPE - MLA_ROPE), F32)
    return jnp.concatenate([z0, c, c, z1], axis=1), jnp.concatenate([z0, s, s, z1], axis=1)


def kernel(x_prompt, x_sample, mem_prompt, cache_ckv, cache_kpe, page_table, cache_mem_k, cache_mem_v, state_gdn, state_gdn_conv, state_s5_re, state_s5_im, ffn1_g_pre, ffn1_g_post, ffn1_w_gate, ffn1_w_up, ffn1_w_down, mix_g_pre, mix_g_post, w_in, w_out, mla_g_q, mla_w_uq, mla_g_kv, mla_w_uk, mla_w_uv, gdn_conv_w, gdn_a_log, gdn_dt_bias, gdn_g_norm, s5_a_re, s5_a_im, s5_log_dt, s5_b_re, s5_b_im, s5_c_re, s5_c_im, s5_d, s5_w_glu, s5_b_glu, xa_g_pre, xa_g_post, xa_w_q, xa_w_k, xa_w_v, xa_w_o, ffn2_g_pre, ffn2_g_post, ffn2_w_gate, ffn2_w_up, ffn2_w_down):
    params = dict(
        ffn1_g_pre=ffn1_g_pre, ffn1_g_post=ffn1_g_post, ffn1_w_gate=ffn1_w_gate, ffn1_w_up=ffn1_w_up,
        ffn1_w_down=ffn1_w_down, mix_g_pre=mix_g_pre, mix_g_post=mix_g_post, w_in=w_in, w_out=w_out,
        mla_g_q=mla_g_q, mla_w_uq=mla_w_uq, mla_g_kv=mla_g_kv, mla_w_uk=mla_w_uk, mla_w_uv=mla_w_uv,
        gdn_conv_w=gdn_conv_w, gdn_a_log=gdn_a_log, gdn_dt_bias=gdn_dt_bias, gdn_g_norm=gdn_g_norm,
        s5_a_re=s5_a_re, s5_a_im=s5_a_im, s5_log_dt=s5_log_dt, s5_b_re=s5_b_re, s5_b_im=s5_b_im,
        s5_c_re=s5_c_re, s5_c_im=s5_c_im, s5_d=s5_d, s5_w_glu=s5_w_glu, s5_b_glu=s5_b_glu,
        xa_g_pre=xa_g_pre, xa_g_post=xa_g_post, xa_w_q=xa_w_q, xa_w_o=xa_w_o,
        ffn2_g_pre=ffn2_g_pre, ffn2_g_post=ffn2_g_post, ffn2_w_gate=ffn2_w_gate, ffn2_w_up=ffn2_w_up,
        ffn2_w_down=ffn2_w_down)
    depth = w_in.shape[0]
    bsz, seqlen, d_model = x_prompt.shape
    dec_b, dec_t, _ = x_sample.shape
    n_mem = mem_prompt.shape[1]
    past_len = page_table.shape[1] * PAGE_SIZE
    n_s = dec_b * dec_t

    tc_p, ts_p = _rope_tables(jnp.arange(seqlen))
    tc_s, ts_s = _rope_tables(jnp.tile(past_len + jnp.arange(dec_t), dec_b))

    mem_k, mem_v = _memproj(mem_prompt.reshape(n_mem, d_model), xa_w_k.astype(BF16), xa_w_v.astype(BF16))
    mem_k4 = mem_k.reshape(depth, bsz, n_mem, d_model)
    mem_v4 = mem_v.reshape(depth, bsz, n_mem, d_model)
    cmk = cache_mem_k.reshape(depth, dec_b, n_mem, d_model)
    cmv = cache_mem_v.reshape(depth, dec_b, n_mem, d_model)

    yp = x_prompt.reshape(seqlen, d_model)
    ys = x_sample.reshape(n_s, d_model)
    zero_state = jnp.zeros((1, S5_LANES), F32)
    outs = {k: [] for k in ("p_ckv", "p_kpe", "p_gdn", "p_conv", "p_s5r", "p_s5i",
                            "s_ckv", "s_kpe", "s_gdn", "s_conv", "s_s5r", "s_s5i")}
    for l in range(depth):
        w = _layer_weights(l, params)
        yp = _half_ffn(yp, w["ffn1_g_pre"], w["ffn1_g_post"], w["ffn1_w_gate"], w["ffn1_w_up"], w["ffn1_w_down"])
        ckv, kpe, qkv, z, misc, su, q, k, v = _mixprep(False, yp, w, tc_p, ts_p)
        o_mla = _prompt_attention(q, k, v)
        o_gdn, gdn_s = _gdn_prompt(qkv, z, misc, w)
        o_s5, h_re, h_im = _s5(su, w, zero_state, zero_state, False, BF16)
        x2, xq = _mixout(False, yp, o_mla, o_gdn, o_s5, w)
        o_xa = _xattn(xq, mem_k4, mem_v4, l, True, dec_t)
        yp = _xaout(x2, o_xa, w)
        yp = _half_ffn(yp, w["ffn2_g_pre"], w["ffn2_g_post"], w["ffn2_w_gate"], w["ffn2_w_up"], w["ffn2_w_down"])
        outs["p_ckv"].append(ckv.reshape(bsz, seqlen, MLA_KV_RANK))
        outs["p_kpe"].append(kpe.reshape(bsz, seqlen, MLA_ROPE))
        outs["p_gdn"].append(gdn_s.reshape(bsz, GDN_HEADS, GDN_DK, GDN_DV))
        outs["p_conv"].append(qkv[seqlen - (GDN_CONV - 1):].reshape(bsz, GDN_CONV - 1, GDN_QKV))
        outs["p_s5r"].append(h_re.reshape(bsz, S5_GROUPS, S5_STATE))
        outs["p_s5i"].append(h_im.reshape(bsz, S5_GROUPS, S5_STATE))

        ys = _half_ffn(ys, w["ffn1_g_pre"], w["ffn1_g_post"], w["ffn1_w_gate"], w["ffn1_w_up"], w["ffn1_w_down"])
        ckv, kpe, qkv, z, misc, su, qlat, qpe = _mixprep(True, ys, w, tc_s, ts_s)
        o_lat = _paged_attention(l, qlat, qpe, ckv, kpe, cache_ckv, cache_kpe, page_table, dec_t)
        st_rows = jnp.pad(state_gdn_conv[l], ((0, 0), (0, dec_t - (GDN_CONV - 1)), (0, 0))).reshape(n_s, GDN_QKV)
        o_gdn, gdn_s = _gdn_sample(qkv, st_rows, z, misc, w, state_gdn[l].reshape(dec_b, -1), dec_t)
        o_s5, h_re, h_im = _s5(su, w, state_s5_re[l].reshape(dec_b, S5_LANES),
                               state_s5_im[l].reshape(dec_b, S5_LANES), True, F32)
        x2, xq = _mixout(True, ys, o_lat, o_gdn, o_s5, w)
        o_xa = _xattn(xq, cmk, cmv, l, False, dec_t)
        ys = _xaout(x2, o_xa, w)
        ys = _half_ffn(ys, w["ffn2_g_pre"], w["ffn2_g_post"], w["ffn2_w_gate"], w["ffn2_w_up"], w["ffn2_w_down"])
        outs["s_ckv"].append(ckv.reshape(dec_b, dec_t, MLA_KV_RANK))
        outs["s_kpe"].append(kpe.reshape(dec_b, dec_t, MLA_ROPE))
        outs["s_gdn"].append(gdn_s.reshape(dec_b, GDN_HEADS, GDN_DK, GDN_DV))
        outs["s_conv"].append(qkv.reshape(dec_b, dec_t, GDN_QKV)[:, dec_t - (GDN_CONV - 1):])
        outs["s_s5r"].append(h_re.reshape(dec_b, S5_GROUPS, S5_STATE))
        outs["s_s5i"].append(h_im.reshape(dec_b, S5_GROUPS, S5_STATE))

    st = {k: jnp.stack(v) for k, v in outs.items()}
    p_mem_k = mem_k.reshape(depth, bsz, n_mem, XA_HEADS, XA_HEAD_DIM)
    p_mem_v = mem_v.reshape(depth, bsz, n_mem, XA_HEADS, XA_HEAD_DIM)
    return (yp.reshape(bsz, seqlen, d_model), ys.reshape(dec_b, dec_t, d_model),
            st["p_ckv"], st["p_kpe"], st["p_gdn"], st["p_conv"], st["p_s5r"], st["p_s5i"], p_mem_k, p_mem_v,
            st["s_ckv"], st["s_kpe"], st["s_gdn"], st["s_conv"], st["s_s5r"], st["s_s5i"])
```

```python
import functools
import math

import jax
import jax.numpy as jnp
import numpy as np
from jax import lax
from jax.experimental import pallas as pl
from jax.experimental.pallas import tpu as pltpu

F32 = jnp.float32
BF16 = jnp.bfloat16

RMS_EPS = 1e-6
MLA_HEADS = 8
MLA_NOPE = 64
MLA_ROPE = 32
MLA_V = 64
MLA_Q_RANK = 384
MLA_KV_RANK = 256
ROPE_THETA = 10000.0
HEAD_BLOCK = 128
PAGE_SIZE = 128
PAGES_PER_STEP = 32
PAGE_GROUP = 8
ATTN_TILE = 1024
ATTN_SUB = 512
GDN_HEADS = 4
GDN_DK = 64
GDN_DV = 64
GDN_CONV = 4
GDN_CHUNK = 64
GDN_QKV = GDN_HEADS * (2 * GDN_DK + GDN_DV)
S5_GROUPS = 16
S5_GROUP = 16
S5_STATE = 64
S5_WIDTH = S5_GROUPS * S5_GROUP
S5_LANES = S5_GROUPS * S5_STATE
XA_HEADS = 4
XA_HEAD_DIM = 256
XA_SEQ_PER_STEP = 4

VMEM_LIMIT_BYTES = 56 * 1024 * 1024

MLA_SCALE = (MLA_NOPE + MLA_ROPE) ** -0.5
LOG2_E = math.log2(math.e)
XA_SCALE = XA_HEAD_DIM ** -0.5


def _cparams(*sem):
    return pltpu.CompilerParams(dimension_semantics=tuple(sem), vmem_limit_bytes=VMEM_LIMIT_BYTES)


def _rms(x, g):
    return x * lax.rsqrt(jnp.mean(x * x, axis=-1, keepdims=True) + RMS_EPS) * g


def _bdot(a, b):
    return jnp.dot(a.astype(BF16), b.astype(BF16), preferred_element_type=F32)


def _bdot_nt(a, b):
    return lax.dot_general(a.astype(BF16), b.astype(BF16), (((1,), (1,)), ((), ())),
                           preferred_element_type=F32)


def _split3(a):
    hi = a.astype(BF16)
    r1 = a - hi.astype(F32)
    mid = r1.astype(BF16)
    lo = (r1 - mid.astype(F32)).astype(BF16)
    return hi, mid, lo


def _dot_split(a, b):
    a0 = a.astype(BF16)
    a1 = (a - a0.astype(F32)).astype(BF16)
    b0 = b.astype(BF16)
    b1 = (b - b0.astype(F32)).astype(BF16)
    d = functools.partial(jnp.dot, preferred_element_type=F32)
    return d(a0, b0) + (d(a0, b1) + d(a1, b0))


def _silu(x):
    return x * jax.nn.sigmoid(x)


def _softplus(x):
    return jnp.maximum(x, 0.0) + jnp.log(1.0 + jnp.exp(-jnp.abs(x)))


def _ffn_kernel(x_ref, gpre_ref, gpost_ref, wg_ref, wu_ref, wd_ref, o_ref, xn_ref, acc_ref):
    j = pl.program_id(1)

    @pl.when(j == 0)
    def _():
        xn_ref[...] = _rms(x_ref[...], gpre_ref[...]).astype(BF16)
        acc_ref[...] = jnp.zeros_like(acc_ref)

    xn = xn_ref[...]
    g = jnp.dot(xn, wg_ref[...], preferred_element_type=F32)
    u = jnp.dot(xn, wu_ref[...], preferred_element_type=F32)
    h = (_silu(g) * u).astype(BF16)
    acc_ref[...] += jnp.dot(h, wd_ref[...], preferred_element_type=F32)

    @pl.when(j == pl.num_programs(1) - 1)
    def _():
        o_ref[...] = x_ref[...] + 0.5 * _rms(acc_ref[...], gpost_ref[...])


def _half_ffn(x, g_pre, g_post, wg, wu, wd):
    rows, d = x.shape
    f = wg.shape[1]
    tm = min(512, rows)
    tf = f // 2
    return pl.pallas_call(
        _ffn_kernel,
        grid=(rows // tm, f // tf),
        in_specs=[
            pl.BlockSpec((tm, d), lambda i, j: (i, 0)),
            pl.BlockSpec((1, d), lambda i, j: (0, 0)),
            pl.BlockSpec((1, d), lambda i, j: (0, 0)),
            pl.BlockSpec((d, tf), lambda i, j: (0, j)),
            pl.BlockSpec((d, tf), lambda i, j: (0, j)),
            pl.BlockSpec((tf, d), lambda i, j: (j, 0)),
        ],
        out_specs=pl.BlockSpec((tm, d), lambda i, j: (i, 0)),
        out_shape=jax.ShapeDtypeStruct((rows, d), F32),
        scratch_shapes=[pltpu.VMEM((tm, d), BF16), pltpu.VMEM((tm, d), F32)],
        compiler_params=_cparams("parallel", "arbitrary"),
        name="half_ffn",
    )(x, g_pre, g_post, wg, wu, wd)


C_CQ = (0, 384)
C_CKV = (384, 640)
C_KPL = (640, 768)
C_KPLS = (768, 896)
C_MISC = (896, 1024)
C_QKV = (1024, 1792)
C_Z = (1792, 2048)
C_SU = (2048, 2304)
W_BIG = 2304


def _mixprep_body(sample, x_ref, g_ref, wbig_ref, gq_ref, wqa_ref, wqb_ref, gkv_ref, tc_ref, ts_ref,
                  wk_ref, wv_ref, ckv_ref, kpe_ref, qkv_ref, z_ref, misc_ref, su_ref, a_ref, b_ref, c_ref):
    u = _rms(x_ref[...], g_ref[...]).astype(BF16)
    y = jnp.dot(u, wbig_ref[...], preferred_element_type=F32)
    sl = lambda c: y[:, c[0]:c[1]]
    tc = tc_ref[...]
    ts = ts_ref[...]
    kpe_pl = sl(C_KPL) * tc + sl(C_KPLS) * ts
    kpe_ref[...] = kpe_pl[:, MLA_NOPE:MLA_NOPE + MLA_ROPE]
    ckv_n = _rms(sl(C_CKV), gkv_ref[...])
    ckv_ref[...] = ckv_n
    qkv_ref[...] = sl(C_QKV)
    z_ref[...] = sl(C_Z)
    misc_ref[...] = sl(C_MISC)
    su_ref[...] = sl(C_SU)

    cqn = _rms(sl(C_CQ), gq_ref[...]).astype(BF16)
    qa = jnp.dot(cqn, wqa_ref[...], preferred_element_type=F32)
    qb = jnp.dot(cqn, wqb_ref[...], preferred_element_type=F32)
    lane = lax.broadcasted_iota(jnp.int32, (1, HEAD_BLOCK), 1)
    qmul = jnp.where(lane < MLA_NOPE, 1.0, 0.0) + tc
    ckb = ckv_n.astype(BF16)
    if sample:
        qlat_ref, qpe_ref = a_ref, b_ref
        for h in range(MLA_HEADS):
            hs = slice(h * HEAD_BLOCK, (h + 1) * HEAD_BLOCK)
            qh = qa[:, hs] * qmul + qb[:, hs] * ts
            qlat_ref[h] = jnp.dot(qa[:, hs].astype(BF16), wk_ref[h], preferred_element_type=F32) * MLA_SCALE
            qpe_ref[h] = qh[:, MLA_NOPE:MLA_NOPE + MLA_ROPE] * MLA_SCALE
    else:
        q_ref, k_ref, v_ref = a_ref, b_ref, c_ref
        kn = jnp.dot(ckb, wk_ref[...], preferred_element_type=F32)
        for h in range(MLA_HEADS):
            hs = slice(h * HEAD_BLOCK, (h + 1) * HEAD_BLOCK)
            qh = qa[:, hs] * qmul + qb[:, hs] * ts
            q_ref[h] = (qh * (MLA_SCALE * LOG2_E)).astype(BF16)
            k_ref[h] = (kn[:, hs] + kpe_pl).astype(BF16)
        v_ref[...] = _bdot_nt(wv_ref[...], ckb).astype(BF16)


def _mixprep_prompt_kernel(x_ref, g_ref, wbig_ref, gq_ref, wqa_ref, wqb_ref, gkv_ref, tc_ref, ts_ref,
                           wk_ref, wv_ref, ckv_ref, kpe_ref, qkv_ref, z_ref, misc_ref, su_ref,
                           q_ref, k_ref, v_ref):
    _mixprep_body(False, x_ref, g_ref, wbig_ref, gq_ref, wqa_ref, wqb_ref, gkv_ref, tc_ref, ts_ref,
                  wk_ref, wv_ref, ckv_ref, kpe_ref, qkv_ref, z_ref, misc_ref, su_ref, q_ref, k_ref, v_ref)


def _mixprep_sample_kernel(x_ref, g_ref, wbig_ref, gq_ref, wqa_ref, wqb_ref, gkv_ref, tc_ref, ts_ref,
                           wk_ref, ckv_ref, kpe_ref, qkv_ref, z_ref, misc_ref, su_ref, qlat_ref, qpe_ref):
    _mixprep_body(True, x_ref, g_ref, wbig_ref, gq_ref, wqa_ref, wqb_ref, gkv_ref, tc_ref, ts_ref,
                  wk_ref, None, ckv_ref, kpe_ref, qkv_ref, z_ref, misc_ref, su_ref, qlat_ref, qpe_ref, None)


def _mixprep(sample, x, w, tc, ts):
    rows, d = x.shape
    tm = min(512, rows)
    row = lambda n: pl.BlockSpec((tm, n), lambda i: (i, 0))
    full = lambda a: pl.BlockSpec(a.shape, lambda i: (0,) * a.ndim)
    hrow = lambda n: pl.BlockSpec((MLA_HEADS, tm, n), lambda i: (0, i, 0))
    common_in = [x, w["mix_g_pre"], w["w_big"], w["mla_g_q"], w["wq_a"], w["wq_b"], w["mla_g_kv"], tc, ts]
    common_specs = [row(d)] + [full(a) for a in common_in[1:7]] + [row(HEAD_BLOCK), row(HEAD_BLOCK)]
    common_out = [
        (jax.ShapeDtypeStruct((rows, MLA_KV_RANK), F32), row(MLA_KV_RANK)),
        (jax.ShapeDtypeStruct((rows, MLA_ROPE), F32), row(MLA_ROPE)),
        (jax.ShapeDtypeStruct((rows, GDN_QKV), F32), row(GDN_QKV)),
        (jax.ShapeDtypeStruct((rows, GDN_HEADS * GDN_DV), F32), row(GDN_HEADS * GDN_DV)),
        (jax.ShapeDtypeStruct((rows, HEAD_BLOCK), F32), row(HEAD_BLOCK)),
        (jax.ShapeDtypeStruct((rows, S5_WIDTH), F32), row(S5_WIDTH)),
    ]
    if sample:
        ins = common_in + [w["wuk_t"]]
        specs = common_specs + [full(w["wuk_t"])]
        outs = common_out + [
            (jax.ShapeDtypeStruct((MLA_HEADS, rows, MLA_KV_RANK), F32), hrow(MLA_KV_RANK)),
            (jax.ShapeDtypeStruct((MLA_HEADS, rows, MLA_ROPE), F32), hrow(MLA_ROPE)),
        ]
        body = _mixprep_sample_kernel
    else:
        ins = common_in + [w["wuk_pad"], w["wuv"]]
        specs = common_specs + [full(w["wuk_pad"]), full(w["wuv"])]
        outs = common_out + [
            (jax.ShapeDtypeStruct((MLA_HEADS, rows, HEAD_BLOCK), BF16), hrow(HEAD_BLOCK)),
            (jax.ShapeDtypeStruct((MLA_HEADS, rows, HEAD_BLOCK), BF16), hrow(HEAD_BLOCK)),
            (jax.ShapeDtypeStruct((MLA_HEADS * MLA_V, rows), BF16),
             pl.BlockSpec((MLA_HEADS * MLA_V, tm), lambda i: (0, i))),
        ]
        body = _mixprep_prompt_kernel
    return pl.pallas_call(
        body,
        grid=(rows // tm,),
        in_specs=specs,
        out_specs=[o[1] for o in outs],
        out_shape=[o[0] for o in outs],
        compiler_params=_cparams("parallel"),
        name="mixprep_sample" if sample else "mixprep_prompt",
    )(*ins)


def _attn_kernel(qi_ref, ki_ref, q_ref, k_ref, vt_ref, o_ref, m_ref, l_ref, acc_ref):
    p = pl.program_id(1)
    i = qi_ref[p]
    j = ki_ref[p]
    tq, tk = q_ref.shape[1], k_ref.shape[1]

    @pl.when(j == 0)
    def _():
        m_ref[...] = jnp.full(m_ref.shape, -jnp.inf, F32)
        l_ref[...] = jnp.zeros_like(l_ref)
        acc_ref[...] = jnp.zeros_like(acc_ref)

    sub = min(ATTN_SUB, tk)

    def step(masked):
        hrows = [slice(hh * MLA_V, (hh + 1) * MLA_V) for hh in range(2)]
        m = [m_ref[hh] for hh in range(2)]
        l = [l_ref[hh] for hh in range(2)]
        acc = [acc_ref[hrows[hh], :] for hh in range(2)]
        units = [(c, hh) for c in range(tk // sub) for hh in range(2)]

        def scores(c, hh):
            return lax.dot_general(k_ref[hh, c * sub:(c + 1) * sub, :], q_ref[hh], (((1,), (1,)), ((), ())),
                                   preferred_element_type=F32)

        st_next = scores(*units[0])
        for n, (c, hh) in enumerate(units):
            st = st_next
            if n + 1 < len(units):
                st_next = scores(*units[n + 1])
            if masked:
                key = lax.broadcasted_iota(jnp.int32, (sub, tq), 0) + c * sub
                qry = lax.broadcasted_iota(jnp.int32, (sub, tq), 1)
                st = jnp.where(key <= qry, st, -jnp.inf)
            m_new = jnp.maximum(m[hh], jnp.max(st, axis=0, keepdims=True))
            alpha = jnp.exp2(m[hh] - m_new)
            pt = jnp.exp2(st - m_new)
            l[hh] = alpha * l[hh] + jnp.sum(pt, axis=0, keepdims=True)
            acc[hh] = alpha * acc[hh] + jnp.dot(vt_ref[hrows[hh], c * sub:(c + 1) * sub], pt.astype(BF16),
                                                preferred_element_type=F32)
            m[hh] = m_new
        for hh in range(2):
            m_ref[hh] = m[hh]
            l_ref[hh] = l[hh]
            acc_ref[hrows[hh], :] = acc[hh]

    @pl.when(j < i)
    def _():
        step(False)

    @pl.when(j == i)
    def _():
        step(True)
        inv = jnp.concatenate([jnp.broadcast_to(1.0 / l_ref[hh], (MLA_V, tq)) for hh in range(2)], axis=0)
        o_ref[...] = (acc_ref[...] * inv).T.astype(o_ref.dtype)


def _prompt_attention(q, k, vt):
    rows = q.shape[1]
    t = min(ATTN_TILE, rows)
    n = rows // t
    qi = np.array([i for i in range(n) for _ in range(i + 1)], np.int32)
    ki = np.array([j for i in range(n) for j in range(i + 1)], np.int32)
    grid_spec = pltpu.PrefetchScalarGridSpec(
        num_scalar_prefetch=2,
        grid=(MLA_HEADS // 2, len(qi)),
        in_specs=[
            pl.BlockSpec((2, t, HEAD_BLOCK), lambda hp, p, qi, ki: (hp, qi[p], 0)),
            pl.BlockSpec((2, t, HEAD_BLOCK), lambda hp, p, qi, ki: (hp, ki[p], 0)),
            pl.BlockSpec((2 * MLA_V, t), lambda hp, p, qi, ki: (hp, ki[p])),
        ],
        out_specs=pl.BlockSpec((t, 2 * MLA_V), lambda hp, p, qi, ki: (qi[p], hp)),
        scratch_shapes=[pltpu.VMEM((2, 1, t), F32), pltpu.VMEM((2, 1, t), F32),
                        pltpu.VMEM((2 * MLA_V, t), F32)],
    )
    return pl.pallas_call(
        _attn_kernel,
        grid_spec=grid_spec,
        out_shape=jax.ShapeDtypeStruct((rows, MLA_HEADS * MLA_V), BF16),
        compiler_params=_cparams("parallel", "arbitrary"),
        name="mla_prompt_attention",
    )(jnp.asarray(qi), jnp.asarray(ki), q, k, vt)


def _paged_kernel(layer, n_pp, n_steps, pt_ref, qlat_ref, qpe_ref, cnew_ref, knew_ref, ckv_hbm, kpe_hbm,
                  o_ref, m_ref, l_ref, acc_ref, ckv_buf, kpe_buf, ckv_all_ref, kpe_all_ref, sem):
    j = pl.program_id(1)
    step = pl.program_id(0) * n_steps + j
    n_total = pl.num_programs(0) * n_steps
    slot = step % 2
    t_new = cnew_ref.shape[0]
    rows = MLA_HEADS * t_new

    def page_copies(s, dst_slot, k):
        page = pt_ref[s * n_pp + k]
        return (pltpu.make_async_copy(ckv_hbm.at[layer, page], ckv_buf.at[dst_slot, k], sem.at[0, dst_slot]),
                pltpu.make_async_copy(kpe_hbm.at[layer, page], kpe_buf.at[dst_slot, k], sem.at[1, dst_slot]))

    def start_step(s, dst_slot):
        for k in range(n_pp):
            for cp in page_copies(s, dst_slot, k):
                cp.start()

    @pl.when(step == 0)
    def _():
        start_step(0, 0)

    @pl.when(step + 1 < n_total)
    def _():
        start_step(step + 1, 1 - slot)

    ql = qlat_ref[...].reshape(rows, MLA_KV_RANK).astype(BF16)
    qp = qpe_ref[...].reshape(rows, MLA_ROPE).astype(BF16)

    @pl.when(j == 0)
    def _():
        cn = cnew_ref[...].astype(BF16)
        s = _bdot_nt(ql, cn) + _bdot_nt(qp, knew_ref[...])
        tok = lax.broadcasted_iota(jnp.int32, (rows, t_new), 0) % t_new
        key = lax.broadcasted_iota(jnp.int32, (rows, t_new), 1)
        s = jnp.where(key <= tok, s, -jnp.inf)
        m = jnp.max(s, axis=-1, keepdims=True)
        pm = jnp.exp(s - m)
        m_ref[...] = jnp.broadcast_to(m, m_ref.shape)
        l_ref[...] = jnp.broadcast_to(jnp.sum(pm, axis=-1, keepdims=True), l_ref.shape)
        acc_ref[...] = jnp.dot(pm.astype(BF16), cn, preferred_element_type=F32)

    for k in range(n_pp):
        for cp in page_copies(step, slot, k):
            cp.wait()
    n_grp = max(1, n_pp // PAGE_GROUP)
    per = n_pp // n_grp
    gkeys = per * PAGE_SIZE

    def scores(g):
        for k in range(g * per, (g + 1) * per):
            ckv_all_ref[k * PAGE_SIZE:(k + 1) * PAGE_SIZE, :] = ckv_buf[slot, k].astype(BF16)
            kpe_all_ref[:, k * PAGE_SIZE:(k + 1) * PAGE_SIZE] = kpe_buf[slot, k].astype(BF16)
        cg = ckv_all_ref[g * gkeys:(g + 1) * gkeys, :]
        kg = kpe_all_ref[:, g * gkeys:(g + 1) * gkeys]
        return _bdot_nt(ql, cg) + jnp.dot(qp, kg, preferred_element_type=F32), cg

    m = m_ref[...]
    l = l_ref[...]
    acc = acc_ref[...]
    nxt = scores(0)
    for g in range(n_grp):
        s, cg = nxt
        if g + 1 < n_grp:
            nxt = scores(g + 1)
        m_new = jnp.maximum(m, jnp.max(s, axis=-1, keepdims=True))
        alpha = jnp.exp(m - m_new)
        pm = jnp.exp(s - m_new[:, :1])
        l = alpha * l + jnp.sum(pm, axis=-1, keepdims=True)
        acc = alpha[:, :1] * acc + jnp.dot(pm.astype(BF16), cg, preferred_element_type=F32)
        m = m_new
    m_ref[...] = m
    l_ref[...] = l
    acc_ref[...] = acc

    @pl.when(j == n_steps - 1)
    def _():
        o = acc_ref[...] / l_ref[...][:, :1]
        o_ref[...] = o.reshape(MLA_HEADS, t_new, MLA_KV_RANK)


def _paged_attention(layer, qlat, qpe, ckv_new, kpe_new, cache_ckv, cache_kpe_t, page_table, t_new):
    n_b, n_pages = page_table.shape
    n_pp = min(PAGES_PER_STEP, n_pages)
    n_steps = n_pages // n_pp
    rows = MLA_HEADS * t_new
    grid_spec = pltpu.PrefetchScalarGridSpec(
        num_scalar_prefetch=1,
        grid=(n_b, n_steps),
        in_specs=[
            pl.BlockSpec((MLA_HEADS, t_new, MLA_KV_RANK), lambda b, j, pt: (0, b, 0)),
            pl.BlockSpec((MLA_HEADS, t_new, MLA_ROPE), lambda b, j, pt: (0, b, 0)),
            pl.BlockSpec((t_new, MLA_KV_RANK), lambda b, j, pt: (b, 0)),
            pl.BlockSpec((t_new, MLA_ROPE), lambda b, j, pt: (b, 0)),
            pl.BlockSpec(memory_space=pl.ANY),
            pl.BlockSpec(memory_space=pl.ANY),
        ],
        out_specs=pl.BlockSpec((MLA_HEADS, t_new, MLA_KV_RANK), lambda b, j, pt: (0, b, 0)),
        scratch_shapes=[pltpu.VMEM((rows, HEAD_BLOCK), F32), pltpu.VMEM((rows, HEAD_BLOCK), F32),
                        pltpu.VMEM((rows, MLA_KV_RANK), F32),
                        pltpu.VMEM((2, n_pp, PAGE_SIZE, MLA_KV_RANK), F32),
                        pltpu.VMEM((2, n_pp, MLA_ROPE, PAGE_SIZE), F32),
                        pltpu.VMEM((n_pp * PAGE_SIZE, MLA_KV_RANK), BF16),
                        pltpu.VMEM((MLA_ROPE, n_pp * PAGE_SIZE), BF16),
                        pltpu.SemaphoreType.DMA((2, 2))],
    )
    return pl.pallas_call(
        functools.partial(_paged_kernel, layer, n_pp, n_steps),
        grid_spec=grid_spec,
        out_shape=jax.ShapeDtypeStruct((MLA_HEADS, n_b * t_new, MLA_KV_RANK), F32),
        compiler_params=_cparams("arbitrary", "arbitrary"),
        name="mla_paged_attention",
    )(page_table.reshape(-1), qlat, qpe, ckv_new, kpe_new, cache_ckv, cache_kpe_t)


def _gdn_gates(misc, alog, dtb):
    beta = jax.nn.sigmoid(misc)
    g = -jnp.exp(alog) * _softplus(misc + dtb)
    return beta, g


def _l2n(x, scale):
    return x * (lax.rsqrt(jnp.sum(x * x, axis=-1, keepdims=True) + 1e-6) * scale)


def _gdn_prompt_kernel(qkv_ref, z_ref, misc_ref, convw_ref, alog_ref, dtb_ref, gnorm_ref,
                       o_ref, sfin_ref, s_ref, carry_ref):
    i = pl.program_id(0)

    @pl.when(i == 0)
    def _():
        s_ref[...] = jnp.zeros_like(s_ref)
        carry_ref[...] = jnp.zeros_like(carry_ref)

    x = qkv_ref[...]
    tm = x.shape[0]
    w = convw_ref[...]
    row8 = lax.broadcasted_iota(jnp.int32, (8, GDN_QKV), 0)
    cprev = carry_ref[...]
    acc = x * w[GDN_CONV - 1:GDN_CONV]
    for d in range(1, GDN_CONV):
        xr = pltpu.roll(x, d, 0)
        head = jnp.where(row8 < d, pltpu.roll(cprev, d, 0), xr[0:8])
        xs = jnp.concatenate([head, xr[8:]], axis=0)
        acc = acc + xs * w[GDN_CONV - 1 - d:GDN_CONV - d]
    carry_ref[...] = x[tm - 8:tm]
    conv = _silu(acc)

    nq = GDN_HEADS * GDN_DK
    beta_all, g_all = _gdn_gates(misc_ref[...], alog_ref[...], dtb_ref[...])
    z = z_ref[...]
    gnorm = gnorm_ref[...]
    c = GDN_CHUNK
    blk = 2 * c
    ri = lax.broadcasted_iota(jnp.int32, (blk, blk), 0)
    ci = lax.broadcasted_iota(jnp.int32, (blk, blk), 1)
    tri2 = jnp.where((ri >= ci) & ((ri // c) == (ci // c)), 1.0, 0.0).astype(BF16)
    ns = GDN_HEADS * c
    rs_ = lax.broadcasted_iota(jnp.int32, (ns, ns), 0)
    cs_ = lax.broadcasted_iota(jnp.int32, (ns, ns), 1)
    same = (rs_ // c) == (cs_ // c)
    incl = same & (rs_ >= cs_)
    strict = same & (rs_ > cs_)
    heads = range(GDN_HEADS)
    stack = lambda parts: jnp.concatenate(parts, axis=0)

    solved = []
    for b2 in range(tm // blk):
        gblk = g_all[b2 * blk:(b2 + 1) * blk]
        g0, g1, g2 = _split3(gblk)
        d = functools.partial(jnp.dot, preferred_element_type=F32)
        gcum = d(tri2, g0) + d(tri2, g1) + d(tri2, g2)
        gcum_t = gcum.T
        for c2 in range(2):
            r0 = b2 * blk + c2 * c
            cr = slice(c2 * c, (c2 + 1) * c)
            qs = [_l2n(conv[r0:r0 + c, h * GDN_DK:(h + 1) * GDN_DK], GDN_DK ** -0.5) for h in heads]
            ks = [_l2n(conv[r0:r0 + c, nq + h * GDN_DK:nq + (h + 1) * GDN_DK], 1.0) for h in heads]
            vs = [conv[r0:r0 + c, 2 * nq + h * GDN_DV:2 * nq + (h + 1) * GDN_DV] for h in heads]
            gcs = [gcum[cr, GDN_HEADS + h:GDN_HEADS + h + 1] for h in heads]
            q_st, k_st, v_st, gc_st = stack(qs), stack(ks), stack(vs), stack(gcs)
            gr_st = jnp.concatenate([gcum_t[GDN_HEADS + h:GDN_HEADS + h + 1, cr] for h in heads], axis=1)
            beta_st = stack([beta_all[r0:r0 + c, h:h + 1] for h in heads])
            decay = jnp.exp(jnp.where(incl, gc_st - gr_st, -jnp.inf))
            eg_st = jnp.exp(gc_st)
            kk = _bdot_nt(k_st, k_st)
            mk = -jnp.where(strict, beta_st * kk * decay, 0.0)
            rhs = jnp.concatenate([v_st * beta_st, k_st * (beta_st * eg_st)], axis=-1)
            qk = _bdot_nt(q_st, k_st) * decay
            solved.append([r0, rhs, qk, qs, ks, gcs, eg_st, mk])

    n_sq = int(math.log2(c))
    for kq in range(n_sq):
        for item in solved:
            item[1] = item[1] + _dot_split(item[7], item[1])
            if kq < n_sq - 1:
                item[7] = _dot_split(item[7], item[7])

    state = [s_ref[h] for h in heads]
    for r0, rhs, qk, qs, ks, gcs, eg_st, _ in solved:
        outs = []
        for h in heads:
            hr = slice(h * c, (h + 1) * c)
            uu = rhs[hr, :GDN_DV]
            ww = rhs[hr, GDN_DV:]
            gc = gcs[h]
            glast = gc[c - 1:c, :]
            s_h = state[h]
            v_new = uu - _bdot(ww, s_h)
            o_h = _bdot(qs[h] * eg_st[hr], s_h) + _bdot(qk[hr, h * c:(h + 1) * c], v_new)
            kdec = ks[h] * jnp.exp(glast - gc)
            state[h] = s_h * jnp.exp(glast) + lax.dot_general(
                kdec.astype(BF16), v_new.astype(BF16), (((0,), (0,)), ((), ())),
                preferred_element_type=F32)
            zh = z[r0:r0 + c, h * GDN_DV:(h + 1) * GDN_DV]
            outs.append(_rms(o_h, gnorm) * _silu(zh))
        o_ref[r0:r0 + c, :] = jnp.concatenate(outs, axis=-1).astype(o_ref.dtype)
    for h in heads:
        s_ref[h] = state[h]

    @pl.when(i == pl.num_programs(0) - 1)
    def _():
        sfin_ref[...] = s_ref[...]


def _gdn_prompt(qkv, z, misc, w):
    rows = qkv.shape[0]
    tm = min(256, rows)
    row = lambda n: pl.BlockSpec((tm, n), lambda i: (i, 0))
    full = lambda a: pl.BlockSpec(a.shape, lambda i: (0,) * a.ndim)
    st = (GDN_HEADS, GDN_DK, GDN_DV)
    ins = [qkv, z, misc, w["gdn_conv_w"], w["gdn_alog_pad"], w["gdn_dtb_pad"], w["gdn_g_norm"]]
    return pl.pallas_call(
        _gdn_prompt_kernel,
        grid=(rows // tm,),
        in_specs=[row(GDN_QKV), row(GDN_HEADS * GDN_DV), row(HEAD_BLOCK)] + [full(a) for a in ins[3:]],
        out_specs=[row(GDN_HEADS * GDN_DV), pl.BlockSpec(st, lambda i: (0, 0, 0))],
        out_shape=[jax.ShapeDtypeStruct((rows, GDN_HEADS * GDN_DV), BF16), jax.ShapeDtypeStruct(st, F32)],
        scratch_shapes=[pltpu.VMEM(st, F32), pltpu.VMEM((8, GDN_QKV), F32)],
        compiler_params=_cparams("arbitrary"),
        name="gdn_prompt",
    )(*ins)


def _gdn_sample_kernel(t_new, qkv_ref, st_ref, z_ref, misc_ref, convw_ref, alog_ref, dtb_ref, gcol_ref,
                       s0_ref, o_ref, sout_ref,
                       s_ref, rows_ref, zo_ref, gate_ref, qt_ref, kt_ref, vt_ref, gt_ref, ot_ref):
    n_rows = qkv_ref.shape[0]
    n_b = n_rows // t_new
    nq = GDN_HEADS * GDN_DK
    x = qkv_ref[...]
    st = st_ref[...]
    w = convw_ref[...]
    tpos = lax.broadcasted_iota(jnp.int32, (n_rows, GDN_QKV), 0) % t_new
    acc = x * w[GDN_CONV - 1:GDN_CONV]
    for d in range(1, GDN_CONV):
        xr = pltpu.roll(x, d, 0)
        back = GDN_CONV - 1 - d
        sr = st if back == 0 else pltpu.roll(st, n_rows - back, 0)
        acc = acc + jnp.where(tpos < d, sr, xr) * w[GDN_CONV - 1 - d:GDN_CONV - d]
    conv = _silu(acc)
    parts = []
    for h in range(GDN_HEADS):
        parts.append(_l2n(conv[:, h * GDN_DK:(h + 1) * GDN_DK], GDN_DK ** -0.5))
    for h in range(GDN_HEADS):
        parts.append(_l2n(conv[:, nq + h * GDN_DK:nq + (h + 1) * GDN_DK], 1.0))
    parts.append(conv[:, 2 * nq:])
    feats = jnp.concatenate(parts, axis=-1)
    n_chunk = GDN_QKV // HEAD_BLOCK
    per_part = nq // HEAD_BLOCK
    for cc in range(n_chunk):
        rows_ref[cc] = feats[:, cc * HEAD_BLOCK:(cc + 1) * HEAD_BLOCK]
    zsil = _silu(z_ref[...])
    for cc in range(per_part):
        zo_ref[cc] = zsil[:, cc * HEAD_BLOCK:(cc + 1) * HEAD_BLOCK]
    beta_all, g_all = _gdn_gates(misc_ref[...], alog_ref[...], dtb_ref[...])
    lane = lax.broadcasted_iota(jnp.int32, beta_all.shape, 1)
    gate_ref[...] = jnp.where(lane < GDN_HEADS, beta_all, jnp.exp(g_all))

    for t in range(t_new):
        for cc in range(n_chunk):
            blk_t = rows_ref[cc, pl.ds(t, n_b, stride=t_new), :].T
            dst = (qt_ref, kt_ref, vt_ref)[cc // per_part]
            lo = (cc % per_part) * HEAD_BLOCK
            dst[t, lo:lo + HEAD_BLOCK, :] = blk_t
        gt_ref[t] = gate_ref[pl.ds(t, n_b, stride=t_new), :].T

    n_blk = (GDN_HEADS * GDN_DK * GDN_DV) // HEAD_BLOCK
    per = HEAD_BLOCK // GDN_DV
    for cb in range(n_blk):
        s_ref[per * cb:per * (cb + 1)] = s0_ref[:, cb * HEAD_BLOCK:(cb + 1) * HEAD_BLOCK].T.reshape(
            per, GDN_DV, n_b)

    gcol = gcol_ref[...]
    for t in range(t_new):
        for h in range(GDN_HEADS):
            egr = gt_ref[t, GDN_HEADS + h:GDN_HEADS + h + 1, :]
            betar = gt_ref[t, h:h + 1, :]
            base = h * GDN_DK

            def p1(dk, racc):
                kb = kt_ref[t, pl.ds(base + dk, 1), :]
                return racc + s_ref[base + dk] * kb

            rr = lax.fori_loop(0, GDN_DK, p1, jnp.zeros((GDN_DV, n_b), F32), unroll=8) * egr
            dd = betar * (vt_ref[t, base:base + GDN_DV, :] - rr)

            def p2(dk, oacc):
                kb = kt_ref[t, pl.ds(base + dk, 1), :]
                qb = qt_ref[t, pl.ds(base + dk, 1), :]
                sn = s_ref[base + dk] * egr + kb * dd
                s_ref[base + dk] = sn
                return oacc + sn * qb

            oo = lax.fori_loop(0, GDN_DK, p2, jnp.zeros((GDN_DV, n_b), F32), unroll=8)
            on = oo * lax.rsqrt(jnp.mean(oo * oo, axis=0, keepdims=True) + RMS_EPS) * gcol
            ot_ref[base:base + GDN_DV, :] = on
        on_rows = ot_ref[...].T
        for cc in range(per_part):
            zt = zo_ref[cc, pl.ds(t, n_b, stride=t_new), :]
            zo_ref[cc, pl.ds(t, n_b, stride=t_new), :] = on_rows[:, cc * HEAD_BLOCK:(cc + 1) * HEAD_BLOCK] * zt
    for cc in range(per_part):
        o_ref[:, cc * HEAD_BLOCK:(cc + 1) * HEAD_BLOCK] = zo_ref[cc]

    for cb in range(n_blk):
        sout_ref[:, cb * HEAD_BLOCK:(cb + 1) * HEAD_BLOCK] = s_ref[per * cb:per * (cb + 1)].reshape(
            HEAD_BLOCK, n_b).T


def _gdn_sample(qkv, st_rows, z, misc, w, s0, t_new):
    rows = qkv.shape[0]
    n_b = rows // t_new
    feat = GDN_HEADS * GDN_DK
    n_state = GDN_HEADS * GDN_DK * GDN_DV
    ins = [qkv, st_rows, z, misc, w["gdn_conv_w"], w["gdn_alog_pad"], w["gdn_dtb_pad"], w["gdn_g_col"], s0]
    return pl.pallas_call(
        functools.partial(_gdn_sample_kernel, t_new),
        out_shape=[jax.ShapeDtypeStruct((rows, GDN_HEADS * GDN_DV), F32),
                   jax.ShapeDtypeStruct((n_b, n_state), F32)],
        scratch_shapes=[
            pltpu.VMEM((GDN_HEADS * GDN_DK, GDN_DV, n_b), F32),
            pltpu.VMEM((GDN_QKV // HEAD_BLOCK, rows, HEAD_BLOCK), F32),
            pltpu.VMEM((GDN_HEADS * GDN_DV // HEAD_BLOCK, rows, HEAD_BLOCK), F32),
            pltpu.VMEM((rows, HEAD_BLOCK), F32),
            pltpu.VMEM((t_new, feat, n_b), F32),
            pltpu.VMEM((t_new, feat, n_b), F32),
            pltpu.VMEM((t_new, GDN_HEADS * GDN_DV, n_b), F32),
            pltpu.VMEM((t_new, HEAD_BLOCK, n_b), F32),
            pltpu.VMEM((GDN_HEADS * GDN_DV, n_b), F32),
        ],
        compiler_params=pltpu.CompilerParams(vmem_limit_bytes=VMEM_LIMIT_BYTES),
        name="gdn_sample",
    )(*ins)


def _cmul(ar, ai, br, bi):
    return ar * br - ai * bi, ar * bi + ai * br


def _s5_kernel(per_group_state, u_ref, bbr_ref, bbi_ref, ar_ref, ai_ref, cr_ref, ci_ref, d_ref,
               wglu_ref, bglu_ref, h0r_ref, h0i_ref, o_ref, hr_out_ref, hi_out_ref,
               br_ref, bi_ref, cr_carry_ref, ci_carry_ref):
    i = pl.program_id(0)
    tm = u_ref.shape[0]
    n_grp = tm // 8

    if not per_group_state:
        @pl.when(i == 0)
        def _():
            cr_carry_ref[...] = jnp.zeros_like(cr_carry_ref)
            ci_carry_ref[...] = jnp.zeros_like(ci_carry_ref)

    u = u_ref[...]
    ub = u.astype(BF16)
    br_ref[...] = jnp.dot(ub, bbr_ref[...], preferred_element_type=F32)
    bi_ref[...] = jnp.dot(ub, bbi_ref[...], preferred_element_type=F32)

    ar = ar_ref[...]
    ai = ai_ref[...]
    p1 = (ar, ai)
    p2 = _cmul(*p1, *p1)
    p3 = _cmul(*p2, *p1)
    p4 = _cmul(*p2, *p2)
    p5 = _cmul(*p4, *p1)
    p6 = _cmul(*p4, *p2)
    p7 = _cmul(*p4, *p3)
    p8 = _cmul(*p4, *p4)
    row8 = lax.broadcasted_iota(jnp.int32, (8, S5_LANES), 0)
    pw_r = jnp.zeros((8, S5_LANES), F32)
    pw_i = jnp.zeros((8, S5_LANES), F32)
    for t, pw in enumerate((p1, p2, p3, p4, p5, p6, p7, p8)):
        pw_r = jnp.where(row8 == t, pw[0], pw_r)
        pw_i = jnp.where(row8 == t, pw[1], pw_i)

    def body(gi, carry):
        r0 = pl.multiple_of(gi * 8, 8)
        a = br_ref[pl.ds(r0, 8), :]
        b = bi_ref[pl.ds(r0, 8), :]
        for d, pw in ((1, p1), (2, p2), (4, p4)):
            a_s = jnp.where(row8 >= d, pltpu.roll(a, d, 0), 0.0)
            b_s = jnp.where(row8 >= d, pltpu.roll(b, d, 0), 0.0)
            da, db = _cmul(pw[0], pw[1], a_s, b_s)
            a = a + da
            b = b + db
        if per_group_state:
            c_r = h0r_ref[pl.ds(gi, 1), :]
            c_i = h0i_ref[pl.ds(gi, 1), :]
        else:
            c_r, c_i = carry
        da, db = _cmul(pw_r, pw_i, c_r, c_i)
        a = a + da
        b = b + db
        br_ref[pl.ds(r0, 8), :] = a
        bi_ref[pl.ds(r0, 8), :] = b
        if per_group_state:
            hr_out_ref[pl.ds(gi, 1), :] = a[7:8]
            hi_out_ref[pl.ds(gi, 1), :] = b[7:8]
            return carry
        return a[7:8], b[7:8]

    if per_group_state:
        zero = jnp.zeros((1, S5_LANES), F32)
        lax.fori_loop(0, n_grp, body, (zero, zero))
    else:
        c_fin = lax.fori_loop(0, n_grp, body, (cr_carry_ref[...], ci_carry_ref[...]))
        cr_carry_ref[...] = c_fin[0]
        ci_carry_ref[...] = c_fin[1]
        hr_out_ref[...] = c_fin[0]
        hi_out_ref[...] = c_fin[1]

    y = (jnp.dot(br_ref[...].astype(BF16), cr_ref[...], preferred_element_type=F32)
         - jnp.dot(bi_ref[...].astype(BF16), ci_ref[...], preferred_element_type=F32)
         + d_ref[...] * u)
    zg = jax.nn.gelu(y)
    gate = jax.nn.sigmoid(jnp.dot(zg.astype(BF16), wglu_ref[...], preferred_element_type=F32) + bglu_ref[...])
    o_ref[...] = (zg * gate).astype(o_ref.dtype)


def _s5(su, w, h0r, h0i, per_group_state, out_dtype):
    rows = su.shape[0]
    tm = min(256, rows)
    n_grp = tm // 8
    row = lambda n: pl.BlockSpec((tm, n), lambda i: (i, 0))
    full = lambda a: pl.BlockSpec(a.shape, lambda i: (0,) * a.ndim)
    wlist = [w["s5_bb_re"], w["s5_bb_im"], w["s5_ab_re"], w["s5_ab_im"], w["s5_c_re"], w["s5_c_im"],
             w["s5_d"], w["s5_w_glu"], w["s5_b_glu"]]
    if per_group_state:
        st_spec = pl.BlockSpec((n_grp, S5_LANES), lambda i: (i, 0))
        st_shape = jax.ShapeDtypeStruct((rows // 8, S5_LANES), F32)
    else:
        st_spec = pl.BlockSpec((1, S5_LANES), lambda i: (0, 0))
        st_shape = jax.ShapeDtypeStruct((1, S5_LANES), F32)
    return pl.pallas_call(
        functools.partial(_s5_kernel, per_group_state),
        grid=(rows // tm,),
        in_specs=[row(S5_WIDTH)] + [full(a) for a in wlist] + [st_spec, st_spec],
        out_specs=[row(S5_WIDTH), st_spec, st_spec],
        out_shape=[jax.ShapeDtypeStruct((rows, S5_WIDTH), out_dtype), st_shape, st_shape],
        scratch_shapes=[pltpu.VMEM((tm, S5_LANES), F32), pltpu.VMEM((tm, S5_LANES), F32),
                        pltpu.VMEM((1, S5_LANES), F32), pltpu.VMEM((1, S5_LANES), F32)],
        compiler_params=_cparams("arbitrary"),
        name="s5_sample" if per_group_state else "s5_prompt",
    )(su, *wlist, h0r, h0i)


def _mixout_kernel(latent, x_ref, oa_ref, og_ref, os_ref, wuv_ref, woa_ref, wob_ref, woc_ref,
                   gpost_ref, gxa_ref, wq_ref, x2_ref, q_ref):
    if latent:
        mixed = None
        for h in range(MLA_HEADS):
            o_h = jnp.dot(oa_ref[h].astype(BF16), wuv_ref[h], preferred_element_type=F32)
            t = jnp.dot(o_h.astype(BF16), woa_ref[h * MLA_V:(h + 1) * MLA_V, :], preferred_element_type=F32)
            mixed = t if mixed is None else mixed + t
    else:
        mixed = jnp.dot(oa_ref[...].astype(BF16), woa_ref[...], preferred_element_type=F32)
    mixed = mixed + jnp.dot(og_ref[...].astype(BF16), wob_ref[...], preferred_element_type=F32)
    mixed = mixed + jnp.dot(os_ref[...].astype(BF16), woc_ref[...], preferred_element_type=F32)
    x2 = x_ref[...] + _rms(mixed, gpost_ref[...])
    x2_ref[...] = x2
    hq = _rms(x2, gxa_ref[...]).astype(BF16)
    q_ref[...] = jnp.dot(hq, wq_ref[...], preferred_element_type=F32) * XA_SCALE


def _mixout(latent, x, o_mla, o_gdn, o_s5, w):
    rows, d = x.shape
    tm = min(512, rows)
    row = lambda n: pl.BlockSpec((tm, n), lambda i: (i, 0))
    full = lambda a: pl.BlockSpec(a.shape, lambda i: (0,) * a.ndim)
    if latent:
        oa_spec = pl.BlockSpec((MLA_HEADS, tm, MLA_KV_RANK), lambda i: (0, i, 0))
    else:
        oa_spec = row(MLA_HEADS * MLA_V)
    wl = [w["wuv_h"], w["wo_a"], w["wo_b"], w["wo_c"], w["mix_g_post"], w["xa_g_pre"], w["xa_w_q"]]
    return pl.pallas_call(
        functools.partial(_mixout_kernel, latent),
        grid=(rows // tm,),
        in_specs=[row(d), oa_spec, row(o_gdn.shape[1]), row(o_s5.shape[1])] + [full(a) for a in wl],
        out_specs=[row(d), row(d)],
        out_shape=[jax.ShapeDtypeStruct((rows, d), F32), jax.ShapeDtypeStruct((rows, d), F32)],
        compiler_params=_cparams("parallel"),
        name="mixout_sample" if latent else "mixout_prompt",
    )(x, o_mla, o_gdn, o_s5, *wl)


def _xattn_heads_merged_kernel(n_b, q_ref, mk_ref, mv_ref, o_ref):
    rows = q_ref.shape[0] // n_b
    n_mem = mk_ref.shape[1]
    for bi in range(n_b):
        q = q_ref[bi * rows:(bi + 1) * rows, :]
        q_all = jnp.concatenate([q[:, h * XA_HEAD_DIM:(h + 1) * XA_HEAD_DIM] for h in range(XA_HEADS)], axis=0)
        k2 = mk_ref[bi].reshape(n_mem * XA_HEADS, XA_HEAD_DIM)
        v2 = mv_ref[bi].reshape(n_mem * XA_HEADS, XA_HEAD_DIM)
        s = _bdot_nt(q_all, k2)
        q_head = lax.broadcasted_iota(jnp.int32, s.shape, 0) // rows
        m_head = lax.broadcasted_iota(jnp.int32, s.shape, 1) % XA_HEADS
        s = jnp.where(q_head == m_head, s, -jnp.inf)
        m = jnp.max(s, axis=-1, keepdims=True)
        pm = jnp.exp(s - m)
        pr = pm / jnp.sum(pm, axis=-1, keepdims=True)
        o_all = _bdot(pr, v2)
        o_ref[bi * rows:(bi + 1) * rows, :] = jnp.concatenate(
            [o_all[h * rows:(h + 1) * rows] for h in range(XA_HEADS)], axis=-1).astype(o_ref.dtype)


def _xattn_kernel(n_b, q_ref, mk_ref, mv_ref, o_ref):
    rows = q_ref.shape[0] // n_b
    for bi in range(n_b):
        q = q_ref[bi * rows:(bi + 1) * rows, :]
        outs = []
        for h in range(XA_HEADS):
            hs = slice(h * XA_HEAD_DIM, (h + 1) * XA_HEAD_DIM)
            mk, mv = mk_ref[bi, :, hs], mv_ref[bi, :, hs]
            s = _bdot_nt(q[:, hs], mk)
            m = jnp.max(s, axis=-1, keepdims=True)
            pm = jnp.exp(s - m)
            pr = pm / jnp.sum(pm, axis=-1, keepdims=True)
            outs.append(_bdot(pr, mv))
        o_ref[bi * rows:(bi + 1) * rows, :] = jnp.concatenate(outs, axis=-1).astype(o_ref.dtype)


def _xattn(q, mem_k, mem_v, layer, shared, t_new):
    rows, d = q.shape
    n_mem = mem_k.shape[2]
    if shared:
        tm, n_b = min(512, rows), 1
        mspec = pl.BlockSpec((None, 1, n_mem, d), lambda i: (layer, 0, 0, 0))
        body = _xattn_kernel
    else:
        n_b = min(XA_SEQ_PER_STEP, rows // t_new)
        tm = n_b * t_new
        mspec = pl.BlockSpec((None, n_b, n_mem, XA_HEADS, XA_HEAD_DIM), lambda i: (layer, i, 0, 0, 0))
        body = _xattn_heads_merged_kernel
    return pl.pallas_call(
        functools.partial(body, n_b),
        grid=(rows // tm,),
        in_specs=[pl.BlockSpec((tm, d), lambda i: (i, 0)), mspec, mspec],
        out_specs=pl.BlockSpec((tm, d), lambda i: (i, 0)),
        out_shape=jax.ShapeDtypeStruct((rows, d), BF16),
        compiler_params=_cparams("parallel"),
        name="xattn_prompt" if shared else "xattn_sample",
    )(q, mem_k, mem_v)


def _xaout_kernel(x_ref, o_ref, wo_ref, g_ref, y_ref):
    xa = jnp.dot(o_ref[...], wo_ref[...], preferred_element_type=F32)
    y_ref[...] = x_ref[...] + _rms(xa, g_ref[...])


def _xaout(x, o, w):
    rows, d = x.shape
    tm = min(512, rows)
    row = pl.BlockSpec((tm, d), lambda i: (i, 0))
    return pl.pallas_call(
        _xaout_kernel,
        grid=(rows // tm,),
        in_specs=[row, row, pl.BlockSpec((d, d), lambda i: (0, 0)), pl.BlockSpec((1, d), lambda i: (0, 0))],
        out_specs=row,
        out_shape=jax.ShapeDtypeStruct((rows, d), F32),
        compiler_params=_cparams("parallel"),
        name="xattn_out",
    )(x, o, w["xa_w_o"], w["xa_g_post"])


def _memproj_kernel(m_ref, wk_ref, wv_ref, k_ref, v_ref):
    m = m_ref[...].astype(BF16)
    for l in range(wk_ref.shape[0]):
        k_ref[l] = jnp.dot(m, wk_ref[l], preferred_element_type=F32)
        v_ref[l] = jnp.dot(m, wv_ref[l], preferred_element_type=F32)


def _memproj(mem, wk, wv):
    depth = wk.shape[0]
    shp = jax.ShapeDtypeStruct((depth,) + mem.shape, F32)
    return pl.pallas_call(
        _memproj_kernel,
        out_shape=[shp, shp],
        compiler_params=pltpu.CompilerParams(vmem_limit_bytes=VMEM_LIMIT_BYTES),
        name="mem_kv_proj",
    )(mem, wk, wv)


def _layer_weights(l, p):
    d_model = p["w_in"].shape[1]
    w = {}
    r1 = lambda a: a[l].reshape(1, -1).astype(F32)
    for name in ("ffn1", "ffn2"):
        w[name + "_g_pre"] = r1(p[name + "_g_pre"])
        w[name + "_g_post"] = r1(p[name + "_g_post"])
        for s in ("_w_gate", "_w_up", "_w_down"):
            w[name + s] = p[name + s][l].astype(BF16)
    for name in ("mix_g_pre", "mix_g_post", "mla_g_q", "mla_g_kv", "xa_g_pre", "xa_g_post", "gdn_g_norm",
                 "s5_d", "s5_b_glu"):
        w[name] = r1(p[name])
    w["gdn_g_col"] = p["gdn_g_norm"][l].reshape(-1, 1).astype(F32)

    w_in = p["w_in"][l]
    offs = np.cumsum([0, MLA_Q_RANK, MLA_KV_RANK, MLA_ROPE, GDN_QKV, GDN_HEADS * GDN_DV, GDN_HEADS, GDN_HEADS,
                      S5_WIDTH])
    w_cq, w_ckv, w_kpe, w_qkv, w_z, w_b, w_a, w_su = [w_in[:, offs[i]:offs[i + 1]] for i in range(8)]
    half = MLA_ROPE // 2
    zeros = lambda n: jnp.zeros((d_model, n), w_in.dtype)
    w_kpe_sw = jnp.concatenate([-w_kpe[:, half:], w_kpe[:, :half]], axis=1)
    tail = HEAD_BLOCK - MLA_NOPE - MLA_ROPE
    w_kpl = jnp.concatenate([zeros(MLA_NOPE), w_kpe, zeros(tail)], axis=1)
    w_kpls = jnp.concatenate([zeros(MLA_NOPE), w_kpe_sw, zeros(tail)], axis=1)
    w_misc = jnp.concatenate([w_b, w_a, zeros(HEAD_BLOCK - 2 * GDN_HEADS)], axis=1)
    w["w_big"] = jnp.concatenate([w_cq, w_ckv, w_kpl, w_kpls, w_misc, w_qkv, w_z, w_su], axis=1).astype(BF16)

    w_uq = p["mla_w_uq"][l].reshape(MLA_Q_RANK, MLA_HEADS, MLA_NOPE + MLA_ROPE)
    nope, x1, x2 = w_uq[..., :MLA_NOPE], w_uq[..., MLA_NOPE:MLA_NOPE + half], w_uq[..., MLA_NOPE + half:]
    zq = lambda n: jnp.zeros((MLA_Q_RANK, MLA_HEADS, n), w_uq.dtype)
    w["wq_a"] = jnp.concatenate([nope, x1, x2, zq(tail)], axis=-1).reshape(MLA_Q_RANK, -1).astype(BF16)
    w["wq_b"] = jnp.concatenate([zq(MLA_NOPE), -x2, x1, zq(tail)], axis=-1).reshape(MLA_Q_RANK, -1).astype(BF16)

    w_uk = p["mla_w_uk"][l]
    w_uv = p["mla_w_uv"][l]
    zk = jnp.zeros((MLA_KV_RANK, MLA_HEADS, HEAD_BLOCK - MLA_NOPE), w_uk.dtype)
    w["wuk_pad"] = jnp.concatenate([w_uk, zk], axis=-1).reshape(MLA_KV_RANK, -1).astype(BF16)
    wuk_t = jnp.transpose(w_uk, (1, 2, 0))
    w["wuk_t"] = jnp.concatenate(
        [wuk_t, jnp.zeros((MLA_HEADS, HEAD_BLOCK - MLA_NOPE, MLA_KV_RANK), w_uk.dtype)], axis=1).astype(BF16)
    w["wuv"] = w_uv.reshape(MLA_KV_RANK, -1).T.astype(BF16)
    w["wuv_h"] = jnp.transpose(w_uv, (1, 0, 2)).astype(BF16)

    w_out = p["w_out"][l]
    n_a = MLA_HEADS * MLA_V
    n_b = n_a + GDN_HEADS * GDN_DV
    w["wo_a"] = w_out[:n_a].astype(BF16)
    w["wo_b"] = w_out[n_a:n_b].astype(BF16)
    w["wo_c"] = w_out[n_b:].astype(BF16)
    w["xa_w_q"] = p["xa_w_q"][l].astype(BF16)
    w["xa_w_o"] = p["xa_w_o"][l].astype(BF16)

    w["gdn_conv_w"] = p["gdn_conv_w"][l].astype(F32)
    pad_gate = lambda v: jnp.zeros((1, HEAD_BLOCK), F32).at[0, GDN_HEADS:2 * GDN_HEADS].set(v.astype(F32))
    w["gdn_alog_pad"] = pad_gate(p["gdn_a_log"][l])
    w["gdn_dtb_pad"] = pad_gate(p["gdn_dt_bias"][l])

    a_re, a_im = p["s5_a_re"][l].astype(F32), p["s5_a_im"][l].astype(F32)
    dt = jnp.exp(p["s5_log_dt"][l].astype(F32))[:, None]
    mag = jnp.exp(a_re * dt)
    ab_re, ab_im = mag * jnp.cos(a_im * dt), mag * jnp.sin(a_im * dt)
    den = a_re * a_re + a_im * a_im
    nr, ni = ab_re - 1.0, ab_im
    coef_re = (nr * a_re + ni * a_im) / den
    coef_im = (ni * a_re - nr * a_im) / den
    b_re, b_im = p["s5_b_re"][l].astype(F32), p["s5_b_im"][l].astype(F32)
    bb_re = coef_re[..., None] * b_re - coef_im[..., None] * b_im
    bb_im = coef_re[..., None] * b_im + coef_im[..., None] * b_re
    eye = jnp.eye(S5_GROUPS, dtype=F32)
    bd_in = lambda bb: jnp.einsum("gnp,gh->gphn", bb, eye).reshape(S5_WIDTH, S5_LANES).astype(BF16)
    bd_out = lambda cc: jnp.einsum("gpn,gh->gnhp", cc.astype(F32), eye).reshape(S5_LANES, S5_WIDTH).astype(BF16)
    w["s5_bb_re"], w["s5_bb_im"] = bd_in(bb_re), bd_in(bb_im)
    w["s5_c_re"], w["s5_c_im"] = bd_out(p["s5_c_re"][l]), bd_out(p["s5_c_im"][l])
    w["s5_ab_re"] = ab_re.reshape(1, S5_LANES)
    w["s5_ab_im"] = ab_im.reshape(1, S5_LANES)
    w["s5_w_glu"] = p["s5_w_glu"][l].astype(BF16)
    return w


def _rope_tables(pos):
    half = MLA_ROPE // 2
    inv = ROPE_THETA ** (-jnp.arange(half, dtype=F32) / half)
    ang = pos.astype(F32)[:, None] * inv[None, :]
    c, s = jnp.cos(ang), jnp.sin(ang)
    n = pos.shape[0]
    z0 = jnp.zeros((n, MLA_NOPE), F32)
    z1 = jnp.zeros((n, HEAD_BLOCK - MLA_NOPE - MLA_ROPE), F32)
    return jnp.concatenate([z0, c, c, z1], axis=1), jnp.concatenate([z0, s, s, z1], axis=1)


def kernel(x_prompt, x_sample, mem_prompt, cache_ckv, cache_kpe, page_table, cache_mem_k, cache_mem_v, state_gdn, state_gdn_conv, state_s5_re, state_s5_im, ffn1_g_pre, ffn1_g_post, ffn1_w_gate, ffn1_w_up, ffn1_w_down, mix_g_pre, mix_g_post, w_in, w_out, mla_g_q, mla_w_uq, mla_g_kv, mla_w_uk, mla_w_uv, gdn_conv_w, gdn_a_log, gdn_dt_bias, gdn_g_norm, s5_a_re, s5_a_im, s5_log_dt, s5_b_re, s5_b_im, s5_c_re, s5_c_im, s5_d, s5_w_glu, s5_b_glu, xa_g_pre, xa_g_post, xa_w_q, xa_w_k, xa_w_v, xa_w_o, ffn2_g_pre, ffn2_g_post, ffn2_w_gate, ffn2_w_up, ffn2_w_down):
    params = dict(
        ffn1_g_pre=ffn1_g_pre, ffn1_g_post=ffn1_g_post, ffn1_w_gate=ffn1_w_gate, ffn1_w_up=ffn1_w_up,
        ffn1_w_down=ffn1_w_down, mix_g_pre=mix_g_pre, mix_g_post=mix_g_post, w_in=w_in, w_out=w_out,
        mla_g_q=mla_g_q, mla_w_uq=mla_w_uq, mla_g_kv=mla_g_kv, mla_w_uk=mla_w_uk, mla_w_uv=mla_w_uv,
        gdn_conv_w=gdn_conv_w, gdn_a_log=gdn_a_log, gdn_dt_bias=gdn_dt_bias, gdn_g_norm=gdn_g_norm,
        s5_a_re=s5_a_re, s5_a_im=s5_a_im, s5_log_dt=s5_log_dt, s5_b_re=s5_b_re, s5_b_im=s5_b_im,
        s5_c_re=s5_c_re, s5_c_im=s5_c_im, s5_d=s5_d, s5_w_glu=s5_w_glu, s5_b_glu=s5_b_glu,
        xa_g_pre=xa_g_pre, xa_g_post=xa_g_post, xa_w_q=xa_w_q, xa_w_o=xa_w_o,
        ffn2_g_pre=ffn2_g_pre, ffn2_g_post=ffn2_g_post, ffn2_w_gate=ffn2_w_gate, ffn2_w_up=ffn2_w_up,
        ffn2_w_down=ffn2_w_down)
    depth = w_in.shape[0]
    bsz, seqlen, d_model = x_prompt.shape
    dec_b, dec_t, _ = x_sample.shape
    n_mem = mem_prompt.shape[1]
    past_len = page_table.shape[1] * PAGE_SIZE
    n_s = dec_b * dec_t

    tc_p, ts_p = _rope_tables(jnp.arange(seqlen))
    tc_s, ts_s = _rope_tables(jnp.tile(past_len + jnp.arange(dec_t), dec_b))

    mem_k, mem_v = _memproj(mem_prompt.reshape(n_mem, d_model), xa_w_k.astype(BF16), xa_w_v.astype(BF16))
    mem_k4 = mem_k.reshape(depth, bsz, n_mem, d_model)
    mem_v4 = mem_v.reshape(depth, bsz, n_mem, d_model)
    cache_kpe_t = jnp.swapaxes(cache_kpe, 2, 3)

    yp = x_prompt.reshape(seqlen, d_model)
    ys = x_sample.reshape(n_s, d_model)
    zero_state = jnp.zeros((1, S5_LANES), F32)
    outs = {k: [] for k in ("p_ckv", "p_kpe", "p_gdn", "p_conv", "p_s5r", "p_s5i",
                            "s_ckv", "s_kpe", "s_gdn", "s_conv", "s_s5r", "s_s5i")}
    weights = [_layer_weights(l, params) for l in range(depth)]
    for l, w in enumerate(weights):
        ys = _half_ffn(ys, w["ffn1_g_pre"], w["ffn1_g_post"], w["ffn1_w_gate"], w["ffn1_w_up"], w["ffn1_w_down"])
        ckv, kpe, qkv, z, misc, su, qlat, qpe = _mixprep(True, ys, w, tc_s, ts_s)
        o_lat = _paged_attention(l, qlat, qpe, ckv, kpe, cache_ckv, cache_kpe_t, page_table, dec_t)
        st_rows = jnp.pad(state_gdn_conv[l], ((0, 0), (0, dec_t - (GDN_CONV - 1)), (0, 0))).reshape(n_s, GDN_QKV)
        o_gdn, gdn_s = _gdn_sample(qkv, st_rows, z, misc, w, state_gdn[l].reshape(dec_b, -1), dec_t)
        o_s5, h_re, h_im = _s5(su, w, state_s5_re[l].reshape(dec_b, S5_LANES),
                               state_s5_im[l].reshape(dec_b, S5_LANES), True, F32)
        x2, xq = _mixout(True, ys, o_lat, o_gdn, o_s5, w)
        o_xa = _xattn(xq, cache_mem_k, cache_mem_v, l, False, dec_t)
        ys = _xaout(x2, o_xa, w)
        ys = _half_ffn(ys, w["ffn2_g_pre"], w["ffn2_g_post"], w["ffn2_w_gate"], w["ffn2_w_up"], w["ffn2_w_down"])
        outs["s_ckv"].append(ckv.reshape(dec_b, dec_t, MLA_KV_RANK))
        outs["s_kpe"].append(kpe.reshape(dec_b, dec_t, MLA_ROPE))
        outs["s_gdn"].append(gdn_s.reshape(dec_b, GDN_HEADS, GDN_DK, GDN_DV))
        outs["s_conv"].append(qkv.reshape(dec_b, dec_t, GDN_QKV)[:, dec_t - (GDN_CONV - 1):])
        outs["s_s5r"].append(h_re.reshape(dec_b, S5_GROUPS, S5_STATE))
        outs["s_s5i"].append(h_im.reshape(dec_b, S5_GROUPS, S5_STATE))

    for l, w in enumerate(weights):
        yp = _half_ffn(yp, w["ffn1_g_pre"], w["ffn1_g_post"], w["ffn1_w_gate"], w["ffn1_w_up"], w["ffn1_w_down"])
        ckv, kpe, qkv, z, misc, su, q, k, v = _mixprep(False, yp, w, tc_p, ts_p)
        o_mla = _prompt_attention(q, k, v)
        o_gdn, gdn_s = _gdn_prompt(qkv, z, misc, w)
        o_s5, h_re, h_im = _s5(su, w, zero_state, zero_state, False, BF16)
        x2, xq = _mixout(False, yp, o_mla, o_gdn, o_s5, w)
        o_xa = _xattn(xq, mem_k4, mem_v4, l, True, dec_t)
        yp = _xaout(x2, o_xa, w)
        yp = _half_ffn(yp, w["ffn2_g_pre"], w["ffn2_g_post"], w["ffn2_w_gate"], w["ffn2_w_up"], w["ffn2_w_down"])
        outs["p_ckv"].append(ckv.reshape(bsz, seqlen, MLA_KV_RANK))
        outs["p_kpe"].append(kpe.reshape(bsz, seqlen, MLA_ROPE))
        outs["p_gdn"].append(gdn_s.reshape(bsz, GDN_HEADS, GDN_DK, GDN_DV))
        outs["p_conv"].append(qkv[seqlen - (GDN_CONV - 1):].reshape(bsz, GDN_CONV - 1, GDN_QKV))
        outs["p_s5r"].append(h_re.reshape(bsz, S5_GROUPS, S5_STATE))
        outs["p_s5i"].append(h_im.reshape(bsz, S5_GROUPS, S5_STATE))

    st = {k: jnp.stack(v) for k, v in outs.items()}
    p_mem_k = mem_k.reshape(depth, bsz, n_mem, XA_HEADS, XA_HEAD_DIM)
    p_mem_v = mem_v.reshape(depth, bsz, n_mem, XA_HEADS, XA_HEAD_DIM)
    return (yp.reshape(bsz, seqlen, d_model), ys.reshape(dec_b, dec_t, d_model),
            st["p_ckv"], st["p_kpe"], st["p_gdn"], st["p_conv"], st["p_s5r"], st["p_s5i"], p_mem_k, p_mem_v,
            st["s_ckv"], st["s_kpe"], st["s_gdn"], st["s_conv"], st["s_s5r"], st["s_s5i"])
```

```python
import functools
import math

import jax
import jax.numpy as jnp
import numpy as np
from jax import lax
from jax.experimental import pallas as pl
from jax.experimental.pallas import tpu as pltpu

F32 = jnp.float32
BF16 = jnp.bfloat16

RMS_EPS = 1e-6
MLA_HEADS = 8
MLA_NOPE = 64
MLA_ROPE = 32
MLA_V = 64
MLA_Q_RANK = 384
MLA_KV_RANK = 256
ROPE_THETA = 10000.0
HEAD_BLOCK = 128
PAGE_SIZE = 128
PAGES_PER_STEP = 32
PAGE_AHEAD = 2
PAGE_SLOTS = PAGE_AHEAD + 1
PAGE_GROUP = 8
ATTN_TILE = 1024
ATTN_SUB = 512
GDN_HEADS = 4
GDN_DK = 64
GDN_DV = 64
GDN_CONV = 4
GDN_CHUNK = 64
GDN_QKV = GDN_HEADS * (2 * GDN_DK + GDN_DV)
S5_GROUPS = 16
S5_GROUP = 16
S5_STATE = 64
S5_WIDTH = S5_GROUPS * S5_GROUP
S5_LANES = S5_GROUPS * S5_STATE
XA_HEADS = 4
XA_HEAD_DIM = 256
XA_SEQ_PER_STEP = 4

VMEM_LIMIT_BYTES = 56 * 1024 * 1024

MLA_SCALE = (MLA_NOPE + MLA_ROPE) ** -0.5
LOG2_E = math.log2(math.e)
XA_SCALE = XA_HEAD_DIM ** -0.5


def _cparams(*sem):
    return pltpu.CompilerParams(dimension_semantics=tuple(sem), vmem_limit_bytes=VMEM_LIMIT_BYTES)


def _rms(x, g):
    return x * lax.rsqrt(jnp.mean(x * x, axis=-1, keepdims=True) + RMS_EPS) * g


def _bdot(a, b):
    return jnp.dot(a.astype(BF16), b.astype(BF16), preferred_element_type=F32)


def _bdot_nt(a, b):
    return lax.dot_general(a.astype(BF16), b.astype(BF16), (((1,), (1,)), ((), ())),
                           preferred_element_type=F32)


def _split3(a):
    hi = a.astype(BF16)
    r1 = a - hi.astype(F32)
    mid = r1.astype(BF16)
    lo = (r1 - mid.astype(F32)).astype(BF16)
    return hi, mid, lo


def _dot_split(a, b):
    a0 = a.astype(BF16)
    a1 = (a - a0.astype(F32)).astype(BF16)
    b0 = b.astype(BF16)
    b1 = (b - b0.astype(F32)).astype(BF16)
    d = functools.partial(jnp.dot, preferred_element_type=F32)
    return d(a0, b0) + (d(a0, b1) + d(a1, b0))


def _silu(x):
    return x * jax.nn.sigmoid(x)


def _softplus(x):
    return jnp.maximum(x, 0.0) + jnp.log(1.0 + jnp.exp(-jnp.abs(x)))


def _ffn_kernel(x_ref, gpre_ref, gpost_ref, wg_ref, wu_ref, wd_ref, o_ref, xn_ref, acc_ref):
    j = pl.program_id(1)

    @pl.when(j == 0)
    def _():
        xn_ref[...] = _rms(x_ref[...], gpre_ref[...]).astype(BF16)
        acc_ref[...] = jnp.zeros_like(acc_ref)

    xn = xn_ref[...]
    g = jnp.dot(xn, wg_ref[...], preferred_element_type=F32)
    u = jnp.dot(xn, wu_ref[...], preferred_element_type=F32)
    h = (_silu(g) * u).astype(BF16)
    acc_ref[...] += jnp.dot(h, wd_ref[...], preferred_element_type=F32)

    @pl.when(j == pl.num_programs(1) - 1)
    def _():
        o_ref[...] = x_ref[...] + 0.5 * _rms(acc_ref[...], gpost_ref[...])


def _half_ffn(x, g_pre, g_post, wg, wu, wd, layer):
    rows, d = x.shape
    f = wg.shape[2]
    tm = min(512, rows)
    tf = f // 2
    return pl.pallas_call(
        _ffn_kernel,
        grid=(rows // tm, f // tf),
        in_specs=[
            pl.BlockSpec((tm, d), lambda i, j: (i, 0)),
            pl.BlockSpec((1, d), lambda i, j: (0, 0)),
            pl.BlockSpec((1, d), lambda i, j: (0, 0)),
            pl.BlockSpec((None, d, tf), lambda i, j: (layer, 0, j)),
            pl.BlockSpec((None, d, tf), lambda i, j: (layer, 0, j)),
            pl.BlockSpec((None, tf, d), lambda i, j: (layer, j, 0)),
        ],
        out_specs=pl.BlockSpec((tm, d), lambda i, j: (i, 0)),
        out_shape=jax.ShapeDtypeStruct((rows, d), F32),
        scratch_shapes=[pltpu.VMEM((tm, d), BF16), pltpu.VMEM((tm, d), F32)],
        compiler_params=_cparams("parallel", "arbitrary"),
        name="half_ffn",
    )(x, g_pre, g_post, wg, wu, wd)


C_CQ = (0, 384)
C_CKV = (384, 640)
C_KPL = (640, 768)
C_KPLS = (768, 896)
C_MISC = (896, 1024)
C_QKV = (1024, 1792)
C_Z = (1792, 2048)
C_SU = (2048, 2304)
W_BIG = 2304


def _mixprep_body(sample, x_ref, g_ref, wbig_ref, gq_ref, wqa_ref, wqb_ref, gkv_ref, tc_ref, ts_ref,
                  wk_ref, wv_ref, ckv_ref, kpe_ref, qkv_ref, z_ref, misc_ref, su_ref, a_ref, b_ref, c_ref):
    u = _rms(x_ref[...], g_ref[...]).astype(BF16)
    y = jnp.dot(u, wbig_ref[...], preferred_element_type=F32)
    sl = lambda c: y[:, c[0]:c[1]]
    tc = tc_ref[...]
    ts = ts_ref[...]
    kpe_pl = sl(C_KPL) * tc + sl(C_KPLS) * ts
    kpe_ref[...] = kpe_pl[:, MLA_NOPE:MLA_NOPE + MLA_ROPE]
    ckv_n = _rms(sl(C_CKV), gkv_ref[...])
    ckv_ref[...] = ckv_n
    qkv_ref[...] = sl(C_QKV)
    z_ref[...] = sl(C_Z)
    misc_ref[...] = sl(C_MISC)
    su_ref[...] = sl(C_SU)

    cqn = _rms(sl(C_CQ), gq_ref[...]).astype(BF16)
    qa = jnp.dot(cqn, wqa_ref[...], preferred_element_type=F32)
    qb = jnp.dot(cqn, wqb_ref[...], preferred_element_type=F32)
    lane = lax.broadcasted_iota(jnp.int32, (1, HEAD_BLOCK), 1)
    qmul = jnp.where(lane < MLA_NOPE, 1.0, 0.0) + tc
    ckb = ckv_n.astype(BF16)
    if sample:
        qlat_ref, qpe_ref = a_ref, b_ref
        for h in range(MLA_HEADS):
            hs = slice(h * HEAD_BLOCK, (h + 1) * HEAD_BLOCK)
            qh = qa[:, hs] * qmul + qb[:, hs] * ts
            qlat_ref[h] = jnp.dot(qa[:, hs].astype(BF16), wk_ref[h], preferred_element_type=F32) * MLA_SCALE
            qpe_ref[h] = qh[:, MLA_NOPE:MLA_NOPE + MLA_ROPE] * MLA_SCALE
    else:
        q_ref, k_ref, v_ref = a_ref, b_ref, c_ref
        kn = jnp.dot(ckb, wk_ref[...], preferred_element_type=F32)
        for h in range(MLA_HEADS):
            hs = slice(h * HEAD_BLOCK, (h + 1) * HEAD_BLOCK)
            qh = qa[:, hs] * qmul + qb[:, hs] * ts
            q_ref[h] = (qh * (MLA_SCALE * LOG2_E)).astype(BF16)
            k_ref[h] = (kn[:, hs] + kpe_pl).astype(BF16)
        v_ref[...] = _bdot_nt(wv_ref[...], ckb).astype(BF16)


def _mixprep_prompt_kernel(x_ref, g_ref, wbig_ref, gq_ref, wqa_ref, wqb_ref, gkv_ref, tc_ref, ts_ref,
                           wk_ref, wv_ref, ckv_ref, kpe_ref, qkv_ref, z_ref, misc_ref, su_ref,
                           q_ref, k_ref, v_ref):
    _mixprep_body(False, x_ref, g_ref, wbig_ref, gq_ref, wqa_ref, wqb_ref, gkv_ref, tc_ref, ts_ref,
                  wk_ref, wv_ref, ckv_ref, kpe_ref, qkv_ref, z_ref, misc_ref, su_ref, q_ref, k_ref, v_ref)


def _mixprep_sample_kernel(x_ref, g_ref, wbig_ref, gq_ref, wqa_ref, wqb_ref, gkv_ref, tc_ref, ts_ref,
                           wk_ref, ckv_ref, kpe_ref, qkv_ref, z_ref, misc_ref, su_ref, qlat_ref, qpe_ref):
    _mixprep_body(True, x_ref, g_ref, wbig_ref, gq_ref, wqa_ref, wqb_ref, gkv_ref, tc_ref, ts_ref,
                  wk_ref, None, ckv_ref, kpe_ref, qkv_ref, z_ref, misc_ref, su_ref, qlat_ref, qpe_ref, None)


def _mixprep(sample, x, w, tc, ts):
    rows, d = x.shape
    tm = min(512, rows)
    row = lambda n: pl.BlockSpec((tm, n), lambda i: (i, 0))
    full = lambda a: pl.BlockSpec(a.shape, lambda i: (0,) * a.ndim)
    hrow = lambda n: pl.BlockSpec((MLA_HEADS, tm, n), lambda i: (0, i, 0))
    common_in = [x, w["mix_g_pre"], w["w_big"], w["mla_g_q"], w["wq_a"], w["wq_b"], w["mla_g_kv"], tc, ts]
    common_specs = [row(d)] + [full(a) for a in common_in[1:7]] + [row(HEAD_BLOCK), row(HEAD_BLOCK)]
    common_out = [
        (jax.ShapeDtypeStruct((rows, MLA_KV_RANK), F32), row(MLA_KV_RANK)),
        (jax.ShapeDtypeStruct((rows, MLA_ROPE), F32), row(MLA_ROPE)),
        (jax.ShapeDtypeStruct((rows, GDN_QKV), F32), row(GDN_QKV)),
        (jax.ShapeDtypeStruct((rows, GDN_HEADS * GDN_DV), F32), row(GDN_HEADS * GDN_DV)),
        (jax.ShapeDtypeStruct((rows, HEAD_BLOCK), F32), row(HEAD_BLOCK)),
        (jax.ShapeDtypeStruct((rows, S5_WIDTH), F32), row(S5_WIDTH)),
    ]
    if sample:
        ins = common_in + [w["wuk_t"]]
        specs = common_specs + [full(w["wuk_t"])]
        outs = common_out + [
            (jax.ShapeDtypeStruct((MLA_HEADS, rows, MLA_KV_RANK), F32), hrow(MLA_KV_RANK)),
            (jax.ShapeDtypeStruct((MLA_HEADS, rows, MLA_ROPE), F32), hrow(MLA_ROPE)),
        ]
        body = _mixprep_sample_kernel
    else:
        ins = common_in + [w["wuk_pad"], w["wuv"]]
        specs = common_specs + [full(w["wuk_pad"]), full(w["wuv"])]
        outs = common_out + [
            (jax.ShapeDtypeStruct((MLA_HEADS, rows, HEAD_BLOCK), BF16), hrow(HEAD_BLOCK)),
            (jax.ShapeDtypeStruct((MLA_HEADS, rows, HEAD_BLOCK), BF16), hrow(HEAD_BLOCK)),
            (jax.ShapeDtypeStruct((MLA_HEADS * MLA_V, rows), BF16),
             pl.BlockSpec((MLA_HEADS * MLA_V, tm), lambda i: (0, i))),
        ]
        body = _mixprep_prompt_kernel
    return pl.pallas_call(
        body,
        grid=(rows // tm,),
        in_specs=specs,
        out_specs=[o[1] for o in outs],
        out_shape=[o[0] for o in outs],
        compiler_params=_cparams("parallel"),
        name="mixprep_sample" if sample else "mixprep_prompt",
    )(*ins)


def _attn_kernel(qi_ref, ki_ref, q_ref, k_ref, vt_ref, o_ref, m_ref, l_ref, acc_ref):
    p = pl.program_id(1)
    i = qi_ref[p]
    j = ki_ref[p]
    tq, tk = q_ref.shape[1], k_ref.shape[1]

    @pl.when(j == 0)
    def _():
        m_ref[...] = jnp.full(m_ref.shape, -jnp.inf, F32)
        l_ref[...] = jnp.zeros_like(l_ref)
        acc_ref[...] = jnp.zeros_like(acc_ref)

    sub = min(ATTN_SUB, tk)

    def step(masked):
        hrows = [slice(hh * MLA_V, (hh + 1) * MLA_V) for hh in range(2)]
        m = [m_ref[hh] for hh in range(2)]
        l = [l_ref[hh] for hh in range(2)]
        acc = [acc_ref[hrows[hh], :] for hh in range(2)]
        units = [(c, hh) for c in range(tk // sub) for hh in range(2)]

        def scores(c, hh):
            return lax.dot_general(k_ref[hh, c * sub:(c + 1) * sub, :], q_ref[hh], (((1,), (1,)), ((), ())),
                                   preferred_element_type=F32)

        st_next = scores(*units[0])
        for n, (c, hh) in enumerate(units):
            st = st_next
            if n + 1 < len(units):
                st_next = scores(*units[n + 1])
            if masked:
                key = lax.broadcasted_iota(jnp.int32, (sub, tq), 0) + c * sub
                qry = lax.broadcasted_iota(jnp.int32, (sub, tq), 1)
                st = jnp.where(key <= qry, st, -jnp.inf)
            m_new = jnp.maximum(m[hh], jnp.max(st, axis=0, keepdims=True))
            alpha = jnp.exp2(m[hh] - m_new)
            pt = jnp.exp2(st - m_new)
            l[hh] = alpha * l[hh] + jnp.sum(pt, axis=0, keepdims=True)
            acc[hh] = alpha * acc[hh] + jnp.dot(vt_ref[hrows[hh], c * sub:(c + 1) * sub], pt.astype(BF16),
                                                preferred_element_type=F32)
            m[hh] = m_new
        for hh in range(2):
            m_ref[hh] = m[hh]
            l_ref[hh] = l[hh]
            acc_ref[hrows[hh], :] = acc[hh]

    @pl.when(j < i)
    def _():
        step(False)

    @pl.when(j == i)
    def _():
        step(True)
        inv = jnp.concatenate([jnp.broadcast_to(1.0 / l_ref[hh], (MLA_V, tq)) for hh in range(2)], axis=0)
        o_ref[...] = (acc_ref[...] * inv).T.astype(o_ref.dtype)


def _prompt_attention(q, k, vt):
    rows = q.shape[1]
    t = min(ATTN_TILE, rows)
    n = rows // t
    qi = np.array([i for i in range(n) for _ in range(i + 1)], np.int32)
    ki = np.array([j for i in range(n) for j in range(i + 1)], np.int32)
    grid_spec = pltpu.PrefetchScalarGridSpec(
        num_scalar_prefetch=2,
        grid=(MLA_HEADS // 2, len(qi)),
        in_specs=[
            pl.BlockSpec((2, t, HEAD_BLOCK), lambda hp, p, qi, ki: (hp, qi[p], 0)),
            pl.BlockSpec((2, t, HEAD_BLOCK), lambda hp, p, qi, ki: (hp, ki[p], 0)),
            pl.BlockSpec((2 * MLA_V, t), lambda hp, p, qi, ki: (hp, ki[p])),
        ],
        out_specs=pl.BlockSpec((t, 2 * MLA_V), lambda hp, p, qi, ki: (qi[p], hp)),
        scratch_shapes=[pltpu.VMEM((2, 1, t), F32), pltpu.VMEM((2, 1, t), F32),
                        pltpu.VMEM((2 * MLA_V, t), F32)],
    )
    return pl.pallas_call(
        _attn_kernel,
        grid_spec=grid_spec,
        out_shape=jax.ShapeDtypeStruct((rows, MLA_HEADS * MLA_V), BF16),
        compiler_params=_cparams("parallel", "arbitrary"),
        name="mla_prompt_attention",
    )(jnp.asarray(qi), jnp.asarray(ki), q, k, vt)


def _paged_kernel(layer, n_pp, n_steps, n_total, pt_ref, qlat_ref, qpe_ref, cnew_ref, knew_ref, ckv_hbm, kpe_hbm,
                  o_ref, m_ref, l_ref, acc_ref, ckv_buf, kpe_buf, ckv_all_ref, kpe_all_ref, sem):
    j = pl.program_id(1)
    step = pl.program_id(0) * n_steps + j
    slot = lax.rem(step, PAGE_SLOTS)
    ahead = lax.rem(step + PAGE_AHEAD, n_total)
    ahead_slot = lax.rem(step + PAGE_AHEAD, PAGE_SLOTS)
    t_new = cnew_ref.shape[0]
    rows = MLA_HEADS * t_new

    n_grp = max(1, n_pp // PAGE_GROUP)
    per = n_pp // n_grp

    def page_copies(s, dst_slot, k):
        page = pt_ref[s * n_pp + k]
        return (pltpu.make_async_copy(ckv_hbm.at[layer, page], ckv_buf.at[dst_slot, k], sem.at[0, dst_slot]),
                pltpu.make_async_copy(kpe_hbm.at[layer, page], kpe_buf.at[dst_slot, k], sem.at[1, dst_slot]))

    def start_page(s, dst_slot, k):
        for cp in page_copies(s, dst_slot, k):
            cp.start()

    def wait_step(s, dst_slot):
        for k in range(n_pp):
            for cp in page_copies(s, dst_slot, k):
                cp.wait()

    @pl.when(step == 0)
    def _():
        for s in range(PAGE_AHEAD):
            for k in range(n_pp):
                start_page(s, s, k)

    ql = qlat_ref[...].reshape(rows, MLA_KV_RANK).astype(BF16)
    qp = qpe_ref[...].reshape(rows, MLA_ROPE).astype(BF16)

    @pl.when(j == 0)
    def _():
        cn = cnew_ref[...].astype(BF16)
        s = _bdot_nt(ql, cn) + _bdot_nt(qp, knew_ref[...])
        tok = lax.broadcasted_iota(jnp.int32, (rows, t_new), 0) % t_new
        key = lax.broadcasted_iota(jnp.int32, (rows, t_new), 1)
        s = jnp.where(key <= tok, s, -jnp.inf)
        m = jnp.max(s, axis=-1, keepdims=True)
        pm = jnp.exp(s - m)
        m_ref[...] = jnp.broadcast_to(m, m_ref.shape)
        l_ref[...] = jnp.broadcast_to(jnp.sum(pm, axis=-1, keepdims=True), l_ref.shape)
        acc_ref[...] = jnp.dot(pm.astype(BF16), cn, preferred_element_type=F32)

    wait_step(step, slot)
    gkeys = per * PAGE_SIZE

    def scores(g):
        for k in range(g * per, (g + 1) * per):
            start_page(ahead, ahead_slot, k)
            ckv_all_ref[k * PAGE_SIZE:(k + 1) * PAGE_SIZE, :] = ckv_buf[slot, k].astype(BF16)
            kpe_all_ref[:, k * PAGE_SIZE:(k + 1) * PAGE_SIZE] = kpe_buf[slot, k].astype(BF16)
        cg = ckv_all_ref[g * gkeys:(g + 1) * gkeys, :]
        kg = kpe_all_ref[:, g * gkeys:(g + 1) * gkeys]
        return _bdot_nt(ql, cg) + jnp.dot(qp, kg, preferred_element_type=F32), cg

    m = m_ref[...]
    l = l_ref[...]
    acc = acc_ref[...]
    nxt = scores(0)
    for g in range(n_grp):
        s, cg = nxt
        if g + 1 < n_grp:
            nxt = scores(g + 1)
        m_new = jnp.maximum(m, jnp.max(s, axis=-1, keepdims=True))
        alpha = jnp.exp(m - m_new)
        pm = jnp.exp(s - m_new[:, :1])
        l = alpha * l + jnp.sum(pm, axis=-1, keepdims=True)
        acc = alpha[:, :1] * acc + jnp.dot(pm.astype(BF16), cg, preferred_element_type=F32)
        m = m_new
    m_ref[...] = m
    l_ref[...] = l
    acc_ref[...] = acc

    @pl.when(j == n_steps - 1)
    def _():
        o = acc_ref[...] / l_ref[...][:, :1]
        o_ref[...] = o.reshape(MLA_HEADS, t_new, MLA_KV_RANK)

    @pl.when(step == n_total - 1)
    def _():
        for s in range(PAGE_AHEAD):
            wait_step(s, (n_total + s) % PAGE_SLOTS)


def _paged_attention(layer, qlat, qpe, ckv_new, kpe_new, cache_ckv, cache_kpe_t, page_table, t_new):
    n_b, n_pages = page_table.shape
    n_pp = min(PAGES_PER_STEP, n_pages)
    n_steps = n_pages // n_pp
    assert n_b * n_steps >= PAGE_AHEAD, "the page prefetch chain needs at least PAGE_AHEAD grid steps"
    rows = MLA_HEADS * t_new
    grid_spec = pltpu.PrefetchScalarGridSpec(
        num_scalar_prefetch=1,
        grid=(n_b, n_steps),
        in_specs=[
            pl.BlockSpec((MLA_HEADS, t_new, MLA_KV_RANK), lambda b, j, pt: (0, b, 0)),
            pl.BlockSpec((MLA_HEADS, t_new, MLA_ROPE), lambda b, j, pt: (0, b, 0)),
            pl.BlockSpec((t_new, MLA_KV_RANK), lambda b, j, pt: (b, 0)),
            pl.BlockSpec((t_new, MLA_ROPE), lambda b, j, pt: (b, 0)),
            pl.BlockSpec(memory_space=pl.ANY),
            pl.BlockSpec(memory_space=pl.ANY),
        ],
        out_specs=pl.BlockSpec((MLA_HEADS, t_new, MLA_KV_RANK), lambda b, j, pt: (0, b, 0)),
        scratch_shapes=[pltpu.VMEM((rows, HEAD_BLOCK), F32), pltpu.VMEM((rows, HEAD_BLOCK), F32),
                        pltpu.VMEM((rows, MLA_KV_RANK), F32),
                        pltpu.VMEM((PAGE_SLOTS, n_pp, PAGE_SIZE, MLA_KV_RANK), F32),
                        pltpu.VMEM((PAGE_SLOTS, n_pp, MLA_ROPE, PAGE_SIZE), F32),
                        pltpu.VMEM((n_pp * PAGE_SIZE, MLA_KV_RANK), BF16),
                        pltpu.VMEM((MLA_ROPE, n_pp * PAGE_SIZE), BF16),
                        pltpu.SemaphoreType.DMA((2, PAGE_SLOTS))],
    )
    return pl.pallas_call(
        functools.partial(_paged_kernel, layer, n_pp, n_steps, n_b * n_steps),
        grid_spec=grid_spec,
        out_shape=jax.ShapeDtypeStruct((MLA_HEADS, n_b * t_new, MLA_KV_RANK), F32),
        compiler_params=_cparams("arbitrary", "arbitrary"),
        name="mla_paged_attention",
    )(page_table.reshape(-1), qlat, qpe, ckv_new, kpe_new, cache_ckv, cache_kpe_t)


def _gdn_gates(misc, alog, dtb):
    beta = jax.nn.sigmoid(misc)
    g = -jnp.exp(alog) * _softplus(misc + dtb)
    return beta, g


def _l2n(x, scale):
    return x * (lax.rsqrt(jnp.sum(x * x, axis=-1, keepdims=True) + 1e-6) * scale)


def _gdn_prompt_kernel(qkv_ref, z_ref, misc_ref, convw_ref, alog_ref, dtb_ref, gnorm_ref,
                       o_ref, sfin_ref, s_ref, carry_ref):
    i = pl.program_id(0)

    @pl.when(i == 0)
    def _():
        s_ref[...] = jnp.zeros_like(s_ref)
        carry_ref[...] = jnp.zeros_like(carry_ref)

    x = qkv_ref[...]
    tm = x.shape[0]
    w = convw_ref[...]
    row8 = lax.broadcasted_iota(jnp.int32, (8, GDN_QKV), 0)
    cprev = carry_ref[...]
    acc = x * w[GDN_CONV - 1:GDN_CONV]
    for d in range(1, GDN_CONV):
        xr = pltpu.roll(x, d, 0)
        head = jnp.where(row8 < d, pltpu.roll(cprev, d, 0), xr[0:8])
        xs = jnp.concatenate([head, xr[8:]], axis=0)
        acc = acc + xs * w[GDN_CONV - 1 - d:GDN_CONV - d]
    carry_ref[...] = x[tm - 8:tm]
    conv = _silu(acc)

    nq = GDN_HEADS * GDN_DK
    beta_all, g_all = _gdn_gates(misc_ref[...], alog_ref[...], dtb_ref[...])
    z = z_ref[...]
    gnorm = gnorm_ref[...]
    c = GDN_CHUNK
    blk = 2 * c
    ri = lax.broadcasted_iota(jnp.int32, (blk, blk), 0)
    ci = lax.broadcasted_iota(jnp.int32, (blk, blk), 1)
    tri2 = jnp.where((ri >= ci) & ((ri // c) == (ci // c)), 1.0, 0.0).astype(BF16)
    ns = GDN_HEADS * c
    rs_ = lax.broadcasted_iota(jnp.int32, (ns, ns), 0)
    cs_ = lax.broadcasted_iota(jnp.int32, (ns, ns), 1)
    same = (rs_ // c) == (cs_ // c)
    incl = same & (rs_ >= cs_)
    strict = same & (rs_ > cs_)
    heads = range(GDN_HEADS)
    stack = lambda parts: jnp.concatenate(parts, axis=0)

    solved = []
    for b2 in range(tm // blk):
        gblk = g_all[b2 * blk:(b2 + 1) * blk]
        g0, g1, g2 = _split3(gblk)
        d = functools.partial(jnp.dot, preferred_element_type=F32)
        gcum = d(tri2, g0) + d(tri2, g1) + d(tri2, g2)
        gcum_t = gcum.T
        for c2 in range(2):
            r0 = b2 * blk + c2 * c
            cr = slice(c2 * c, (c2 + 1) * c)
            qs = [_l2n(conv[r0:r0 + c, h * GDN_DK:(h + 1) * GDN_DK], GDN_DK ** -0.5) for h in heads]
            ks = [_l2n(conv[r0:r0 + c, nq + h * GDN_DK:nq + (h + 1) * GDN_DK], 1.0) for h in heads]
            vs = [conv[r0:r0 + c, 2 * nq + h * GDN_DV:2 * nq + (h + 1) * GDN_DV] for h in heads]
            gcs = [gcum[cr, GDN_HEADS + h:GDN_HEADS + h + 1] for h in heads]
            q_st, k_st, v_st, gc_st = stack(qs), stack(ks), stack(vs), stack(gcs)
            gr_st = jnp.concatenate([gcum_t[GDN_HEADS + h:GDN_HEADS + h + 1, cr] for h in heads], axis=1)
            beta_st = stack([beta_all[r0:r0 + c, h:h + 1] for h in heads])
            decay = jnp.exp(jnp.where(incl, gc_st - gr_st, -jnp.inf))
            eg_st = jnp.exp(gc_st)
            kk = _bdot_nt(k_st, k_st)
            mk = -jnp.where(strict, beta_st * kk * decay, 0.0)
            rhs = jnp.concatenate([v_st * beta_st, k_st * (beta_st * eg_st)], axis=-1)
            qk = _bdot_nt(q_st, k_st) * decay
            solved.append([r0, rhs, qk, qs, ks, gcs, eg_st, mk])

    n_sq = int(math.log2(c))
    for kq in range(n_sq):
        for item in solved:
            item[1] = item[1] + _dot_split(item[7], item[1])
            if kq < n_sq - 1:
                item[7] = _dot_split(item[7], item[7])

    state = [s_ref[h] for h in heads]
    for r0, rhs, qk, qs, ks, gcs, eg_st, _ in solved:
        outs = []
        for h in heads:
            hr = slice(h * c, (h + 1) * c)
            uu = rhs[hr, :GDN_DV]
            ww = rhs[hr, GDN_DV:]
            gc = gcs[h]
            glast = gc[c - 1:c, :]
            s_h = state[h]
            v_new = uu - _bdot(ww, s_h)
            o_h = _bdot(qs[h] * eg_st[hr], s_h) + _bdot(qk[hr, h * c:(h + 1) * c], v_new)
            kdec = ks[h] * jnp.exp(glast - gc)
            state[h] = s_h * jnp.exp(glast) + lax.dot_general(
                kdec.astype(BF16), v_new.astype(BF16), (((0,), (0,)), ((), ())),
                preferred_element_type=F32)
            zh = z[r0:r0 + c, h * GDN_DV:(h + 1) * GDN_DV]
            outs.append(_rms(o_h, gnorm) * _silu(zh))
        o_ref[r0:r0 + c, :] = jnp.concatenate(outs, axis=-1).astype(o_ref.dtype)
    for h in heads:
        s_ref[h] = state[h]

    @pl.when(i == pl.num_programs(0) - 1)
    def _():
        sfin_ref[...] = s_ref[...]


def _gdn_prompt(qkv, z, misc, w):
    rows = qkv.shape[0]
    tm = min(256, rows)
    row = lambda n: pl.BlockSpec((tm, n), lambda i: (i, 0))
    full = lambda a: pl.BlockSpec(a.shape, lambda i: (0,) * a.ndim)
    st = (GDN_HEADS, GDN_DK, GDN_DV)
    ins = [qkv, z, misc, w["gdn_conv_w"], w["gdn_alog_pad"], w["gdn_dtb_pad"], w["gdn_g_norm"]]
    return pl.pallas_call(
        _gdn_prompt_kernel,
        grid=(rows // tm,),
        in_specs=[row(GDN_QKV), row(GDN_HEADS * GDN_DV), row(HEAD_BLOCK)] + [full(a) for a in ins[3:]],
        out_specs=[row(GDN_HEADS * GDN_DV), pl.BlockSpec(st, lambda i: (0, 0, 0))],
        out_shape=[jax.ShapeDtypeStruct((rows, GDN_HEADS * GDN_DV), BF16), jax.ShapeDtypeStruct(st, F32)],
        scratch_shapes=[pltpu.VMEM(st, F32), pltpu.VMEM((8, GDN_QKV), F32)],
        compiler_params=_cparams("arbitrary"),
        name="gdn_prompt",
    )(*ins)


def _gdn_sample_kernel(t_new, qkv_ref, st_ref, z_ref, misc_ref, convw_ref, alog_ref, dtb_ref, gcol_ref,
                       s0_ref, o_ref, sout_ref,
                       s_ref, rows_ref, zo_ref, gate_ref, qt_ref, kt_ref, vt_ref, gt_ref, ot_ref):
    n_rows = qkv_ref.shape[0]
    n_b = n_rows // t_new
    nq = GDN_HEADS * GDN_DK
    x = qkv_ref[...]
    st = st_ref[...]
    w = convw_ref[...]
    tpos = lax.broadcasted_iota(jnp.int32, (n_rows, GDN_QKV), 0) % t_new
    acc = x * w[GDN_CONV - 1:GDN_CONV]
    for d in range(1, GDN_CONV):
        xr = pltpu.roll(x, d, 0)
        back = GDN_CONV - 1 - d
        sr = st if back == 0 else pltpu.roll(st, n_rows - back, 0)
        acc = acc + jnp.where(tpos < d, sr, xr) * w[GDN_CONV - 1 - d:GDN_CONV - d]
    conv = _silu(acc)
    parts = []
    for h in range(GDN_HEADS):
        parts.append(_l2n(conv[:, h * GDN_DK:(h + 1) * GDN_DK], GDN_DK ** -0.5))
    for h in range(GDN_HEADS):
        parts.append(_l2n(conv[:, nq + h * GDN_DK:nq + (h + 1) * GDN_DK], 1.0))
    parts.append(conv[:, 2 * nq:])
    feats = jnp.concatenate(parts, axis=-1)
    n_chunk = GDN_QKV // HEAD_BLOCK
    per_part = nq // HEAD_BLOCK
    for cc in range(n_chunk):
        rows_ref[cc] = feats[:, cc * HEAD_BLOCK:(cc + 1) * HEAD_BLOCK]
    zsil = _silu(z_ref[...])
    for cc in range(per_part):
        zo_ref[cc] = zsil[:, cc * HEAD_BLOCK:(cc + 1) * HEAD_BLOCK]
    beta_all, g_all = _gdn_gates(misc_ref[...], alog_ref[...], dtb_ref[...])
    lane = lax.broadcasted_iota(jnp.int32, beta_all.shape, 1)
    gate_ref[...] = jnp.where(lane < GDN_HEADS, beta_all, jnp.exp(g_all))

    for t in range(t_new):
        for cc in range(n_chunk):
            blk_t = rows_ref[cc, pl.ds(t, n_b, stride=t_new), :].T
            dst = (qt_ref, kt_ref, vt_ref)[cc // per_part]
            lo = (cc % per_part) * HEAD_BLOCK
            dst[t, lo:lo + HEAD_BLOCK, :] = blk_t
        gt_ref[t] = gate_ref[pl.ds(t, n_b, stride=t_new), :].T

    n_blk = (GDN_HEADS * GDN_DK * GDN_DV) // HEAD_BLOCK
    per = HEAD_BLOCK // GDN_DV
    for cb in range(n_blk):
        s_ref[per * cb:per * (cb + 1)] = s0_ref[:, cb * HEAD_BLOCK:(cb + 1) * HEAD_BLOCK].T.reshape(
            per, GDN_DV, n_b)

    gcol = gcol_ref[...]
    for t in range(t_new):
        for h in range(GDN_HEADS):
            egr = gt_ref[t, GDN_HEADS + h:GDN_HEADS + h + 1, :]
            betar = gt_ref[t, h:h + 1, :]
            base = h * GDN_DK

            def p1(dk, racc):
                kb = kt_ref[t, pl.ds(base + dk, 1), :]
                return racc + s_ref[base + dk] * kb

            rr = lax.fori_loop(0, GDN_DK, p1, jnp.zeros((GDN_DV, n_b), F32), unroll=8) * egr
            dd = betar * (vt_ref[t, base:base + GDN_DV, :] - rr)

            def p2(dk, oacc):
                kb = kt_ref[t, pl.ds(base + dk, 1), :]
                qb = qt_ref[t, pl.ds(base + dk, 1), :]
                sn = s_ref[base + dk] * egr + kb * dd
                s_ref[base + dk] = sn
                return oacc + sn * qb

            oo = lax.fori_loop(0, GDN_DK, p2, jnp.zeros((GDN_DV, n_b), F32), unroll=8)
            on = oo * lax.rsqrt(jnp.mean(oo * oo, axis=0, keepdims=True) + RMS_EPS) * gcol
            ot_ref[base:base + GDN_DV, :] = on
        on_rows = ot_ref[...].T
        for cc in range(per_part):
            zt = zo_ref[cc, pl.ds(t, n_b, stride=t_new), :]
            zo_ref[cc, pl.ds(t, n_b, stride=t_new), :] = on_rows[:, cc * HEAD_BLOCK:(cc + 1) * HEAD_BLOCK] * zt
    for cc in range(per_part):
        o_ref[:, cc * HEAD_BLOCK:(cc + 1) * HEAD_BLOCK] = zo_ref[cc]

    for cb in range(n_blk):
        sout_ref[:, cb * HEAD_BLOCK:(cb + 1) * HEAD_BLOCK] = s_ref[per * cb:per * (cb + 1)].reshape(
            HEAD_BLOCK, n_b).T


def _gdn_sample(qkv, st_rows, z, misc, w, s0, t_new):
    rows = qkv.shape[0]
    n_b = rows // t_new
    feat = GDN_HEADS * GDN_DK
    n_state = GDN_HEADS * GDN_DK * GDN_DV
    ins = [qkv, st_rows, z, misc, w["gdn_conv_w"], w["gdn_alog_pad"], w["gdn_dtb_pad"], w["gdn_g_col"], s0]
    return pl.pallas_call(
        functools.partial(_gdn_sample_kernel, t_new),
        out_shape=[jax.ShapeDtypeStruct((rows, GDN_HEADS * GDN_DV), F32),
                   jax.ShapeDtypeStruct((n_b, n_state), F32)],
        scratch_shapes=[
            pltpu.VMEM((GDN_HEADS * GDN_DK, GDN_DV, n_b), F32),
            pltpu.VMEM((GDN_QKV // HEAD_BLOCK, rows, HEAD_BLOCK), F32),
            pltpu.VMEM((GDN_HEADS * GDN_DV // HEAD_BLOCK, rows, HEAD_BLOCK), F32),
            pltpu.VMEM((rows, HEAD_BLOCK), F32),
            pltpu.VMEM((t_new, feat, n_b), F32),
            pltpu.VMEM((t_new, feat, n_b), F32),
            pltpu.VMEM((t_new, GDN_HEADS * GDN_DV, n_b), F32),
            pltpu.VMEM((t_new, HEAD_BLOCK, n_b), F32),
            pltpu.VMEM((GDN_HEADS * GDN_DV, n_b), F32),
        ],
        compiler_params=pltpu.CompilerParams(vmem_limit_bytes=VMEM_LIMIT_BYTES),
        name="gdn_sample",
    )(*ins)


def _cmul(ar, ai, br, bi):
    return ar * br - ai * bi, ar * bi + ai * br


def _s5_kernel(per_group_state, u_ref, bbr_ref, bbi_ref, ar_ref, ai_ref, cr_ref, ci_ref, d_ref,
               wglu_ref, bglu_ref, h0r_ref, h0i_ref, o_ref, hr_out_ref, hi_out_ref,
               br_ref, bi_ref, cr_carry_ref, ci_carry_ref):
    i = pl.program_id(0)
    tm = u_ref.shape[0]
    n_grp = tm // 8

    if not per_group_state:
        @pl.when(i == 0)
        def _():
            cr_carry_ref[...] = jnp.zeros_like(cr_carry_ref)
            ci_carry_ref[...] = jnp.zeros_like(ci_carry_ref)

    u = u_ref[...]
    ub = u.astype(BF16)
    br_ref[...] = jnp.dot(ub, bbr_ref[...], preferred_element_type=F32)
    bi_ref[...] = jnp.dot(ub, bbi_ref[...], preferred_element_type=F32)

    ar = ar_ref[...]
    ai = ai_ref[...]
    p1 = (ar, ai)
    p2 = _cmul(*p1, *p1)
    p3 = _cmul(*p2, *p1)
    p4 = _cmul(*p2, *p2)
    p5 = _cmul(*p4, *p1)
    p6 = _cmul(*p4, *p2)
    p7 = _cmul(*p4, *p3)
    p8 = _cmul(*p4, *p4)
    row8 = lax.broadcasted_iota(jnp.int32, (8, S5_LANES), 0)
    pw_r = jnp.zeros((8, S5_LANES), F32)
    pw_i = jnp.zeros((8, S5_LANES), F32)
    for t, pw in enumerate((p1, p2, p3, p4, p5, p6, p7, p8)):
        pw_r = jnp.where(row8 == t, pw[0], pw_r)
        pw_i = jnp.where(row8 == t, pw[1], pw_i)

    def body(gi, carry):
        r0 = pl.multiple_of(gi * 8, 8)
        a = br_ref[pl.ds(r0, 8), :]
        b = bi_ref[pl.ds(r0, 8), :]
        for d, pw in ((1, p1), (2, p2), (4, p4)):
            a_s = jnp.where(row8 >= d, pltpu.roll(a, d, 0), 0.0)
            b_s = jnp.where(row8 >= d, pltpu.roll(b, d, 0), 0.0)
            da, db = _cmul(pw[0], pw[1], a_s, b_s)
            a = a + da
            b = b + db
        if per_group_state:
            c_r = h0r_ref[pl.ds(gi, 1), :]
            c_i = h0i_ref[pl.ds(gi, 1), :]
        else:
            c_r, c_i = carry
        da, db = _cmul(pw_r, pw_i, c_r, c_i)
        a = a + da
        b = b + db
        br_ref[pl.ds(r0, 8), :] = a
        bi_ref[pl.ds(r0, 8), :] = b
        if per_group_state:
            hr_out_ref[pl.ds(gi, 1), :] = a[7:8]
            hi_out_ref[pl.ds(gi, 1), :] = b[7:8]
            return carry
        return a[7:8], b[7:8]

    if per_group_state:
        zero = jnp.zeros((1, S5_LANES), F32)
        lax.fori_loop(0, n_grp, body, (zero, zero))
    else:
        c_fin = lax.fori_loop(0, n_grp, body, (cr_carry_ref[...], ci_carry_ref[...]))
        cr_carry_ref[...] = c_fin[0]
        ci_carry_ref[...] = c_fin[1]
        hr_out_ref[...] = c_fin[0]
        hi_out_ref[...] = c_fin[1]

    y = (jnp.dot(br_ref[...].astype(BF16), cr_ref[...], preferred_element_type=F32)
         - jnp.dot(bi_ref[...].astype(BF16), ci_ref[...], preferred_element_type=F32)
         + d_ref[...] * u)
    zg = jax.nn.gelu(y)
    gate = jax.nn.sigmoid(jnp.dot(zg.astype(BF16), wglu_ref[...], preferred_element_type=F32) + bglu_ref[...])
    o_ref[...] = (zg * gate).astype(o_ref.dtype)


def _s5(su, w, h0r, h0i, per_group_state, out_dtype):
    rows = su.shape[0]
    tm = min(256, rows)
    n_grp = tm // 8
    row = lambda n: pl.BlockSpec((tm, n), lambda i: (i, 0))
    full = lambda a: pl.BlockSpec(a.shape, lambda i: (0,) * a.ndim)
    wlist = [w["s5_bb_re"], w["s5_bb_im"], w["s5_ab_re"], w["s5_ab_im"], w["s5_c_re"], w["s5_c_im"],
             w["s5_d"], w["s5_w_glu"], w["s5_b_glu"]]
    if per_group_state:
        st_spec = pl.BlockSpec((n_grp, S5_LANES), lambda i: (i, 0))
        st_shape = jax.ShapeDtypeStruct((rows // 8, S5_LANES), F32)
    else:
        st_spec = pl.BlockSpec((1, S5_LANES), lambda i: (0, 0))
        st_shape = jax.ShapeDtypeStruct((1, S5_LANES), F32)
    return pl.pallas_call(
        functools.partial(_s5_kernel, per_group_state),
        grid=(rows // tm,),
        in_specs=[row(S5_WIDTH)] + [full(a) for a in wlist] + [st_spec, st_spec],
        out_specs=[row(S5_WIDTH), st_spec, st_spec],
        out_shape=[jax.ShapeDtypeStruct((rows, S5_WIDTH), out_dtype), st_shape, st_shape],
        scratch_shapes=[pltpu.VMEM((tm, S5_LANES), F32), pltpu.VMEM((tm, S5_LANES), F32),
                        pltpu.VMEM((1, S5_LANES), F32), pltpu.VMEM((1, S5_LANES), F32)],
        compiler_params=_cparams("arbitrary"),
        name="s5_sample" if per_group_state else "s5_prompt",
    )(su, *wlist, h0r, h0i)


def _mixout_kernel(x_ref, oa_ref, og_ref, os_ref, wuv_ref, woa_ref, wob_ref, woc_ref,
                   gpost_ref, gxa_ref, wq_ref, x2_ref, q_ref):
    mixed = None
    for h in range(MLA_HEADS):
        o_h = jnp.dot(oa_ref[h].astype(BF16), wuv_ref[h], preferred_element_type=F32)
        t = jnp.dot(o_h.astype(BF16), woa_ref[h * MLA_V:(h + 1) * MLA_V, :], preferred_element_type=F32)
        mixed = t if mixed is None else mixed + t
    mixed = mixed + jnp.dot(og_ref[...].astype(BF16), wob_ref[...], preferred_element_type=F32)
    mixed = mixed + jnp.dot(os_ref[...].astype(BF16), woc_ref[...], preferred_element_type=F32)
    x2 = x_ref[...] + _rms(mixed, gpost_ref[...])
    x2_ref[...] = x2
    hq = _rms(x2, gxa_ref[...]).astype(BF16)
    q_ref[...] = jnp.dot(hq, wq_ref[...], preferred_element_type=F32) * XA_SCALE


def _mixout(x, o_lat, o_gdn, o_s5, w):
    rows, d = x.shape
    tm = min(512, rows)
    row = lambda n: pl.BlockSpec((tm, n), lambda i: (i, 0))
    full = lambda a: pl.BlockSpec(a.shape, lambda i: (0,) * a.ndim)
    oa_spec = pl.BlockSpec((MLA_HEADS, tm, MLA_KV_RANK), lambda i: (0, i, 0))
    wl = [w["wuv_h"], w["wo_a"], w["wo_b"], w["wo_c"], w["mix_g_post"], w["xa_g_pre"], w["xa_w_q"]]
    return pl.pallas_call(
        _mixout_kernel,
        grid=(rows // tm,),
        in_specs=[row(d), oa_spec, row(o_gdn.shape[1]), row(o_s5.shape[1])] + [full(a) for a in wl],
        out_specs=[row(d), row(d)],
        out_shape=[jax.ShapeDtypeStruct((rows, d), F32), jax.ShapeDtypeStruct((rows, d), F32)],
        compiler_params=_cparams("parallel"),
        name="mixout_sample",
    )(x, o_lat, o_gdn, o_s5, *wl)


def _postmix_kernel(x_ref, oa_ref, og_ref, os_ref, woa_ref, wob_ref, woc_ref, gpost_ref, gxa_ref, wq_ref,
                    mk_ref, mv_ref, wo_ref, gxo_ref, y_ref):
    mixed = jnp.dot(oa_ref[...], woa_ref[...], preferred_element_type=F32)
    mixed = mixed + jnp.dot(og_ref[...], wob_ref[...], preferred_element_type=F32)
    mixed = mixed + jnp.dot(os_ref[...], woc_ref[...], preferred_element_type=F32)
    x2 = x_ref[...] + _rms(mixed, gpost_ref[...])
    q = jnp.dot(_rms(x2, gxa_ref[...]).astype(BF16), wq_ref[...], preferred_element_type=F32) * XA_SCALE
    outs = []
    for h in range(XA_HEADS):
        hs = slice(h * XA_HEAD_DIM, (h + 1) * XA_HEAD_DIM)
        s = _bdot_nt(q[:, hs], mk_ref[:, hs])
        m = jnp.max(s, axis=-1, keepdims=True)
        pm = jnp.exp(s - m)
        pr = pm / jnp.sum(pm, axis=-1, keepdims=True)
        outs.append(_bdot(pr, mv_ref[:, hs]).astype(BF16))
    xa = jnp.dot(jnp.concatenate(outs, axis=-1), wo_ref[...], preferred_element_type=F32)
    y_ref[...] = x2 + _rms(xa, gxo_ref[...])


def _postmix(x, o_mla, o_gdn, o_s5, w, mem_k, mem_v, layer):
    rows, d = x.shape
    n_mem = mem_k.shape[2]
    tm = min(512, rows)
    row = lambda n: pl.BlockSpec((tm, n), lambda i: (i, 0))
    full = lambda a: pl.BlockSpec(a.shape, lambda i: (0,) * a.ndim)
    mspec = pl.BlockSpec((None, None, n_mem, d), lambda i: (layer, 0, 0, 0))
    w1 = [w["wo_a"], w["wo_b"], w["wo_c"], w["mix_g_post"], w["xa_g_pre"], w["xa_w_q"]]
    w2 = [w["xa_w_o"], w["xa_g_post"]]
    return pl.pallas_call(
        _postmix_kernel,
        grid=(rows // tm,),
        in_specs=([row(d), row(o_mla.shape[1]), row(o_gdn.shape[1]), row(o_s5.shape[1])]
                  + [full(a) for a in w1] + [mspec, mspec] + [full(a) for a in w2]),
        out_specs=row(d),
        out_shape=jax.ShapeDtypeStruct((rows, d), F32),
        compiler_params=_cparams("parallel"),
        name="postmix_prompt",
    )(x, o_mla, o_gdn, o_s5, *w1, mem_k, mem_v, *w2)


def _xattn_heads_merged_kernel(n_b, q_ref, mk_ref, mv_ref, o_ref):
    rows = q_ref.shape[0] // n_b
    n_mem = mk_ref.shape[1]
    for bi in range(n_b):
        q = q_ref[bi * rows:(bi + 1) * rows, :]
        q_all = jnp.concatenate([q[:, h * XA_HEAD_DIM:(h + 1) * XA_HEAD_DIM] for h in range(XA_HEADS)], axis=0)
        k2 = mk_ref[bi].reshape(n_mem * XA_HEADS, XA_HEAD_DIM)
        v2 = mv_ref[bi].reshape(n_mem * XA_HEADS, XA_HEAD_DIM)
        s = _bdot_nt(q_all, k2)
        q_head = lax.broadcasted_iota(jnp.int32, s.shape, 0) // rows
        m_head = lax.broadcasted_iota(jnp.int32, s.shape, 1) % XA_HEADS
        s = jnp.where(q_head == m_head, s, -jnp.inf)
        m = jnp.max(s, axis=-1, keepdims=True)
        pm = jnp.exp(s - m)
        pr = pm / jnp.sum(pm, axis=-1, keepdims=True)
        o_all = _bdot(pr, v2)
        o_ref[bi * rows:(bi + 1) * rows, :] = jnp.concatenate(
            [o_all[h * rows:(h + 1) * rows] for h in range(XA_HEADS)], axis=-1).astype(o_ref.dtype)


def _xattn_sample(q, mem_k, mem_v, layer, t_new):
    rows, d = q.shape
    n_mem = mem_k.shape[2]
    n_b = min(XA_SEQ_PER_STEP, rows // t_new)
    tm = n_b * t_new
    mspec = pl.BlockSpec((None, n_b, n_mem, XA_HEADS, XA_HEAD_DIM), lambda i: (layer, i, 0, 0, 0))
    return pl.pallas_call(
        functools.partial(_xattn_heads_merged_kernel, n_b),
        grid=(rows // tm,),
        in_specs=[pl.BlockSpec((tm, d), lambda i: (i, 0)), mspec, mspec],
        out_specs=pl.BlockSpec((tm, d), lambda i: (i, 0)),
        out_shape=jax.ShapeDtypeStruct((rows, d), BF16),
        compiler_params=_cparams("parallel"),
        name="xattn_sample",
    )(q, mem_k, mem_v)


def _xaout_kernel(x_ref, o_ref, wo_ref, g_ref, y_ref):
    xa = jnp.dot(o_ref[...], wo_ref[...], preferred_element_type=F32)
    y_ref[...] = x_ref[...] + _rms(xa, g_ref[...])


def _xaout(x, o, w):
    rows, d = x.shape
    tm = min(512, rows)
    row = pl.BlockSpec((tm, d), lambda i: (i, 0))
    return pl.pallas_call(
        _xaout_kernel,
        grid=(rows // tm,),
        in_specs=[row, row, pl.BlockSpec((d, d), lambda i: (0, 0)), pl.BlockSpec((1, d), lambda i: (0, 0))],
        out_specs=row,
        out_shape=jax.ShapeDtypeStruct((rows, d), F32),
        compiler_params=_cparams("parallel"),
        name="xattn_out",
    )(x, o, w["xa_w_o"], w["xa_g_post"])


def _memproj_kernel(m_ref, wk_ref, wv_ref, k_ref, v_ref):
    m = m_ref[...].astype(BF16)
    for l in range(wk_ref.shape[0]):
        k_ref[l] = jnp.dot(m, wk_ref[l], preferred_element_type=F32)
        v_ref[l] = jnp.dot(m, wv_ref[l], preferred_element_type=F32)


def _memproj(mem, wk, wv):
    depth = wk.shape[0]
    shp = jax.ShapeDtypeStruct((depth,) + mem.shape, F32)
    return pl.pallas_call(
        _memproj_kernel,
        out_shape=[shp, shp],
        compiler_params=pltpu.CompilerParams(vmem_limit_bytes=VMEM_LIMIT_BYTES),
        name="mem_kv_proj",
    )(mem, wk, wv)


def _layer_weights(l, p):
    d_model = p["w_in"].shape[1]
    w = {}
    r1 = lambda a: a[l].reshape(1, -1).astype(F32)
    for name in ("ffn1", "ffn2"):
        w[name + "_g_pre"] = r1(p[name + "_g_pre"])
        w[name + "_g_post"] = r1(p[name + "_g_post"])
    for name in ("mix_g_pre", "mix_g_post", "mla_g_q", "mla_g_kv", "xa_g_pre", "xa_g_post", "gdn_g_norm",
                 "s5_d", "s5_b_glu"):
        w[name] = r1(p[name])
    w["gdn_g_col"] = p["gdn_g_norm"][l].reshape(-1, 1).astype(F32)

    w_in = p["w_in"][l]
    offs = np.cumsum([0, MLA_Q_RANK, MLA_KV_RANK, MLA_ROPE, GDN_QKV, GDN_HEADS * GDN_DV, GDN_HEADS, GDN_HEADS,
                      S5_WIDTH])
    w_cq, w_ckv, w_kpe, w_qkv, w_z, w_b, w_a, w_su = [w_in[:, offs[i]:offs[i + 1]] for i in range(8)]
    half = MLA_ROPE // 2
    zeros = lambda n: jnp.zeros((d_model, n), w_in.dtype)
    w_kpe_sw = jnp.concatenate([-w_kpe[:, half:], w_kpe[:, :half]], axis=1)
    tail = HEAD_BLOCK - MLA_NOPE - MLA_ROPE
    w_kpl = jnp.concatenate([zeros(MLA_NOPE), w_kpe, zeros(tail)], axis=1)
    w_kpls = jnp.concatenate([zeros(MLA_NOPE), w_kpe_sw, zeros(tail)], axis=1)
    w_misc = jnp.concatenate([w_b, w_a, zeros(HEAD_BLOCK - 2 * GDN_HEADS)], axis=1)
    w["w_big"] = jnp.concatenate([w_cq, w_ckv, w_kpl, w_kpls, w_misc, w_qkv, w_z, w_su], axis=1).astype(BF16)

    w_uq = p["mla_w_uq"][l].reshape(MLA_Q_RANK, MLA_HEADS, MLA_NOPE + MLA_ROPE)
    nope, x1, x2 = w_uq[..., :MLA_NOPE], w_uq[..., MLA_NOPE:MLA_NOPE + half], w_uq[..., MLA_NOPE + half:]
    zq = lambda n: jnp.zeros((MLA_Q_RANK, MLA_HEADS, n), w_uq.dtype)
    w["wq_a"] = jnp.concatenate([nope, x1, x2, zq(tail)], axis=-1).reshape(MLA_Q_RANK, -1).astype(BF16)
    w["wq_b"] = jnp.concatenate([zq(MLA_NOPE), -x2, x1, zq(tail)], axis=-1).reshape(MLA_Q_RANK, -1).astype(BF16)

    w_uk = p["mla_w_uk"][l]
    w_uv = p["mla_w_uv"][l]
    zk = jnp.zeros((MLA_KV_RANK, MLA_HEADS, HEAD_BLOCK - MLA_NOPE), w_uk.dtype)
    w["wuk_pad"] = jnp.concatenate([w_uk, zk], axis=-1).reshape(MLA_KV_RANK, -1).astype(BF16)
    wuk_t = jnp.transpose(w_uk, (1, 2, 0))
    w["wuk_t"] = jnp.concatenate(
        [wuk_t, jnp.zeros((MLA_HEADS, HEAD_BLOCK - MLA_NOPE, MLA_KV_RANK), w_uk.dtype)], axis=1).astype(BF16)
    w["wuv"] = w_uv.reshape(MLA_KV_RANK, -1).T.astype(BF16)
    w["wuv_h"] = jnp.transpose(w_uv, (1, 0, 2)).astype(BF16)

    w_out = p["w_out"][l]
    n_a = MLA_HEADS * MLA_V
    n_b = n_a + GDN_HEADS * GDN_DV
    w["wo_a"] = w_out[:n_a].astype(BF16)
    w["wo_b"] = w_out[n_a:n_b].astype(BF16)
    w["wo_c"] = w_out[n_b:].astype(BF16)
    w["xa_w_q"] = p["xa_w_q"][l].astype(BF16)
    w["xa_w_o"] = p["xa_w_o"][l].astype(BF16)

    w["gdn_conv_w"] = p["gdn_conv_w"][l].astype(F32)
    pad_gate = lambda v: jnp.zeros((1, HEAD_BLOCK), F32).at[0, GDN_HEADS:2 * GDN_HEADS].set(v.astype(F32))
    w["gdn_alog_pad"] = pad_gate(p["gdn_a_log"][l])
    w["gdn_dtb_pad"] = pad_gate(p["gdn_dt_bias"][l])

    a_re, a_im = p["s5_a_re"][l].astype(F32), p["s5_a_im"][l].astype(F32)
    dt = jnp.exp(p["s5_log_dt"][l].astype(F32))[:, None]
    mag = jnp.exp(a_re * dt)
    ab_re, ab_im = mag * jnp.cos(a_im * dt), mag * jnp.sin(a_im * dt)
    den = a_re * a_re + a_im * a_im
    nr, ni = ab_re - 1.0, ab_im
    coef_re = (nr * a_re + ni * a_im) / den
    coef_im = (ni * a_re - nr * a_im) / den
    b_re, b_im = p["s5_b_re"][l].astype(F32), p["s5_b_im"][l].astype(F32)
    bb_re = coef_re[..., None] * b_re - coef_im[..., None] * b_im
    bb_im = coef_re[..., None] * b_im + coef_im[..., None] * b_re
    eye = jnp.eye(S5_GROUPS, dtype=F32)
    bd_in = lambda bb: jnp.einsum("gnp,gh->gphn", bb, eye).reshape(S5_WIDTH, S5_LANES).astype(BF16)
    bd_out = lambda cc: jnp.einsum("gpn,gh->gnhp", cc.astype(F32), eye).reshape(S5_LANES, S5_WIDTH).astype(BF16)
    w["s5_bb_re"], w["s5_bb_im"] = bd_in(bb_re), bd_in(bb_im)
    w["s5_c_re"], w["s5_c_im"] = bd_out(p["s5_c_re"][l]), bd_out(p["s5_c_im"][l])
    w["s5_ab_re"] = ab_re.reshape(1, S5_LANES)
    w["s5_ab_im"] = ab_im.reshape(1, S5_LANES)
    w["s5_w_glu"] = p["s5_w_glu"][l].astype(BF16)
    return w


def _rope_tables(pos):
    half = MLA_ROPE // 2
    inv = ROPE_THETA ** (-jnp.arange(half, dtype=F32) / half)
    ang = pos.astype(F32)[:, None] * inv[None, :]
    c, s = jnp.cos(ang), jnp.sin(ang)
    n = pos.shape[0]
    z0 = jnp.zeros((n, MLA_NOPE), F32)
    z1 = jnp.zeros((n, HEAD_BLOCK - MLA_NOPE - MLA_ROPE), F32)
    return jnp.concatenate([z0, c, c, z1], axis=1), jnp.concatenate([z0, s, s, z1], axis=1)


def kernel(x_prompt, x_sample, mem_prompt, cache_ckv, cache_kpe, page_table, cache_mem_k, cache_mem_v, state_gdn, state_gdn_conv, state_s5_re, state_s5_im, ffn1_g_pre, ffn1_g_post, ffn1_w_gate, ffn1_w_up, ffn1_w_down, mix_g_pre, mix_g_post, w_in, w_out, mla_g_q, mla_w_uq, mla_g_kv, mla_w_uk, mla_w_uv, gdn_conv_w, gdn_a_log, gdn_dt_bias, gdn_g_norm, s5_a_re, s5_a_im, s5_log_dt, s5_b_re, s5_b_im, s5_c_re, s5_c_im, s5_d, s5_w_glu, s5_b_glu, xa_g_pre, xa_g_post, xa_w_q, xa_w_k, xa_w_v, xa_w_o, ffn2_g_pre, ffn2_g_post, ffn2_w_gate, ffn2_w_up, ffn2_w_down):
    params = dict(
        ffn1_g_pre=ffn1_g_pre, ffn1_g_post=ffn1_g_post, ffn1_w_gate=ffn1_w_gate, ffn1_w_up=ffn1_w_up,
        ffn1_w_down=ffn1_w_down, mix_g_pre=mix_g_pre, mix_g_post=mix_g_post, w_in=w_in, w_out=w_out,
        mla_g_q=mla_g_q, mla_w_uq=mla_w_uq, mla_g_kv=mla_g_kv, mla_w_uk=mla_w_uk, mla_w_uv=mla_w_uv,
        gdn_conv_w=gdn_conv_w, gdn_a_log=gdn_a_log, gdn_dt_bias=gdn_dt_bias, gdn_g_norm=gdn_g_norm,
        s5_a_re=s5_a_re, s5_a_im=s5_a_im, s5_log_dt=s5_log_dt, s5_b_re=s5_b_re, s5_b_im=s5_b_im,
        s5_c_re=s5_c_re, s5_c_im=s5_c_im, s5_d=s5_d, s5_w_glu=s5_w_glu, s5_b_glu=s5_b_glu,
        xa_g_pre=xa_g_pre, xa_g_post=xa_g_post, xa_w_q=xa_w_q, xa_w_o=xa_w_o,
        ffn2_g_pre=ffn2_g_pre, ffn2_g_post=ffn2_g_post, ffn2_w_gate=ffn2_w_gate, ffn2_w_up=ffn2_w_up,
        ffn2_w_down=ffn2_w_down)
    depth = w_in.shape[0]
    bsz, seqlen, d_model = x_prompt.shape
    dec_b, dec_t, _ = x_sample.shape
    n_mem = mem_prompt.shape[1]
    past_len = page_table.shape[1] * PAGE_SIZE
    n_s = dec_b * dec_t

    tc_p, ts_p = _rope_tables(jnp.arange(seqlen))
    tc_s, ts_s = _rope_tables(jnp.tile(past_len + jnp.arange(dec_t), dec_b))

    mem_k, mem_v = _memproj(mem_prompt.reshape(n_mem, d_model), xa_w_k.astype(BF16), xa_w_v.astype(BF16))
    mem_k4 = mem_k.reshape(depth, bsz, n_mem, d_model)
    mem_v4 = mem_v.reshape(depth, bsz, n_mem, d_model)
    cache_kpe_t = jnp.swapaxes(cache_kpe, 2, 3)

    yp = x_prompt.reshape(seqlen, d_model)
    ys = x_sample.reshape(n_s, d_model)
    zero_state = jnp.zeros((1, S5_LANES), F32)
    outs = {k: [] for k in ("p_ckv", "p_kpe", "p_gdn", "p_conv", "p_s5r", "p_s5i",
                            "s_ckv", "s_kpe", "s_gdn", "s_conv", "s_s5r", "s_s5i")}
    weights = [_layer_weights(l, params) for l in range(depth)]
    ffn1_w = [a.astype(BF16) for a in (ffn1_w_gate, ffn1_w_up, ffn1_w_down)]
    ffn2_w = [a.astype(BF16) for a in (ffn2_w_gate, ffn2_w_up, ffn2_w_down)]
    for l, w in enumerate(weights):
        ys = _half_ffn(ys, w["ffn1_g_pre"], w["ffn1_g_post"], *ffn1_w, l)
        ckv, kpe, qkv, z, misc, su, qlat, qpe = _mixprep(True, ys, w, tc_s, ts_s)
        o_lat = _paged_attention(l, qlat, qpe, ckv, kpe, cache_ckv, cache_kpe_t, page_table, dec_t)
        st_rows = jnp.pad(state_gdn_conv[l], ((0, 0), (0, dec_t - (GDN_CONV - 1)), (0, 0))).reshape(n_s, GDN_QKV)
        o_gdn, gdn_s = _gdn_sample(qkv, st_rows, z, misc, w, state_gdn[l].reshape(dec_b, -1), dec_t)
        o_s5, h_re, h_im = _s5(su, w, state_s5_re[l].reshape(dec_b, S5_LANES),
                               state_s5_im[l].reshape(dec_b, S5_LANES), True, F32)
        x2, xq = _mixout(ys, o_lat, o_gdn, o_s5, w)
        o_xa = _xattn_sample(xq, cache_mem_k, cache_mem_v, l, dec_t)
        ys = _xaout(x2, o_xa, w)
        ys = _half_ffn(ys, w["ffn2_g_pre"], w["ffn2_g_post"], *ffn2_w, l)
        outs["s_ckv"].append(ckv.reshape(dec_b, dec_t, MLA_KV_RANK))
        outs["s_kpe"].append(kpe.reshape(dec_b, dec_t, MLA_ROPE))
        outs["s_gdn"].append(gdn_s.reshape(dec_b, GDN_HEADS, GDN_DK, GDN_DV))
        outs["s_conv"].append(qkv.reshape(dec_b, dec_t, GDN_QKV)[:, dec_t - (GDN_CONV - 1):])
        outs["s_s5r"].append(h_re.reshape(dec_b, S5_GROUPS, S5_STATE))
        outs["s_s5i"].append(h_im.reshape(dec_b, S5_GROUPS, S5_STATE))

    for l, w in enumerate(weights):
        yp = _half_ffn(yp, w["ffn1_g_pre"], w["ffn1_g_post"], *ffn1_w, l)
        ckv, kpe, qkv, z, misc, su, q, k, v = _mixprep(False, yp, w, tc_p, ts_p)
        o_mla = _prompt_attention(q, k, v)
        o_gdn, gdn_s = _gdn_prompt(qkv, z, misc, w)
        o_s5, h_re, h_im = _s5(su, w, zero_state, zero_state, False, BF16)
        yp = _postmix(yp, o_mla, o_gdn, o_s5, w, mem_k4, mem_v4, l)
        yp = _half_ffn(yp, w["ffn2_g_pre"], w["ffn2_g_post"], *ffn2_w, l)
        outs["p_ckv"].append(ckv.reshape(bsz, seqlen, MLA_KV_RANK))
        outs["p_kpe"].append(kpe.reshape(bsz, seqlen, MLA_ROPE))
        outs["p_gdn"].append(gdn_s.reshape(bsz, GDN_HEADS, GDN_DK, GDN_DV))
        outs["p_conv"].append(qkv[seqlen - (GDN_CONV - 1):].reshape(bsz, GDN_CONV - 1, GDN_QKV))
        outs["p_s5r"].append(h_re.reshape(bsz, S5_GROUPS, S5_STATE))
        outs["p_s5i"].append(h_im.reshape(bsz, S5_GROUPS, S5_STATE))

    st = {k: jnp.stack(v) for k, v in outs.items()}
    p_mem_k = mem_k.reshape(depth, bsz, n_mem, XA_HEADS, XA_HEAD_DIM)
    p_mem_v = mem_v.reshape(depth, bsz, n_mem, XA_HEADS, XA_HEAD_DIM)
    return (yp.reshape(bsz, seqlen, d_model), ys.reshape(dec_b, dec_t, d_model),
            st["p_ckv"], st["p_kpe"], st["p_gdn"], st["p_conv"], st["p_s5r"], st["p_s5i"], p_mem_k, p_mem_v,
            st["s_ckv"], st["s_kpe"], st["s_gdn"], st["s_conv"], st["s_s5r"], st["s_s5i"])
```

```python
import functools
import math

import jax
import jax.numpy as jnp
import numpy as np
from jax import lax
from jax.experimental import pallas as pl
from jax.experimental.pallas import tpu as pltpu

F32 = jnp.float32
BF16 = jnp.bfloat16

RMS_EPS = 1e-6
MLA_HEADS = 8
MLA_NOPE = 64
MLA_ROPE = 32
MLA_V = 64
MLA_Q_RANK = 384
MLA_KV_RANK = 256
ROPE_THETA = 10000.0
HEAD_BLOCK = 128
PAGE_SIZE = 128
FFN_CHUNKS = 4
PAGES_PER_STEP = 64
PAGE_AHEAD = 2
PAGE_SLOTS = PAGE_AHEAD + 1
PAGE_GROUP = 32
ATTN_TILE = 1024
ATTN_SUB = 512
GDN_HEADS = 4
GDN_DK = 64
GDN_DV = 64
GDN_CONV = 4
GDN_CHUNK = 64
GDN_QKV = GDN_HEADS * (2 * GDN_DK + GDN_DV)
S5_GROUPS = 16
S5_GROUP = 16
S5_STATE = 64
S5_WIDTH = S5_GROUPS * S5_GROUP
S5_LANES = S5_GROUPS * S5_STATE
XA_HEADS = 4
XA_HEAD_DIM = 256
XA_SEQ_PER_STEP = 4

VMEM_LIMIT_BYTES = 56 * 1024 * 1024

MLA_SCALE = (MLA_NOPE + MLA_ROPE) ** -0.5
LOG2_E = math.log2(math.e)
XA_SCALE = XA_HEAD_DIM ** -0.5


def _cparams(*sem):
    return pltpu.CompilerParams(dimension_semantics=tuple(sem), vmem_limit_bytes=VMEM_LIMIT_BYTES)


def _rms(x, g):
    return x * lax.rsqrt(jnp.mean(x * x, axis=-1, keepdims=True) + RMS_EPS) * g


def _bdot(a, b):
    return jnp.dot(a.astype(BF16), b.astype(BF16), preferred_element_type=F32)


def _bdot_nt(a, b):
    return lax.dot_general(a.astype(BF16), b.astype(BF16), (((1,), (1,)), ((), ())),
                           preferred_element_type=F32)


def _split3(a):
    hi = a.astype(BF16)
    r1 = a - hi.astype(F32)
    mid = r1.astype(BF16)
    lo = (r1 - mid.astype(F32)).astype(BF16)
    return hi, mid, lo


def _dot_split(a, b):
    a0 = a.astype(BF16)
    a1 = (a - a0.astype(F32)).astype(BF16)
    b0 = b.astype(BF16)
    b1 = (b - b0.astype(F32)).astype(BF16)
    d = functools.partial(jnp.dot, preferred_element_type=F32)
    return d(a0, b0) + (d(a0, b1) + d(a1, b0))


def _silu(x):
    return x * jax.nn.sigmoid(x)


def _softplus(x):
    return jnp.maximum(x, 0.0) + jnp.log(1.0 + jnp.exp(-jnp.abs(x)))


def _lane_chunks(n, parts):
    tiles = n // HEAD_BLOCK
    cuts = [round(tiles * i / parts) * HEAD_BLOCK for i in range(parts + 1)]
    return [(a, b) for a, b in zip(cuts[:-1], cuts[1:]) if b > a]


def _ffn_kernel(x_ref, gpre_ref, gpost_ref, wg_ref, wu_ref, wd_ref, o_ref):
    tm = x_ref.shape[0]
    half = tm // 2
    halves = range(2)
    xs = [x_ref[h * half:(h + 1) * half, :] for h in halves]
    xn = [_rms(x, gpre_ref[...]).astype(BF16) for x in xs]
    accs = [None, None]
    for a, b in _lane_chunks(wg_ref.shape[1], FFN_CHUNKS):
        for h in halves:
            g = jnp.dot(xn[h], wg_ref[:, a:b], preferred_element_type=F32)
            u = jnp.dot(xn[h], wu_ref[:, a:b], preferred_element_type=F32)
            t = jnp.dot((_silu(g) * u).astype(BF16), wd_ref[a:b, :], preferred_element_type=F32)
            accs[h] = t if accs[h] is None else accs[h] + t
    for h in halves:
        o_ref[h * half:(h + 1) * half, :] = xs[h] + 0.5 * _rms(accs[h], gpost_ref[...])


def _half_ffn(x, g_pre, g_post, wg, wu, wd, layer):
    rows, d = x.shape
    f = wg.shape[2]
    tm = min(512, rows)
    wspec = lambda shape: pl.BlockSpec((None,) + shape, lambda i: (layer, 0, 0), pipeline_mode=pl.Buffered(1))
    return pl.pallas_call(
        _ffn_kernel,
        grid=(rows // tm,),
        in_specs=[
            pl.BlockSpec((tm, d), lambda i: (i, 0)),
            pl.BlockSpec((1, d), lambda i: (0, 0)),
            pl.BlockSpec((1, d), lambda i: (0, 0)),
            wspec((d, f)), wspec((d, f)), wspec((f, d)),
        ],
        out_specs=pl.BlockSpec((tm, d), lambda i: (i, 0)),
        out_shape=jax.ShapeDtypeStruct((rows, d), F32),
        compiler_params=_cparams("parallel"),
        name="half_ffn",
    )(x, g_pre, g_post, wg, wu, wd)


C_CQ = (0, 384)
C_CKV = (384, 640)
C_KPL = (640, 768)
C_KPLS = (768, 896)
C_MISC = (896, 1024)
C_QKV = (1024, 1792)
C_Z = (1792, 2048)
C_SU = (2048, 2304)
W_BIG = 2304


def _mixprep_body(sample, x_ref, g_ref, wbig_ref, gq_ref, wqa_ref, wqb_ref, gkv_ref, tc_ref, ts_ref,
                  wk_ref, wv_ref, ckv_ref, kpe_ref, qkv_ref, z_ref, misc_ref, su_ref, a_ref, b_ref, c_ref):
    u = _rms(x_ref[...], g_ref[...]).astype(BF16)
    y = jnp.dot(u, wbig_ref[...], preferred_element_type=F32)
    sl = lambda c: y[:, c[0]:c[1]]
    tc = tc_ref[...]
    ts = ts_ref[...]
    kpe_pl = sl(C_KPL) * tc + sl(C_KPLS) * ts
    kpe_ref[...] = kpe_pl[:, MLA_NOPE:MLA_NOPE + MLA_ROPE]
    ckv_n = _rms(sl(C_CKV), gkv_ref[...])
    ckv_ref[...] = ckv_n
    qkv_ref[...] = sl(C_QKV)
    z_ref[...] = sl(C_Z)
    misc_ref[...] = sl(C_MISC)
    su_ref[...] = sl(C_SU)

    cqn = _rms(sl(C_CQ), gq_ref[...]).astype(BF16)
    qa = jnp.dot(cqn, wqa_ref[...], preferred_element_type=F32)
    qb = jnp.dot(cqn, wqb_ref[...], preferred_element_type=F32)
    lane = lax.broadcasted_iota(jnp.int32, (1, HEAD_BLOCK), 1)
    qmul = jnp.where(lane < MLA_NOPE, 1.0, 0.0) + tc
    ckb = ckv_n.astype(BF16)
    if sample:
        qlat_ref, qpe_ref = a_ref, b_ref
        for h in range(MLA_HEADS):
            hs = slice(h * HEAD_BLOCK, (h + 1) * HEAD_BLOCK)
            qh = qa[:, hs] * qmul + qb[:, hs] * ts
            qlat_ref[h] = jnp.dot(qa[:, hs].astype(BF16), wk_ref[h], preferred_element_type=F32) * MLA_SCALE
            qpe_ref[h] = qh[:, MLA_NOPE:MLA_NOPE + MLA_ROPE] * MLA_SCALE
    else:
        q_ref, k_ref, v_ref = a_ref, b_ref, c_ref
        kn = jnp.dot(ckb, wk_ref[...], preferred_element_type=F32)
        for h in range(MLA_HEADS):
            hs = slice(h * HEAD_BLOCK, (h + 1) * HEAD_BLOCK)
            qh = qa[:, hs] * qmul + qb[:, hs] * ts
            q_ref[h] = (qh * (MLA_SCALE * LOG2_E)).astype(BF16)
            k_ref[h] = (kn[:, hs] + kpe_pl).astype(BF16)
        v_ref[...] = _bdot_nt(wv_ref[...], ckb).astype(BF16)


def _mixprep_prompt_kernel(x_ref, g_ref, wbig_ref, gq_ref, wqa_ref, wqb_ref, gkv_ref, tc_ref, ts_ref,
                           wk_ref, wv_ref, ckv_ref, kpe_ref, qkv_ref, z_ref, misc_ref, su_ref,
                           q_ref, k_ref, v_ref):
    _mixprep_body(False, x_ref, g_ref, wbig_ref, gq_ref, wqa_ref, wqb_ref, gkv_ref, tc_ref, ts_ref,
                  wk_ref, wv_ref, ckv_ref, kpe_ref, qkv_ref, z_ref, misc_ref, su_ref, q_ref, k_ref, v_ref)


def _mixprep_sample_kernel(x_ref, g_ref, wbig_ref, gq_ref, wqa_ref, wqb_ref, gkv_ref, tc_ref, ts_ref,
                           wk_ref, ckv_ref, kpe_ref, qkv_ref, z_ref, misc_ref, su_ref, qlat_ref, qpe_ref):
    _mixprep_body(True, x_ref, g_ref, wbig_ref, gq_ref, wqa_ref, wqb_ref, gkv_ref, tc_ref, ts_ref,
                  wk_ref, None, ckv_ref, kpe_ref, qkv_ref, z_ref, misc_ref, su_ref, qlat_ref, qpe_ref, None)


def _mixprep(sample, x, w, tc, ts):
    rows, d = x.shape
    tm = min(512, rows)
    row = lambda n: pl.BlockSpec((tm, n), lambda i: (i, 0))
    full = lambda a: pl.BlockSpec(a.shape, lambda i: (0,) * a.ndim)
    hrow = lambda n: pl.BlockSpec((MLA_HEADS, tm, n), lambda i: (0, i, 0))
    common_in = [x, w["mix_g_pre"], w["w_big"], w["mla_g_q"], w["wq_a"], w["wq_b"], w["mla_g_kv"], tc, ts]
    common_specs = [row(d)] + [full(a) for a in common_in[1:7]] + [row(HEAD_BLOCK), row(HEAD_BLOCK)]
    common_out = [
        (jax.ShapeDtypeStruct((rows, MLA_KV_RANK), F32), row(MLA_KV_RANK)),
        (jax.ShapeDtypeStruct((rows, MLA_ROPE), F32), row(MLA_ROPE)),
        (jax.ShapeDtypeStruct((rows, GDN_QKV), F32), row(GDN_QKV)),
        (jax.ShapeDtypeStruct((rows, GDN_HEADS * GDN_DV), F32), row(GDN_HEADS * GDN_DV)),
        (jax.ShapeDtypeStruct((rows, HEAD_BLOCK), F32), row(HEAD_BLOCK)),
        (jax.ShapeDtypeStruct((rows, S5_WIDTH), F32), row(S5_WIDTH)),
    ]
    if sample:
        ins = common_in + [w["wuk_t"]]
        specs = common_specs + [full(w["wuk_t"])]
        outs = common_out + [
            (jax.ShapeDtypeStruct((MLA_HEADS, rows, MLA_KV_RANK), F32), hrow(MLA_KV_RANK)),
            (jax.ShapeDtypeStruct((MLA_HEADS, rows, MLA_ROPE), F32), hrow(MLA_ROPE)),
        ]
        body = _mixprep_sample_kernel
    else:
        ins = common_in + [w["wuk_pad"], w["wuv"]]
        specs = common_specs + [full(w["wuk_pad"]), full(w["wuv"])]
        outs = common_out + [
            (jax.ShapeDtypeStruct((MLA_HEADS, rows, HEAD_BLOCK), BF16), hrow(HEAD_BLOCK)),
            (jax.ShapeDtypeStruct((MLA_HEADS, rows, HEAD_BLOCK), BF16), hrow(HEAD_BLOCK)),
            (jax.ShapeDtypeStruct((MLA_HEADS * MLA_V, rows), BF16),
             pl.BlockSpec((MLA_HEADS * MLA_V, tm), lambda i: (0, i))),
        ]
        body = _mixprep_prompt_kernel
    return pl.pallas_call(
        body,
        grid=(rows // tm,),
        in_specs=specs,
        out_specs=[o[1] for o in outs],
        out_shape=[o[0] for o in outs],
        compiler_params=_cparams("parallel"),
        name="mixprep_sample" if sample else "mixprep_prompt",
    )(*ins)


def _attn_kernel(qi_ref, ki_ref, q_ref, k_ref, vt_ref, o_ref, m_ref, l_ref, acc_ref):
    p = pl.program_id(1)
    i = qi_ref[p]
    j = ki_ref[p]
    tq, tk = q_ref.shape[1], k_ref.shape[1]

    @pl.when(j == 0)
    def _():
        m_ref[...] = jnp.full(m_ref.shape, -jnp.inf, F32)
        l_ref[...] = jnp.zeros_like(l_ref)
        acc_ref[...] = jnp.zeros_like(acc_ref)

    sub = min(ATTN_SUB, tk)

    def step(masked):
        hrows = [slice(hh * MLA_V, (hh + 1) * MLA_V) for hh in range(2)]
        m = [m_ref[hh] for hh in range(2)]
        l = [l_ref[hh] for hh in range(2)]
        acc = [acc_ref[hrows[hh], :] for hh in range(2)]
        units = [(c, hh) for c in range(tk // sub) for hh in range(2)]

        def scores(c, hh):
            return lax.dot_general(k_ref[hh, c * sub:(c + 1) * sub, :], q_ref[hh], (((1,), (1,)), ((), ())),
                                   preferred_element_type=F32)

        st_next = scores(*units[0])
        for n, (c, hh) in enumerate(units):
            st = st_next
            if n + 1 < len(units):
                st_next = scores(*units[n + 1])
            if masked:
                key = lax.broadcasted_iota(jnp.int32, (sub, tq), 0) + c * sub
                qry = lax.broadcasted_iota(jnp.int32, (sub, tq), 1)
                st = jnp.where(key <= qry, st, -jnp.inf)
            m_new = jnp.maximum(m[hh], jnp.max(st, axis=0, keepdims=True))
            alpha = jnp.exp2(m[hh] - m_new)
            pt = jnp.exp2(st - m_new)
            l[hh] = alpha * l[hh] + jnp.sum(pt, axis=0, keepdims=True)
            acc[hh] = alpha * acc[hh] + jnp.dot(vt_ref[hrows[hh], c * sub:(c + 1) * sub], pt.astype(BF16),
                                                preferred_element_type=F32)
            m[hh] = m_new
        for hh in range(2):
            m_ref[hh] = m[hh]
            l_ref[hh] = l[hh]
            acc_ref[hrows[hh], :] = acc[hh]

    @pl.when(j < i)
    def _():
        step(False)

    @pl.when(j == i)
    def _():
        step(True)
        inv = jnp.concatenate([jnp.broadcast_to(1.0 / l_ref[hh], (MLA_V, tq)) for hh in range(2)], axis=0)
        o_ref[...] = (acc_ref[...] * inv).T.astype(o_ref.dtype)


def _prompt_attention(q, k, vt):
    rows = q.shape[1]
    t = min(ATTN_TILE, rows)
    n = rows // t
    qi = np.array([i for i in range(n) for _ in range(i + 1)], np.int32)
    ki = np.array([j for i in range(n) for j in range(i + 1)], np.int32)
    grid_spec = pltpu.PrefetchScalarGridSpec(
        num_scalar_prefetch=2,
        grid=(MLA_HEADS // 2, len(qi)),
        in_specs=[
            pl.BlockSpec((2, t, HEAD_BLOCK), lambda hp, p, qi, ki: (hp, qi[p], 0)),
            pl.BlockSpec((2, t, HEAD_BLOCK), lambda hp, p, qi, ki: (hp, ki[p], 0)),
            pl.BlockSpec((2 * MLA_V, t), lambda hp, p, qi, ki: (hp, ki[p])),
        ],
        out_specs=pl.BlockSpec((t, 2 * MLA_V), lambda hp, p, qi, ki: (qi[p], hp)),
        scratch_shapes=[pltpu.VMEM((2, 1, t), F32), pltpu.VMEM((2, 1, t), F32),
                        pltpu.VMEM((2 * MLA_V, t), F32)],
    )
    return pl.pallas_call(
        _attn_kernel,
        grid_spec=grid_spec,
        out_shape=jax.ShapeDtypeStruct((rows, MLA_HEADS * MLA_V), BF16),
        compiler_params=_cparams("parallel", "arbitrary"),
        name="mla_prompt_attention",
    )(jnp.asarray(qi), jnp.asarray(ki), q, k, vt)


def _paged_kernel(layer, n_pp, n_steps, n_total, pt_ref, qlat_ref, qpe_ref, cnew_ref, knew_ref, ckv_hbm, kpe_hbm,
                  o_ref, m_ref, l_ref, acc_ref, ckv_buf, kpe_buf, ckv_all_ref, kpe_all_ref, sem):
    j = pl.program_id(1)
    step = pl.program_id(0) * n_steps + j
    slot = lax.rem(step, PAGE_SLOTS)
    ahead = lax.rem(step + PAGE_AHEAD, n_total)
    ahead_slot = lax.rem(step + PAGE_AHEAD, PAGE_SLOTS)
    t_new = cnew_ref.shape[0]
    rows = MLA_HEADS * t_new

    n_grp = max(1, n_pp // PAGE_GROUP)
    per = n_pp // n_grp

    def page_copies(s, dst_slot, k):
        page = pt_ref[s * n_pp + k]
        return (pltpu.make_async_copy(ckv_hbm.at[layer, page], ckv_buf.at[dst_slot, k], sem.at[0, dst_slot]),
                pltpu.make_async_copy(kpe_hbm.at[layer, page], kpe_buf.at[dst_slot, k], sem.at[1, dst_slot]))

    def start_page(s, dst_slot, k):
        for cp in page_copies(s, dst_slot, k):
            cp.start()

    def wait_step(s, dst_slot):
        for k in range(n_pp):
            for cp in page_copies(s, dst_slot, k):
                cp.wait()

    @pl.when(step == 0)
    def _():
        for s in range(PAGE_AHEAD):
            for k in range(n_pp):
                start_page(s, s, k)

    ql = qlat_ref[...].reshape(rows, MLA_KV_RANK).astype(BF16)
    qp = qpe_ref[...].reshape(rows, MLA_ROPE).astype(BF16)

    @pl.when(j == 0)
    def _():
        cn = cnew_ref[...].astype(BF16)
        s = _bdot_nt(ql, cn) + _bdot_nt(qp, knew_ref[...])
        tok = lax.broadcasted_iota(jnp.int32, (rows, t_new), 0) % t_new
        key = lax.broadcasted_iota(jnp.int32, (rows, t_new), 1)
        s = jnp.where(key <= tok, s, -jnp.inf)
        m = jnp.max(s, axis=-1, keepdims=True)
        pm = jnp.exp(s - m)
        m_ref[...] = jnp.broadcast_to(m, m_ref.shape)
        l_ref[...] = jnp.broadcast_to(jnp.sum(pm, axis=-1, keepdims=True), l_ref.shape)
        acc_ref[...] = jnp.dot(pm.astype(BF16), cn, preferred_element_type=F32)

    wait_step(step, slot)
    gkeys = per * PAGE_SIZE

    def scores(g):
        for k in range(g * per, (g + 1) * per):
            start_page(ahead, ahead_slot, k)
            ckv_all_ref[k * PAGE_SIZE:(k + 1) * PAGE_SIZE, :] = ckv_buf[slot, k].astype(BF16)
            kpe_all_ref[:, k * PAGE_SIZE:(k + 1) * PAGE_SIZE] = kpe_buf[slot, k].astype(BF16)
        cg = ckv_all_ref[g * gkeys:(g + 1) * gkeys, :]
        kg = kpe_all_ref[:, g * gkeys:(g + 1) * gkeys]
        return _bdot_nt(ql, cg) + jnp.dot(qp, kg, preferred_element_type=F32), cg

    m = m_ref[...]
    l = l_ref[...]
    acc = acc_ref[...]
    nxt = scores(0)
    for g in range(n_grp):
        s, cg = nxt
        if g + 1 < n_grp:
            nxt = scores(g + 1)
        m_new = jnp.maximum(m, jnp.max(s, axis=-1, keepdims=True))
        alpha = jnp.exp(m - m_new)
        pm = jnp.exp(s - m_new[:, :1])
        l = alpha * l + jnp.sum(pm, axis=-1, keepdims=True)
        acc = alpha[:, :1] * acc + jnp.dot(pm.astype(BF16), cg, preferred_element_type=F32)
        m = m_new
    m_ref[...] = m
    l_ref[...] = l
    acc_ref[...] = acc

    @pl.when(j == n_steps - 1)
    def _():
        o = acc_ref[...] / l_ref[...][:, :1]
        o_ref[...] = o.reshape(MLA_HEADS, t_new, MLA_KV_RANK)

    @pl.when(step == n_total - 1)
    def _():
        for s in range(PAGE_AHEAD):
            wait_step(s, (n_total + s) % PAGE_SLOTS)


def _paged_attention(layer, qlat, qpe, ckv_new, kpe_new, cache_ckv, cache_kpe_t, page_table, t_new):
    n_b, n_pages = page_table.shape
    n_pp = min(PAGES_PER_STEP, n_pages)
    n_steps = n_pages // n_pp
    assert n_b * n_steps >= PAGE_AHEAD, "the page prefetch chain needs at least PAGE_AHEAD grid steps"
    rows = MLA_HEADS * t_new
    grid_spec = pltpu.PrefetchScalarGridSpec(
        num_scalar_prefetch=1,
        grid=(n_b, n_steps),
        in_specs=[
            pl.BlockSpec((MLA_HEADS, t_new, MLA_KV_RANK), lambda b, j, pt: (0, b, 0)),
            pl.BlockSpec((MLA_HEADS, t_new, MLA_ROPE), lambda b, j, pt: (0, b, 0)),
            pl.BlockSpec((t_new, MLA_KV_RANK), lambda b, j, pt: (b, 0)),
            pl.BlockSpec((t_new, MLA_ROPE), lambda b, j, pt: (b, 0)),
            pl.BlockSpec(memory_space=pl.ANY),
            pl.BlockSpec(memory_space=pl.ANY),
        ],
        out_specs=pl.BlockSpec((MLA_HEADS, t_new, MLA_KV_RANK), lambda b, j, pt: (0, b, 0)),
        scratch_shapes=[pltpu.VMEM((rows, HEAD_BLOCK), F32), pltpu.VMEM((rows, HEAD_BLOCK), F32),
                        pltpu.VMEM((rows, MLA_KV_RANK), F32),
                        pltpu.VMEM((PAGE_SLOTS, n_pp, PAGE_SIZE, MLA_KV_RANK), F32),
                        pltpu.VMEM((PAGE_SLOTS, n_pp, MLA_ROPE, PAGE_SIZE), F32),
                        pltpu.VMEM((n_pp * PAGE_SIZE, MLA_KV_RANK), BF16),
                        pltpu.VMEM((MLA_ROPE, n_pp * PAGE_SIZE), BF16),
                        pltpu.SemaphoreType.DMA((2, PAGE_SLOTS))],
    )
    return pl.pallas_call(
        functools.partial(_paged_kernel, layer, n_pp, n_steps, n_b * n_steps),
        grid_spec=grid_spec,
        out_shape=jax.ShapeDtypeStruct((MLA_HEADS, n_b * t_new, MLA_KV_RANK), F32),
        compiler_params=_cparams("arbitrary", "arbitrary"),
        name="mla_paged_attention",
    )(page_table.reshape(-1), qlat, qpe, ckv_new, kpe_new, cache_ckv, cache_kpe_t)


def _gdn_gates(misc, alog, dtb):
    beta = jax.nn.sigmoid(misc)
    g = -jnp.exp(alog) * _softplus(misc + dtb)
    return beta, g


def _l2n(x, scale):
    return x * (lax.rsqrt(jnp.sum(x * x, axis=-1, keepdims=True) + 1e-6) * scale)


def _gdn_prompt_kernel(qkv_ref, z_ref, misc_ref, convw_ref, alog_ref, dtb_ref, gnorm_ref,
                       o_ref, sfin_ref, s_ref, carry_ref):
    i = pl.program_id(0)

    @pl.when(i == 0)
    def _():
        s_ref[...] = jnp.zeros_like(s_ref)
        carry_ref[...] = jnp.zeros_like(carry_ref)

    x = qkv_ref[...]
    tm = x.shape[0]
    w = convw_ref[...]
    row8 = lax.broadcasted_iota(jnp.int32, (8, GDN_QKV), 0)
    cprev = carry_ref[...]
    acc = x * w[GDN_CONV - 1:GDN_CONV]
    for d in range(1, GDN_CONV):
        xr = pltpu.roll(x, d, 0)
        head = jnp.where(row8 < d, pltpu.roll(cprev, d, 0), xr[0:8])
        xs = jnp.concatenate([head, xr[8:]], axis=0)
        acc = acc + xs * w[GDN_CONV - 1 - d:GDN_CONV - d]
    carry_ref[...] = x[tm - 8:tm]
    conv = _silu(acc)

    nq = GDN_HEADS * GDN_DK
    beta_all, g_all = _gdn_gates(misc_ref[...], alog_ref[...], dtb_ref[...])
    z = z_ref[...]
    gnorm = gnorm_ref[...]
    c = GDN_CHUNK
    blk = 2 * c
    ri = lax.broadcasted_iota(jnp.int32, (blk, blk), 0)
    ci = lax.broadcasted_iota(jnp.int32, (blk, blk), 1)
    tri2 = jnp.where((ri >= ci) & ((ri // c) == (ci // c)), 1.0, 0.0).astype(BF16)
    ns = GDN_HEADS * c
    rs_ = lax.broadcasted_iota(jnp.int32, (ns, ns), 0)
    cs_ = lax.broadcasted_iota(jnp.int32, (ns, ns), 1)
    same = (rs_ // c) == (cs_ // c)
    incl = same & (rs_ >= cs_)
    strict = same & (rs_ > cs_)
    heads = range(GDN_HEADS)
    stack = lambda parts: jnp.concatenate(parts, axis=0)

    solved = []
    for b2 in range(tm // blk):
        gblk = g_all[b2 * blk:(b2 + 1) * blk]
        g0, g1, g2 = _split3(gblk)
        d = functools.partial(jnp.dot, preferred_element_type=F32)
        gcum = d(tri2, g0) + d(tri2, g1) + d(tri2, g2)
        gcum_t = gcum.T
        for c2 in range(2):
            r0 = b2 * blk + c2 * c
            cr = slice(c2 * c, (c2 + 1) * c)
            qs = [_l2n(conv[r0:r0 + c, h * GDN_DK:(h + 1) * GDN_DK], GDN_DK ** -0.5) for h in heads]
            ks = [_l2n(conv[r0:r0 + c, nq + h * GDN_DK:nq + (h + 1) * GDN_DK], 1.0) for h in heads]
            vs = [conv[r0:r0 + c, 2 * nq + h * GDN_DV:2 * nq + (h + 1) * GDN_DV] for h in heads]
            gcs = [gcum[cr, GDN_HEADS + h:GDN_HEADS + h + 1] for h in heads]
            q_st, k_st, v_st, gc_st = stack(qs), stack(ks), stack(vs), stack(gcs)
            gr_st = jnp.concatenate([gcum_t[GDN_HEADS + h:GDN_HEADS + h + 1, cr] for h in heads], axis=1)
            beta_st = stack([beta_all[r0:r0 + c, h:h + 1] for h in heads])
            decay = jnp.exp(jnp.where(incl, gc_st - gr_st, -jnp.inf))
            eg_st = jnp.exp(gc_st)
            kk = _bdot_nt(k_st, k_st)
            mk = -jnp.where(strict, beta_st * kk * decay, 0.0)
            rhs = jnp.concatenate([v_st * beta_st, k_st * (beta_st * eg_st)], axis=-1)
            qk = _bdot_nt(q_st, k_st) * decay
            solved.append([r0, rhs, qk, qs, ks, gcs, eg_st, mk])

    n_sq = int(math.log2(c))
    for kq in range(n_sq):
        for item in solved:
            item[1] = item[1] + _dot_split(item[7], item[1])
            if kq < n_sq - 1:
                item[7] = _dot_split(item[7], item[7])

    state = [s_ref[h] for h in heads]
    for r0, rhs, qk, qs, ks, gcs, eg_st, _ in solved:
        outs = []
        for h in heads:
            hr = slice(h * c, (h + 1) * c)
            uu = rhs[hr, :GDN_DV]
            ww = rhs[hr, GDN_DV:]
            gc = gcs[h]
            glast = gc[c - 1:c, :]
            s_h = state[h]
            v_new = uu - _bdot(ww, s_h)
            o_h = _bdot(qs[h] * eg_st[hr], s_h) + _bdot(qk[hr, h * c:(h + 1) * c], v_new)
            kdec = ks[h] * jnp.exp(glast - gc)
            state[h] = s_h * jnp.exp(glast) + lax.dot_general(
                kdec.astype(BF16), v_new.astype(BF16), (((0,), (0,)), ((), ())),
                preferred_element_type=F32)
            zh = z[r0:r0 + c, h * GDN_DV:(h + 1) * GDN_DV]
            outs.append(_rms(o_h, gnorm) * _silu(zh))
        o_ref[r0:r0 + c, :] = jnp.concatenate(outs, axis=-1).astype(o_ref.dtype)
    for h in heads:
        s_ref[h] = state[h]

    @pl.when(i == pl.num_programs(0) - 1)
    def _():
        sfin_ref[...] = s_ref[...]


def _gdn_prompt(qkv, z, misc, w):
    rows = qkv.shape[0]
    tm = min(256, rows)
    row = lambda n: pl.BlockSpec((tm, n), lambda i: (i, 0))
    full = lambda a: pl.BlockSpec(a.shape, lambda i: (0,) * a.ndim)
    st = (GDN_HEADS, GDN_DK, GDN_DV)
    ins = [qkv, z, misc, w["gdn_conv_w"], w["gdn_alog_pad"], w["gdn_dtb_pad"], w["gdn_g_norm"]]
    return pl.pallas_call(
        _gdn_prompt_kernel,
        grid=(rows // tm,),
        in_specs=[row(GDN_QKV), row(GDN_HEADS * GDN_DV), row(HEAD_BLOCK)] + [full(a) for a in ins[3:]],
        out_specs=[row(GDN_HEADS * GDN_DV), pl.BlockSpec(st, lambda i: (0, 0, 0))],
        out_shape=[jax.ShapeDtypeStruct((rows, GDN_HEADS * GDN_DV), BF16), jax.ShapeDtypeStruct(st, F32)],
        scratch_shapes=[pltpu.VMEM(st, F32), pltpu.VMEM((8, GDN_QKV), F32)],
        compiler_params=_cparams("arbitrary"),
        name="gdn_prompt",
    )(*ins)


def _gdn_sample_kernel(t_new, qkv_ref, st_ref, z_ref, misc_ref, convw_ref, alog_ref, dtb_ref, gcol_ref,
                       s0_ref, o_ref, s_ref,
                       rows_ref, zo_ref, gate_ref, qt_ref, kt_ref, vt_ref, gt_ref, ot_ref):
    n_rows = qkv_ref.shape[0]
    n_b = n_rows // t_new
    nq = GDN_HEADS * GDN_DK
    x = qkv_ref[...]
    st = st_ref[...]
    w = convw_ref[...]
    tpos = lax.broadcasted_iota(jnp.int32, (n_rows, GDN_QKV), 0) % t_new
    acc = x * w[GDN_CONV - 1:GDN_CONV]
    for d in range(1, GDN_CONV):
        xr = pltpu.roll(x, d, 0)
        back = GDN_CONV - 1 - d
        sr = st if back == 0 else pltpu.roll(st, n_rows - back, 0)
        acc = acc + jnp.where(tpos < d, sr, xr) * w[GDN_CONV - 1 - d:GDN_CONV - d]
    conv = _silu(acc)
    parts = []
    for h in range(GDN_HEADS):
        parts.append(_l2n(conv[:, h * GDN_DK:(h + 1) * GDN_DK], GDN_DK ** -0.5))
    for h in range(GDN_HEADS):
        parts.append(_l2n(conv[:, nq + h * GDN_DK:nq + (h + 1) * GDN_DK], 1.0))
    parts.append(conv[:, 2 * nq:])
    feats = jnp.concatenate(parts, axis=-1)
    n_chunk = GDN_QKV // HEAD_BLOCK
    per_part = nq // HEAD_BLOCK
    for cc in range(n_chunk):
        rows_ref[cc] = feats[:, cc * HEAD_BLOCK:(cc + 1) * HEAD_BLOCK]
    zsil = _silu(z_ref[...])
    for cc in range(per_part):
        zo_ref[cc] = zsil[:, cc * HEAD_BLOCK:(cc + 1) * HEAD_BLOCK]
    beta_all, g_all = _gdn_gates(misc_ref[...], alog_ref[...], dtb_ref[...])
    lane = lax.broadcasted_iota(jnp.int32, beta_all.shape, 1)
    gate_ref[...] = jnp.where(lane < GDN_HEADS, beta_all, jnp.exp(g_all))

    for t in range(t_new):
        for cc in range(n_chunk):
            blk_t = rows_ref[cc, pl.ds(t, n_b, stride=t_new), :].T
            dst = (qt_ref, kt_ref, vt_ref)[cc // per_part]
            lo = (cc % per_part) * HEAD_BLOCK
            dst[t, lo:lo + HEAD_BLOCK, :] = blk_t
        gt_ref[t] = gate_ref[pl.ds(t, n_b, stride=t_new), :].T

    s_ref[...] = s0_ref[...]

    gcol = gcol_ref[...]
    for t in range(t_new):
        for h in range(GDN_HEADS):
            egr = gt_ref[t, GDN_HEADS + h:GDN_HEADS + h + 1, :]
            betar = gt_ref[t, h:h + 1, :]
            base = h * GDN_DK

            def p1(dk, racc):
                kb = kt_ref[t, pl.ds(base + dk, 1), :]
                return racc + s_ref[base + dk] * kb

            rr = lax.fori_loop(0, GDN_DK, p1, jnp.zeros((GDN_DV, n_b), F32), unroll=8) * egr
            dd = betar * (vt_ref[t, base:base + GDN_DV, :] - rr)

            def p2(dk, oacc):
                kb = kt_ref[t, pl.ds(base + dk, 1), :]
                qb = qt_ref[t, pl.ds(base + dk, 1), :]
                sn = s_ref[base + dk] * egr + kb * dd
                s_ref[base + dk] = sn
                return oacc + sn * qb

            oo = lax.fori_loop(0, GDN_DK, p2, jnp.zeros((GDN_DV, n_b), F32), unroll=8)
            on = oo * lax.rsqrt(jnp.mean(oo * oo, axis=0, keepdims=True) + RMS_EPS) * gcol
            ot_ref[base:base + GDN_DV, :] = on
        on_rows = ot_ref[...].T
        for cc in range(per_part):
            zt = zo_ref[cc, pl.ds(t, n_b, stride=t_new), :]
            zo_ref[cc, pl.ds(t, n_b, stride=t_new), :] = on_rows[:, cc * HEAD_BLOCK:(cc + 1) * HEAD_BLOCK] * zt
    for cc in range(per_part):
        o_ref[:, cc * HEAD_BLOCK:(cc + 1) * HEAD_BLOCK] = zo_ref[cc]


def _gdn_sample(qkv, st_rows, z, misc, w, s0, t_new):
    rows = qkv.shape[0]
    n_b = rows // t_new
    feat = GDN_HEADS * GDN_DK
    ins = [qkv, st_rows, z, misc, w["gdn_conv_w"], w["gdn_alog_pad"], w["gdn_dtb_pad"], w["gdn_g_col"], s0]
    return pl.pallas_call(
        functools.partial(_gdn_sample_kernel, t_new),
        out_shape=[jax.ShapeDtypeStruct((rows, GDN_HEADS * GDN_DV), F32),
                   jax.ShapeDtypeStruct((GDN_HEADS * GDN_DK, GDN_DV, n_b), F32)],
        scratch_shapes=[
            pltpu.VMEM((GDN_QKV // HEAD_BLOCK, rows, HEAD_BLOCK), F32),
            pltpu.VMEM((GDN_HEADS * GDN_DV // HEAD_BLOCK, rows, HEAD_BLOCK), F32),
            pltpu.VMEM((rows, HEAD_BLOCK), F32),
            pltpu.VMEM((t_new, feat, n_b), F32),
            pltpu.VMEM((t_new, feat, n_b), F32),
            pltpu.VMEM((t_new, GDN_HEADS * GDN_DV, n_b), F32),
            pltpu.VMEM((t_new, HEAD_BLOCK, n_b), F32),
            pltpu.VMEM((GDN_HEADS * GDN_DV, n_b), F32),
        ],
        compiler_params=pltpu.CompilerParams(vmem_limit_bytes=VMEM_LIMIT_BYTES),
        name="gdn_sample",
    )(*ins)


def _cmul(ar, ai, br, bi):
    return ar * br - ai * bi, ar * bi + ai * br


def _s5_kernel(per_group_state, u_ref, bbr_ref, bbi_ref, ar_ref, ai_ref, cr_ref, ci_ref, d_ref,
               wglu_ref, bglu_ref, h0r_ref, h0i_ref, o_ref, hr_out_ref, hi_out_ref,
               br_ref, bi_ref, cr_carry_ref, ci_carry_ref):
    i = pl.program_id(0)
    tm = u_ref.shape[0]
    n_grp = tm // 8

    if not per_group_state:
        @pl.when(i == 0)
        def _():
            cr_carry_ref[...] = jnp.zeros_like(cr_carry_ref)
            ci_carry_ref[...] = jnp.zeros_like(ci_carry_ref)

    u = u_ref[...]
    ub = u.astype(BF16)
    br_ref[...] = jnp.dot(ub, bbr_ref[...], preferred_element_type=F32)
    bi_ref[...] = jnp.dot(ub, bbi_ref[...], preferred_element_type=F32)

    ar = ar_ref[...]
    ai = ai_ref[...]
    p1 = (ar, ai)
    p2 = _cmul(*p1, *p1)
    p3 = _cmul(*p2, *p1)
    p4 = _cmul(*p2, *p2)
    p5 = _cmul(*p4, *p1)
    p6 = _cmul(*p4, *p2)
    p7 = _cmul(*p4, *p3)
    p8 = _cmul(*p4, *p4)
    row8 = lax.broadcasted_iota(jnp.int32, (8, S5_LANES), 0)
    pw_r = jnp.zeros((8, S5_LANES), F32)
    pw_i = jnp.zeros((8, S5_LANES), F32)
    for t, pw in enumerate((p1, p2, p3, p4, p5, p6, p7, p8)):
        pw_r = jnp.where(row8 == t, pw[0], pw_r)
        pw_i = jnp.where(row8 == t, pw[1], pw_i)

    def body(gi, carry):
        r0 = pl.multiple_of(gi * 8, 8)
        a = br_ref[pl.ds(r0, 8), :]
        b = bi_ref[pl.ds(r0, 8), :]
        for d, pw in ((1, p1), (2, p2), (4, p4)):
            a_s = jnp.where(row8 >= d, pltpu.roll(a, d, 0), 0.0)
            b_s = jnp.where(row8 >= d, pltpu.roll(b, d, 0), 0.0)
            da, db = _cmul(pw[0], pw[1], a_s, b_s)
            a = a + da
            b = b + db
        if per_group_state:
            c_r = h0r_ref[pl.ds(gi, 1), :]
            c_i = h0i_ref[pl.ds(gi, 1), :]
        else:
            c_r, c_i = carry
        da, db = _cmul(pw_r, pw_i, c_r, c_i)
        a = a + da
        b = b + db
        br_ref[pl.ds(r0, 8), :] = a
        bi_ref[pl.ds(r0, 8), :] = b
        if per_group_state:
            hr_out_ref[pl.ds(gi, 1), :] = a[7:8]
            hi_out_ref[pl.ds(gi, 1), :] = b[7:8]
            return carry
        return a[7:8], b[7:8]

    if per_group_state:
        zero = jnp.zeros((1, S5_LANES), F32)
        lax.fori_loop(0, n_grp, body, (zero, zero))
    else:
        c_fin = lax.fori_loop(0, n_grp, body, (cr_carry_ref[...], ci_carry_ref[...]))
        cr_carry_ref[...] = c_fin[0]
        ci_carry_ref[...] = c_fin[1]
        hr_out_ref[...] = c_fin[0]
        hi_out_ref[...] = c_fin[1]

    y = (jnp.dot(br_ref[...].astype(BF16), cr_ref[...], preferred_element_type=F32)
         - jnp.dot(bi_ref[...].astype(BF16), ci_ref[...], preferred_element_type=F32)
         + d_ref[...] * u)
    zg = jax.nn.gelu(y)
    gate = jax.nn.sigmoid(jnp.dot(zg.astype(BF16), wglu_ref[...], preferred_element_type=F32) + bglu_ref[...])
    o_ref[...] = (zg * gate).astype(o_ref.dtype)


def _s5(su, w, h0r, h0i, per_group_state, out_dtype):
    rows = su.shape[0]
    tm = min(256, rows)
    n_grp = tm // 8
    row = lambda n: pl.BlockSpec((tm, n), lambda i: (i, 0))
    full = lambda a: pl.BlockSpec(a.shape, lambda i: (0,) * a.ndim)
    wlist = [w["s5_bb_re"], w["s5_bb_im"], w["s5_ab_re"], w["s5_ab_im"], w["s5_c_re"], w["s5_c_im"],
             w["s5_d"], w["s5_w_glu"], w["s5_b_glu"]]
    if per_group_state:
        st_spec = pl.BlockSpec((n_grp, S5_LANES), lambda i: (i, 0))
        st_shape = jax.ShapeDtypeStruct((rows // 8, S5_LANES), F32)
    else:
        st_spec = pl.BlockSpec((1, S5_LANES), lambda i: (0, 0))
        st_shape = jax.ShapeDtypeStruct((1, S5_LANES), F32)
    return pl.pallas_call(
        functools.partial(_s5_kernel, per_group_state),
        grid=(rows // tm,),
        in_specs=[row(S5_WIDTH)] + [full(a) for a in wlist] + [st_spec, st_spec],
        out_specs=[row(S5_WIDTH), st_spec, st_spec],
        out_shape=[jax.ShapeDtypeStruct((rows, S5_WIDTH), out_dtype), st_shape, st_shape],
        scratch_shapes=[pltpu.VMEM((tm, S5_LANES), F32), pltpu.VMEM((tm, S5_LANES), F32),
                        pltpu.VMEM((1, S5_LANES), F32), pltpu.VMEM((1, S5_LANES), F32)],
        compiler_params=_cparams("arbitrary"),
        name="s5_sample" if per_group_state else "s5_prompt",
    )(su, *wlist, h0r, h0i)


def _mixout_kernel(x_ref, oa_ref, og_ref, os_ref, wuv_ref, woa_ref, wob_ref, woc_ref,
                   gpost_ref, gxa_ref, wq_ref, x2_ref, q_ref):
    mixed = None
    for h in range(MLA_HEADS):
        o_h = jnp.dot(oa_ref[h].astype(BF16), wuv_ref[h], preferred_element_type=F32)
        t = jnp.dot(o_h.astype(BF16), woa_ref[h * MLA_V:(h + 1) * MLA_V, :], preferred_element_type=F32)
        mixed = t if mixed is None else mixed + t
    mixed = mixed + jnp.dot(og_ref[...].astype(BF16), wob_ref[...], preferred_element_type=F32)
    mixed = mixed + jnp.dot(os_ref[...].astype(BF16), woc_ref[...], preferred_element_type=F32)
    x2 = x_ref[...] + _rms(mixed, gpost_ref[...])
    x2_ref[...] = x2
    hq = _rms(x2, gxa_ref[...]).astype(BF16)
    q_ref[...] = jnp.dot(hq, wq_ref[...], preferred_element_type=F32) * XA_SCALE


def _mixout(x, o_lat, o_gdn, o_s5, w):
    rows, d = x.shape
    tm = min(512, rows)
    row = lambda n: pl.BlockSpec((tm, n), lambda i: (i, 0))
    full = lambda a: pl.BlockSpec(a.shape, lambda i: (0,) * a.ndim)
    oa_spec = pl.BlockSpec((MLA_HEADS, tm, MLA_KV_RANK), lambda i: (0, i, 0))
    wl = [w["wuv_h"], w["wo_a"], w["wo_b"], w["wo_c"], w["mix_g_post"], w["xa_g_pre"], w["xa_w_q"]]
    return pl.pallas_call(
        _mixout_kernel,
        grid=(rows // tm,),
        in_specs=[row(d), oa_spec, row(o_gdn.shape[1]), row(o_s5.shape[1])] + [full(a) for a in wl],
        out_specs=[row(d), row(d)],
        out_shape=[jax.ShapeDtypeStruct((rows, d), F32), jax.ShapeDtypeStruct((rows, d), F32)],
        compiler_params=_cparams("parallel"),
        name="mixout_sample",
    )(x, o_lat, o_gdn, o_s5, *wl)


def _postmix_kernel(x_ref, oa_ref, og_ref, os_ref, woa_ref, wob_ref, woc_ref, gpost_ref, gxa_ref, wq_ref,
                    mk_ref, mv_ref, wo_ref, gxo_ref, y_ref):
    mixed = jnp.dot(oa_ref[...], woa_ref[...], preferred_element_type=F32)
    mixed = mixed + jnp.dot(og_ref[...], wob_ref[...], preferred_element_type=F32)
    mixed = mixed + jnp.dot(os_ref[...], woc_ref[...], preferred_element_type=F32)
    x2 = x_ref[...] + _rms(mixed, gpost_ref[...])
    q = jnp.dot(_rms(x2, gxa_ref[...]).astype(BF16), wq_ref[...], preferred_element_type=F32) * XA_SCALE
    outs = []
    for h in range(XA_HEADS):
        hs = slice(h * XA_HEAD_DIM, (h + 1) * XA_HEAD_DIM)
        s = _bdot_nt(q[:, hs], mk_ref[:, hs])
        m = jnp.max(s, axis=-1, keepdims=True)
        pm = jnp.exp(s - m)
        pr = pm / jnp.sum(pm, axis=-1, keepdims=True)
        outs.append(_bdot(pr, mv_ref[:, hs]).astype(BF16))
    xa = jnp.dot(jnp.concatenate(outs, axis=-1), wo_ref[...], preferred_element_type=F32)
    y_ref[...] = x2 + _rms(xa, gxo_ref[...])


def _postmix(x, o_mla, o_gdn, o_s5, w, mem_k, mem_v, layer):
    rows, d = x.shape
    n_mem = mem_k.shape[2]
    tm = min(512, rows)
    row = lambda n: pl.BlockSpec((tm, n), lambda i: (i, 0))
    full = lambda a: pl.BlockSpec(a.shape, lambda i: (0,) * a.ndim)
    mspec = pl.BlockSpec((None, None, n_mem, d), lambda i: (layer, 0, 0, 0))
    w1 = [w["wo_a"], w["wo_b"], w["wo_c"], w["mix_g_post"], w["xa_g_pre"], w["xa_w_q"]]
    w2 = [w["xa_w_o"], w["xa_g_post"]]
    return pl.pallas_call(
        _postmix_kernel,
        grid=(rows // tm,),
        in_specs=([row(d), row(o_mla.shape[1]), row(o_gdn.shape[1]), row(o_s5.shape[1])]
                  + [full(a) for a in w1] + [mspec, mspec] + [full(a) for a in w2]),
        out_specs=row(d),
        out_shape=jax.ShapeDtypeStruct((rows, d), F32),
        compiler_params=_cparams("parallel"),
        name="postmix_prompt",
    )(x, o_mla, o_gdn, o_s5, *w1, mem_k, mem_v, *w2)


def _xattn_heads_merged_kernel(n_b, q_ref, mk_ref, mv_ref, o_ref):
    rows = q_ref.shape[0] // n_b
    n_mem = mk_ref.shape[1]
    for bi in range(n_b):
        q = q_ref[bi * rows:(bi + 1) * rows, :]
        q_all = jnp.concatenate([q[:, h * XA_HEAD_DIM:(h + 1) * XA_HEAD_DIM] for h in range(XA_HEADS)], axis=0)
        k2 = mk_ref[bi].reshape(n_mem * XA_HEADS, XA_HEAD_DIM)
        v2 = mv_ref[bi].reshape(n_mem * XA_HEADS, XA_HEAD_DIM)
        s = _bdot_nt(q_all, k2)
        q_head = lax.broadcasted_iota(jnp.int32, s.shape, 0) // rows
        m_head = lax.broadcasted_iota(jnp.int32, s.shape, 1) % XA_HEADS
        s = jnp.where(q_head == m_head, s, -jnp.inf)
        m = jnp.max(s, axis=-1, keepdims=True)
        pm = jnp.exp(s - m)
        pr = pm / jnp.sum(pm, axis=-1, keepdims=True)
        o_all = _bdot(pr, v2)
        o_ref[bi * rows:(bi + 1) * rows, :] = jnp.concatenate(
            [o_all[h * rows:(h + 1) * rows] for h in range(XA_HEADS)], axis=-1).astype(o_ref.dtype)


def _xattn_sample(q, mem_k, mem_v, layer, t_new):
    rows, d = q.shape
    n_mem = mem_k.shape[2]
    n_b = min(XA_SEQ_PER_STEP, rows // t_new)
    tm = n_b * t_new
    mspec = pl.BlockSpec((None, n_b, n_mem, XA_HEADS, XA_HEAD_DIM), lambda i: (layer, i, 0, 0, 0))
    return pl.pallas_call(
        functools.partial(_xattn_heads_merged_kernel, n_b),
        grid=(rows // tm,),
        in_specs=[pl.BlockSpec((tm, d), lambda i: (i, 0)), mspec, mspec],
        out_specs=pl.BlockSpec((tm, d), lambda i: (i, 0)),
        out_shape=jax.ShapeDtypeStruct((rows, d), BF16),
        compiler_params=_cparams("parallel"),
        name="xattn_sample",
    )(q, mem_k, mem_v)


def _xaout_kernel(x_ref, o_ref, wo_ref, g_ref, y_ref):
    xa = jnp.dot(o_ref[...], wo_ref[...], preferred_element_type=F32)
    y_ref[...] = x_ref[...] + _rms(xa, g_ref[...])


def _xaout(x, o, w):
    rows, d = x.shape
    tm = min(512, rows)
    row = pl.BlockSpec((tm, d), lambda i: (i, 0))
    return pl.pallas_call(
        _xaout_kernel,
        grid=(rows // tm,),
        in_specs=[row, row, pl.BlockSpec((d, d), lambda i: (0, 0)), pl.BlockSpec((1, d), lambda i: (0, 0))],
        out_specs=row,
        out_shape=jax.ShapeDtypeStruct((rows, d), F32),
        compiler_params=_cparams("parallel"),
        name="xattn_out",
    )(x, o, w["xa_w_o"], w["xa_g_post"])


def _memproj_kernel(m_ref, wk_ref, wv_ref, k_ref, v_ref):
    m = m_ref[...].astype(BF16)
    for l in range(wk_ref.shape[0]):
        k_ref[l] = jnp.dot(m, wk_ref[l], preferred_element_type=F32)
        v_ref[l] = jnp.dot(m, wv_ref[l], preferred_element_type=F32)


def _memproj(mem, wk, wv):
    depth = wk.shape[0]
    shp = jax.ShapeDtypeStruct((depth,) + mem.shape, F32)
    return pl.pallas_call(
        _memproj_kernel,
        out_shape=[shp, shp],
        compiler_params=pltpu.CompilerParams(vmem_limit_bytes=VMEM_LIMIT_BYTES),
        name="mem_kv_proj",
    )(mem, wk, wv)


def _layer_weights(l, p):
    d_model = p["w_in"].shape[1]
    w = {}
    r1 = lambda a: a[l].reshape(1, -1).astype(F32)
    for name in ("ffn1", "ffn2"):
        w[name + "_g_pre"] = r1(p[name + "_g_pre"])
        w[name + "_g_post"] = r1(p[name + "_g_post"])
    for name in ("mix_g_pre", "mix_g_post", "mla_g_q", "mla_g_kv", "xa_g_pre", "xa_g_post", "gdn_g_norm",
                 "s5_d", "s5_b_glu"):
        w[name] = r1(p[name])
    w["gdn_g_col"] = p["gdn_g_norm"][l].reshape(-1, 1).astype(F32)

    w_in = p["w_in"][l]
    offs = np.cumsum([0, MLA_Q_RANK, MLA_KV_RANK, MLA_ROPE, GDN_QKV, GDN_HEADS * GDN_DV, GDN_HEADS, GDN_HEADS,
                      S5_WIDTH])
    w_cq, w_ckv, w_kpe, w_qkv, w_z, w_b, w_a, w_su = [w_in[:, offs[i]:offs[i + 1]] for i in range(8)]
    half = MLA_ROPE // 2
    zeros = lambda n: jnp.zeros((d_model, n), w_in.dtype)
    w_kpe_sw = jnp.concatenate([-w_kpe[:, half:], w_kpe[:, :half]], axis=1)
    tail = HEAD_BLOCK - MLA_NOPE - MLA_ROPE
    w_kpl = jnp.concatenate([zeros(MLA_NOPE), w_kpe, zeros(tail)], axis=1)
    w_kpls = jnp.concatenate([zeros(MLA_NOPE), w_kpe_sw, zeros(tail)], axis=1)
    w_misc = jnp.concatenate([w_b, w_a, zeros(HEAD_BLOCK - 2 * GDN_HEADS)], axis=1)
    w["w_big"] = jnp.concatenate([w_cq, w_ckv, w_kpl, w_kpls, w_misc, w_qkv, w_z, w_su], axis=1).astype(BF16)

    w_uq = p["mla_w_uq"][l].reshape(MLA_Q_RANK, MLA_HEADS, MLA_NOPE + MLA_ROPE)
    nope, x1, x2 = w_uq[..., :MLA_NOPE], w_uq[..., MLA_NOPE:MLA_NOPE + half], w_uq[..., MLA_NOPE + half:]
    zq = lambda n: jnp.zeros((MLA_Q_RANK, MLA_HEADS, n), w_uq.dtype)
    w["wq_a"] = jnp.concatenate([nope, x1, x2, zq(tail)], axis=-1).reshape(MLA_Q_RANK, -1).astype(BF16)
    w["wq_b"] = jnp.concatenate([zq(MLA_NOPE), -x2, x1, zq(tail)], axis=-1).reshape(MLA_Q_RANK, -1).astype(BF16)

    w_uk = p["mla_w_uk"][l]
    w_uv = p["mla_w_uv"][l]
    zk = jnp.zeros((MLA_KV_RANK, MLA_HEADS, HEAD_BLOCK - MLA_NOPE), w_uk.dtype)
    w["wuk_pad"] = jnp.concatenate([w_uk, zk], axis=-1).reshape(MLA_KV_RANK, -1).astype(BF16)
    wuk_t = jnp.transpose(w_uk, (1, 2, 0))
    w["wuk_t"] = jnp.concatenate(
        [wuk_t, jnp.zeros((MLA_HEADS, HEAD_BLOCK - MLA_NOPE, MLA_KV_RANK), w_uk.dtype)], axis=1).astype(BF16)
    w["wuv"] = w_uv.reshape(MLA_KV_RANK, -1).T.astype(BF16)
    w["wuv_h"] = jnp.transpose(w_uv, (1, 0, 2)).astype(BF16)

    w_out = p["w_out"][l]
    n_a = MLA_HEADS * MLA_V
    n_b = n_a + GDN_HEADS * GDN_DV
    w["wo_a"] = w_out[:n_a].astype(BF16)
    w["wo_b"] = w_out[n_a:n_b].astype(BF16)
    w["wo_c"] = w_out[n_b:].astype(BF16)
    w["xa_w_q"] = p["xa_w_q"][l].astype(BF16)
    w["xa_w_o"] = p["xa_w_o"][l].astype(BF16)

    w["gdn_conv_w"] = p["gdn_conv_w"][l].astype(F32)
    pad_gate = lambda v: jnp.zeros((1, HEAD_BLOCK), F32).at[0, GDN_HEADS:2 * GDN_HEADS].set(v.astype(F32))
    w["gdn_alog_pad"] = pad_gate(p["gdn_a_log"][l])
    w["gdn_dtb_pad"] = pad_gate(p["gdn_dt_bias"][l])

    a_re, a_im = p["s5_a_re"][l].astype(F32), p["s5_a_im"][l].astype(F32)
    dt = jnp.exp(p["s5_log_dt"][l].astype(F32))[:, None]
    mag = jnp.exp(a_re * dt)
    ab_re, ab_im = mag * jnp.cos(a_im * dt), mag * jnp.sin(a_im * dt)
    den = a_re * a_re + a_im * a_im
    nr, ni = ab_re - 1.0, ab_im
    coef_re = (nr * a_re + ni * a_im) / den
    coef_im = (ni * a_re - nr * a_im) / den
    b_re, b_im = p["s5_b_re"][l].astype(F32), p["s5_b_im"][l].astype(F32)
    bb_re = coef_re[..., None] * b_re - coef_im[..., None] * b_im
    bb_im = coef_re[..., None] * b_im + coef_im[..., None] * b_re
    eye = jnp.eye(S5_GROUPS, dtype=F32)
    bd_in = lambda bb: jnp.einsum("gnp,gh->gphn", bb, eye).reshape(S5_WIDTH, S5_LANES).astype(BF16)
    bd_out = lambda cc: jnp.einsum("gpn,gh->gnhp", cc.astype(F32), eye).reshape(S5_LANES, S5_WIDTH).astype(BF16)
    w["s5_bb_re"], w["s5_bb_im"] = bd_in(bb_re), bd_in(bb_im)
    w["s5_c_re"], w["s5_c_im"] = bd_out(p["s5_c_re"][l]), bd_out(p["s5_c_im"][l])
    w["s5_ab_re"] = ab_re.reshape(1, S5_LANES)
    w["s5_ab_im"] = ab_im.reshape(1, S5_LANES)
    w["s5_w_glu"] = p["s5_w_glu"][l].astype(BF16)
    return w


def _rope_tables(pos):
    half = MLA_ROPE // 2
    inv = ROPE_THETA ** (-jnp.arange(half, dtype=F32) / half)
    ang = pos.astype(F32)[:, None] * inv[None, :]
    c, s = jnp.cos(ang), jnp.sin(ang)
    n = pos.shape[0]
    z0 = jnp.zeros((n, MLA_NOPE), F32)
    z1 = jnp.zeros((n, HEAD_BLOCK - MLA_NOPE - MLA_ROPE), F32)
    return jnp.concatenate([z0, c, c, z1], axis=1), jnp.concatenate([z0, s, s, z1], axis=1)


def kernel(x_prompt, x_sample, mem_prompt, cache_ckv, cache_kpe, page_table, cache_mem_k, cache_mem_v, state_gdn, state_gdn_conv, state_s5_re, state_s5_im, ffn1_g_pre, ffn1_g_post, ffn1_w_gate, ffn1_w_up, ffn1_w_down, mix_g_pre, mix_g_post, w_in, w_out, mla_g_q, mla_w_uq, mla_g_kv, mla_w_uk, mla_w_uv, gdn_conv_w, gdn_a_log, gdn_dt_bias, gdn_g_norm, s5_a_re, s5_a_im, s5_log_dt, s5_b_re, s5_b_im, s5_c_re, s5_c_im, s5_d, s5_w_glu, s5_b_glu, xa_g_pre, xa_g_post, xa_w_q, xa_w_k, xa_w_v, xa_w_o, ffn2_g_pre, ffn2_g_post, ffn2_w_gate, ffn2_w_up, ffn2_w_down):
    params = dict(
        ffn1_g_pre=ffn1_g_pre, ffn1_g_post=ffn1_g_post, ffn1_w_gate=ffn1_w_gate, ffn1_w_up=ffn1_w_up,
        ffn1_w_down=ffn1_w_down, mix_g_pre=mix_g_pre, mix_g_post=mix_g_post, w_in=w_in, w_out=w_out,
        mla_g_q=mla_g_q, mla_w_uq=mla_w_uq, mla_g_kv=mla_g_kv, mla_w_uk=mla_w_uk, mla_w_uv=mla_w_uv,
        gdn_conv_w=gdn_conv_w, gdn_a_log=gdn_a_log, gdn_dt_bias=gdn_dt_bias, gdn_g_norm=gdn_g_norm,
        s5_a_re=s5_a_re, s5_a_im=s5_a_im, s5_log_dt=s5_log_dt, s5_b_re=s5_b_re, s5_b_im=s5_b_im,
        s5_c_re=s5_c_re, s5_c_im=s5_c_im, s5_d=s5_d, s5_w_glu=s5_w_glu, s5_b_glu=s5_b_glu,
        xa_g_pre=xa_g_pre, xa_g_post=xa_g_post, xa_w_q=xa_w_q, xa_w_o=xa_w_o,
        ffn2_g_pre=ffn2_g_pre, ffn2_g_post=ffn2_g_post, ffn2_w_gate=ffn2_w_gate, ffn2_w_up=ffn2_w_up,
        ffn2_w_down=ffn2_w_down)
    depth = w_in.shape[0]
    bsz, seqlen, d_model = x_prompt.shape
    dec_b, dec_t, _ = x_sample.shape
    n_mem = mem_prompt.shape[1]
    past_len = page_table.shape[1] * PAGE_SIZE
    n_s = dec_b * dec_t

    tc_p, ts_p = _rope_tables(jnp.arange(seqlen))
    tc_s, ts_s = _rope_tables(jnp.tile(past_len + jnp.arange(dec_t), dec_b))

    mem_k, mem_v = _memproj(mem_prompt.reshape(n_mem, d_model), xa_w_k.astype(BF16), xa_w_v.astype(BF16))
    mem_k4 = mem_k.reshape(depth, bsz, n_mem, d_model)
    mem_v4 = mem_v.reshape(depth, bsz, n_mem, d_model)
    cache_kpe_t = jnp.swapaxes(cache_kpe, 2, 3)

    yp = x_prompt.reshape(seqlen, d_model)
    ys = x_sample.reshape(n_s, d_model)
    zero_state = jnp.zeros((1, S5_LANES), F32)
    outs = {k: [] for k in ("p_ckv", "p_kpe", "p_gdn", "p_conv", "p_s5r", "p_s5i",
                            "s_ckv", "s_kpe", "s_gdn", "s_conv", "s_s5r", "s_s5i")}
    weights = [_layer_weights(l, params) for l in range(depth)]
    ffn1_w = [a.astype(BF16) for a in (ffn1_w_gate, ffn1_w_up, ffn1_w_down)]
    ffn2_w = [a.astype(BF16) for a in (ffn2_w_gate, ffn2_w_up, ffn2_w_down)]
    for l, w in enumerate(weights):
        ys = _half_ffn(ys, w["ffn1_g_pre"], w["ffn1_g_post"], *ffn1_w, l)
        ckv, kpe, qkv, z, misc, su, qlat, qpe = _mixprep(True, ys, w, tc_s, ts_s)
        o_lat = _paged_attention(l, qlat, qpe, ckv, kpe, cache_ckv, cache_kpe_t, page_table, dec_t)
        st_rows = jnp.pad(state_gdn_conv[l], ((0, 0), (0, dec_t - (GDN_CONV - 1)), (0, 0))).reshape(n_s, GDN_QKV)
        s0 = jnp.transpose(state_gdn[l], (1, 2, 3, 0)).reshape(GDN_HEADS * GDN_DK, GDN_DV, dec_b)
        o_gdn, gdn_s = _gdn_sample(qkv, st_rows, z, misc, w, s0, dec_t)
        o_s5, h_re, h_im = _s5(su, w, state_s5_re[l].reshape(dec_b, S5_LANES),
                               state_s5_im[l].reshape(dec_b, S5_LANES), True, F32)
        x2, xq = _mixout(ys, o_lat, o_gdn, o_s5, w)
        o_xa = _xattn_sample(xq, cache_mem_k, cache_mem_v, l, dec_t)
        ys = _xaout(x2, o_xa, w)
        ys = _half_ffn(ys, w["ffn2_g_pre"], w["ffn2_g_post"], *ffn2_w, l)
        outs["s_ckv"].append(ckv.reshape(dec_b, dec_t, MLA_KV_RANK))
        outs["s_kpe"].append(kpe.reshape(dec_b, dec_t, MLA_ROPE))
        outs["s_gdn"].append(jnp.transpose(gdn_s.reshape(GDN_HEADS, GDN_DK, GDN_DV, dec_b), (3, 0, 1, 2)))
        outs["s_conv"].append(qkv.reshape(dec_b, dec_t, GDN_QKV)[:, dec_t - (GDN_CONV - 1):])
        outs["s_s5r"].append(h_re.reshape(dec_b, S5_GROUPS, S5_STATE))
        outs["s_s5i"].append(h_im.reshape(dec_b, S5_GROUPS, S5_STATE))

    for l, w in enumerate(weights):
        yp = _half_ffn(yp, w["ffn1_g_pre"], w["ffn1_g_post"], *ffn1_w, l)
        ckv, kpe, qkv, z, misc, su, q, k, v = _mixprep(False, yp, w, tc_p, ts_p)
        o_mla = _prompt_attention(q, k, v)
        o_gdn, gdn_s = _gdn_prompt(qkv, z, misc, w)
        o_s5, h_re, h_im = _s5(su, w, zero_state, zero_state, False, BF16)
        yp = _postmix(yp, o_mla, o_gdn, o_s5, w, mem_k4, mem_v4, l)
        yp = _half_ffn(yp, w["ffn2_g_pre"], w["ffn2_g_post"], *ffn2_w, l)
        outs["p_ckv"].append(ckv.reshape(bsz, seqlen, MLA_KV_RANK))
        outs["p_kpe"].append(kpe.reshape(bsz, seqlen, MLA_ROPE))
        outs["p_gdn"].append(gdn_s.reshape(bsz, GDN_HEADS, GDN_DK, GDN_DV))
        outs["p_conv"].append(qkv[seqlen - (GDN_CONV - 1):].reshape(bsz, GDN_CONV - 1, GDN_QKV))
        outs["p_s5r"].append(h_re.reshape(bsz, S5_GROUPS, S5_STATE))
        outs["p_s5i"].append(h_im.reshape(bsz, S5_GROUPS, S5_STATE))

    st = {k: jnp.stack(v) for k, v in outs.items()}
    p_mem_k = mem_k.reshape(depth, bsz, n_mem, XA_HEADS, XA_HEAD_DIM)
    p_mem_v = mem_v.reshape(depth, bsz, n_mem, XA_HEADS, XA_HEAD_DIM)
    return (yp.reshape(bsz, seqlen, d_model), ys.reshape(dec_b, dec_t, d_model),
            st["p_ckv"], st["p_kpe"], st["p_gdn"], st["p_conv"], st["p_s5r"], st["p_s5i"], p_mem_k, p_mem_v,
            st["s_ckv"], st["s_kpe"], st["s_gdn"], st["s_conv"], st["s_s5r"], st["s_s5i"])
```

```python
import functools
import math

import jax
import jax.numpy as jnp
import numpy as np
from jax import lax
from jax.experimental import pallas as pl
from jax.experimental.pallas import tpu as pltpu

F32 = jnp.float32
BF16 = jnp.bfloat16

RMS_EPS = 1e-6
MLA_HEADS = 8
MLA_NOPE = 64
MLA_ROPE = 32
MLA_V = 64
MLA_Q_RANK = 384
MLA_KV_RANK = 256
ROPE_THETA = 10000.0
HEAD_BLOCK = 128
PAGE_SIZE = 128
FFN_CHUNKS = 4
PAGES_PER_STEP = 64
PAGE_AHEAD = 2
PAGE_SLOTS = PAGE_AHEAD + 1
PAGE_GROUP = 32
ATTN_HEADS = 4
ATTN_TILE = 1024
ATTN_SUB = 512
GDN_HEADS = 4
GDN_DK = 64
GDN_DV = 64
GDN_CONV = 4
GDN_CHUNK = 64
GDN_QKV = GDN_HEADS * (2 * GDN_DK + GDN_DV)
S5_GROUPS = 16
S5_GROUP = 16
S5_STATE = 64
S5_WIDTH = S5_GROUPS * S5_GROUP
S5_LANES = S5_GROUPS * S5_STATE
XA_HEADS = 4
XA_HEAD_DIM = 256
XA_SEQ_PER_STEP = 4

VMEM_LIMIT_BYTES = 56 * 1024 * 1024

MLA_SCALE = (MLA_NOPE + MLA_ROPE) ** -0.5
LOG2_E = math.log2(math.e)
XA_SCALE = XA_HEAD_DIM ** -0.5


def _cparams(*sem):
    return pltpu.CompilerParams(dimension_semantics=tuple(sem), vmem_limit_bytes=VMEM_LIMIT_BYTES)


def _rms(x, g):
    return x * lax.rsqrt(jnp.mean(x * x, axis=-1, keepdims=True) + RMS_EPS) * g


def _bdot(a, b):
    return jnp.dot(a.astype(BF16), b.astype(BF16), preferred_element_type=F32)


def _bdot_nt(a, b):
    return lax.dot_general(a.astype(BF16), b.astype(BF16), (((1,), (1,)), ((), ())),
                           preferred_element_type=F32)


def _split3(a):
    hi = a.astype(BF16)
    r1 = a - hi.astype(F32)
    mid = r1.astype(BF16)
    lo = (r1 - mid.astype(F32)).astype(BF16)
    return hi, mid, lo


def _dot_split(a, b):
    a0 = a.astype(BF16)
    a1 = (a - a0.astype(F32)).astype(BF16)
    b0 = b.astype(BF16)
    b1 = (b - b0.astype(F32)).astype(BF16)
    d = functools.partial(jnp.dot, preferred_element_type=F32)
    return d(a0, b0) + (d(a0, b1) + d(a1, b0))


def _silu(x):
    return x * jax.nn.sigmoid(x)


def _softplus(x):
    return jnp.maximum(x, 0.0) + jnp.log(1.0 + jnp.exp(-jnp.abs(x)))


def _lane_chunks(n, parts):
    tiles = n // HEAD_BLOCK
    cuts = [round(tiles * i / parts) * HEAD_BLOCK for i in range(parts + 1)]
    return [(a, b) for a, b in zip(cuts[:-1], cuts[1:]) if b > a]


def _ffn_kernel(x_ref, gpre_ref, gpost_ref, wg_ref, wu_ref, wd_ref, o_ref):
    tm = x_ref.shape[0]
    half = tm // 2
    halves = range(2)
    xs = [x_ref[h * half:(h + 1) * half, :] for h in halves]
    xn = [_rms(x, gpre_ref[...]).astype(BF16) for x in xs]
    accs = [None, None]
    for a, b in _lane_chunks(wg_ref.shape[1], FFN_CHUNKS):
        for h in halves:
            g = jnp.dot(xn[h], wg_ref[:, a:b], preferred_element_type=F32)
            u = jnp.dot(xn[h], wu_ref[:, a:b], preferred_element_type=F32)
            t = jnp.dot((_silu(g) * u).astype(BF16), wd_ref[a:b, :], preferred_element_type=F32)
            accs[h] = t if accs[h] is None else accs[h] + t
    for h in halves:
        o_ref[h * half:(h + 1) * half, :] = xs[h] + 0.5 * _rms(accs[h], gpost_ref[...])


def _half_ffn(x, g_pre, g_post, wg, wu, wd, layer):
    rows, d = x.shape
    f = wg.shape[2]
    tm = min(1024, rows)
    wspec = lambda shape: pl.BlockSpec((None,) + shape, lambda i: (layer, 0, 0), pipeline_mode=pl.Buffered(1))
    return pl.pallas_call(
        _ffn_kernel,
        grid=(rows // tm,),
        in_specs=[
            pl.BlockSpec((tm, d), lambda i: (i, 0)),
            pl.BlockSpec((1, d), lambda i: (0, 0)),
            pl.BlockSpec((1, d), lambda i: (0, 0)),
            wspec((d, f)), wspec((d, f)), wspec((f, d)),
        ],
        out_specs=pl.BlockSpec((tm, d), lambda i: (i, 0)),
        out_shape=jax.ShapeDtypeStruct((rows, d), F32),
        compiler_params=_cparams("parallel"),
        name="half_ffn",
    )(x, g_pre, g_post, wg, wu, wd)


C_CQ = (0, 384)
C_CKV = (384, 640)
C_KPL = (640, 768)
C_KPLS = (768, 896)
C_MISC = (896, 1024)
C_QKV = (1024, 1792)
C_Z = (1792, 2048)
C_SU = (2048, 2304)
W_BIG = 2304


def _mixprep_body(sample, x_ref, g_ref, wbig_ref, gq_ref, wqa_ref, wqb_ref, gkv_ref, tc_ref, ts_ref,
                  wk_ref, wv_ref, ckv_ref, kpe_ref, qkv_ref, z_ref, misc_ref, su_ref, a_ref, b_ref, c_ref):
    u = _rms(x_ref[...], g_ref[...]).astype(BF16)
    y = jnp.dot(u, wbig_ref[...], preferred_element_type=F32)
    sl = lambda c: y[:, c[0]:c[1]]
    tc = tc_ref[...]
    ts = ts_ref[...]
    kpe_pl = sl(C_KPL) * tc + sl(C_KPLS) * ts
    kpe_ref[...] = kpe_pl[:, MLA_NOPE:MLA_NOPE + MLA_ROPE]
    ckv_n = _rms(sl(C_CKV), gkv_ref[...])
    ckv_ref[...] = ckv_n
    qkv_ref[...] = sl(C_QKV)
    z_ref[...] = sl(C_Z)
    misc_ref[...] = sl(C_MISC)
    su_ref[...] = sl(C_SU)

    cqn = _rms(sl(C_CQ), gq_ref[...]).astype(BF16)
    qa = jnp.dot(cqn, wqa_ref[...], preferred_element_type=F32)
    qb = jnp.dot(cqn, wqb_ref[...], preferred_element_type=F32)
    lane = lax.broadcasted_iota(jnp.int32, (1, HEAD_BLOCK), 1)
    qmul = jnp.where(lane < MLA_NOPE, 1.0, 0.0) + tc
    ckb = ckv_n.astype(BF16)
    if sample:
        qlat_ref, qpe_ref = a_ref, b_ref
        for h in range(MLA_HEADS):
            hs = slice(h * HEAD_BLOCK, (h + 1) * HEAD_BLOCK)
            qh = qa[:, hs] * qmul + qb[:, hs] * ts
            qlat_ref[h] = jnp.dot(qa[:, hs].astype(BF16), wk_ref[h], preferred_element_type=F32) * MLA_SCALE
            qpe_ref[h] = qh[:, MLA_NOPE:MLA_NOPE + MLA_ROPE] * MLA_SCALE
    else:
        q_ref, k_ref, v_ref = a_ref, b_ref, c_ref
        kn = jnp.dot(ckb, wk_ref[...], preferred_element_type=F32)
        for h in range(MLA_HEADS):
            hs = slice(h * HEAD_BLOCK, (h + 1) * HEAD_BLOCK)
            qh = qa[:, hs] * qmul + qb[:, hs] * ts
            q_ref[h] = (qh * (MLA_SCALE * LOG2_E)).astype(BF16)
            k_ref[h] = (kn[:, hs] + kpe_pl).astype(BF16)
        v_ref[...] = _bdot_nt(wv_ref[...], ckb).astype(BF16)


def _mixprep_prompt_kernel(x_ref, g_ref, wbig_ref, gq_ref, wqa_ref, wqb_ref, gkv_ref, tc_ref, ts_ref,
                           wk_ref, wv_ref, ckv_ref, kpe_ref, qkv_ref, z_ref, misc_ref, su_ref,
                           q_ref, k_ref, v_ref):
    _mixprep_body(False, x_ref, g_ref, wbig_ref, gq_ref, wqa_ref, wqb_ref, gkv_ref, tc_ref, ts_ref,
                  wk_ref, wv_ref, ckv_ref, kpe_ref, qkv_ref, z_ref, misc_ref, su_ref, q_ref, k_ref, v_ref)


def _mixprep_sample_kernel(x_ref, g_ref, wbig_ref, gq_ref, wqa_ref, wqb_ref, gkv_ref, tc_ref, ts_ref,
                           wk_ref, ckv_ref, kpe_ref, qkv_ref, z_ref, misc_ref, su_ref, qlat_ref, qpe_ref):
    _mixprep_body(True, x_ref, g_ref, wbig_ref, gq_ref, wqa_ref, wqb_ref, gkv_ref, tc_ref, ts_ref,
                  wk_ref, None, ckv_ref, kpe_ref, qkv_ref, z_ref, misc_ref, su_ref, qlat_ref, qpe_ref, None)


def _mixprep(sample, x, w, tc, ts):
    rows, d = x.shape
    tm = min(512, rows)
    row = lambda n: pl.BlockSpec((tm, n), lambda i: (i, 0))
    full = lambda a: pl.BlockSpec(a.shape, lambda i: (0,) * a.ndim)
    hrow = lambda n: pl.BlockSpec((MLA_HEADS, tm, n), lambda i: (0, i, 0))
    common_in = [x, w["mix_g_pre"], w["w_big"], w["mla_g_q"], w["wq_a"], w["wq_b"], w["mla_g_kv"], tc, ts]
    common_specs = [row(d)] + [full(a) for a in common_in[1:7]] + [row(HEAD_BLOCK), row(HEAD_BLOCK)]
    common_out = [
        (jax.ShapeDtypeStruct((rows, MLA_KV_RANK), F32), row(MLA_KV_RANK)),
        (jax.ShapeDtypeStruct((rows, MLA_ROPE), F32), row(MLA_ROPE)),
        (jax.ShapeDtypeStruct((rows, GDN_QKV), F32), row(GDN_QKV)),
        (jax.ShapeDtypeStruct((rows, GDN_HEADS * GDN_DV), F32), row(GDN_HEADS * GDN_DV)),
        (jax.ShapeDtypeStruct((rows, HEAD_BLOCK), F32), row(HEAD_BLOCK)),
        (jax.ShapeDtypeStruct((rows, S5_WIDTH), F32), row(S5_WIDTH)),
    ]
    if sample:
        ins = common_in + [w["wuk_t"]]
        specs = common_specs + [full(w["wuk_t"])]
        outs = common_out + [
            (jax.ShapeDtypeStruct((MLA_HEADS, rows, MLA_KV_RANK), F32), hrow(MLA_KV_RANK)),
            (jax.ShapeDtypeStruct((MLA_HEADS, rows, MLA_ROPE), F32), hrow(MLA_ROPE)),
        ]
        body = _mixprep_sample_kernel
    else:
        ins = common_in + [w["wuk_pad"], w["wuv"]]
        specs = common_specs + [full(w["wuk_pad"]), full(w["wuv"])]
        outs = common_out + [
            (jax.ShapeDtypeStruct((MLA_HEADS, rows, HEAD_BLOCK), BF16), hrow(HEAD_BLOCK)),
            (jax.ShapeDtypeStruct((MLA_HEADS, rows, HEAD_BLOCK), BF16), hrow(HEAD_BLOCK)),
            (jax.ShapeDtypeStruct((MLA_HEADS * MLA_V, rows), BF16),
             pl.BlockSpec((MLA_HEADS * MLA_V, tm), lambda i: (0, i))),
        ]
        body = _mixprep_prompt_kernel
    return pl.pallas_call(
        body,
        grid=(rows // tm,),
        in_specs=specs,
        out_specs=[o[1] for o in outs],
        out_shape=[o[0] for o in outs],
        compiler_params=_cparams("parallel"),
        name="mixprep_sample" if sample else "mixprep_prompt",
    )(*ins)


def _attn_kernel(qi_ref, ki_ref, q_ref, k_ref, vt_ref, o_ref, m_ref, l_ref, acc_ref):
    p = pl.program_id(1)
    i = qi_ref[p]
    j = ki_ref[p]
    tq, tk = q_ref.shape[1], k_ref.shape[1]

    @pl.when(j == 0)
    def _():
        m_ref[...] = jnp.full(m_ref.shape, -jnp.inf, F32)
        l_ref[...] = jnp.zeros_like(l_ref)
        acc_ref[...] = jnp.zeros_like(acc_ref)

    sub = min(ATTN_SUB, tk)

    def step(masked):
        hrows = [slice(hh * MLA_V, (hh + 1) * MLA_V) for hh in range(ATTN_HEADS)]
        m = [m_ref[hh] for hh in range(ATTN_HEADS)]
        l = [l_ref[hh] for hh in range(ATTN_HEADS)]
        acc = [acc_ref[hrows[hh], :] for hh in range(ATTN_HEADS)]
        units = [(c, hh) for c in range(tk // sub) for hh in range(ATTN_HEADS)]

        def scores(c, hh):
            return lax.dot_general(k_ref[hh, c * sub:(c + 1) * sub, :], q_ref[hh], (((1,), (1,)), ((), ())),
                                   preferred_element_type=F32)

        st_next = scores(*units[0])
        for n, (c, hh) in enumerate(units):
            st = st_next
            if n + 1 < len(units):
                st_next = scores(*units[n + 1])
            if masked:
                key = lax.broadcasted_iota(jnp.int32, (sub, tq), 0) + c * sub
                qry = lax.broadcasted_iota(jnp.int32, (sub, tq), 1)
                st = jnp.where(key <= qry, st, -jnp.inf)
            m_new = jnp.maximum(m[hh], jnp.max(st, axis=0, keepdims=True))
            alpha = jnp.exp2(m[hh] - m_new)
            pt = jnp.exp2(st - m_new)
            l[hh] = alpha * l[hh] + jnp.sum(pt, axis=0, keepdims=True)
            acc[hh] = alpha * acc[hh] + jnp.dot(vt_ref[hrows[hh], c * sub:(c + 1) * sub], pt.astype(BF16),
                                                preferred_element_type=F32)
            m[hh] = m_new
        for hh in range(ATTN_HEADS):
            m_ref[hh] = m[hh]
            l_ref[hh] = l[hh]
            acc_ref[hrows[hh], :] = acc[hh]

    @pl.when(j < i)
    def _():
        step(False)

    @pl.when(j == i)
    def _():
        step(True)
        inv = jnp.concatenate([jnp.broadcast_to(1.0 / l_ref[hh], (MLA_V, tq)) for hh in range(ATTN_HEADS)], axis=0)
        o_ref[...] = (acc_ref[...] * inv).T.astype(o_ref.dtype)


def _prompt_attention(q, k, vt):
    rows = q.shape[1]
    t = min(ATTN_TILE, rows)
    n = rows // t
    qi = np.array([i for i in range(n) for _ in range(i + 1)], np.int32)
    ki = np.array([j for i in range(n) for j in range(i + 1)], np.int32)
    grid_spec = pltpu.PrefetchScalarGridSpec(
        num_scalar_prefetch=2,
        grid=(MLA_HEADS // ATTN_HEADS, len(qi)),
        in_specs=[
            pl.BlockSpec((ATTN_HEADS, t, HEAD_BLOCK), lambda hp, p, qi, ki: (hp, qi[p], 0)),
            pl.BlockSpec((ATTN_HEADS, t, HEAD_BLOCK), lambda hp, p, qi, ki: (hp, ki[p], 0)),
            pl.BlockSpec((ATTN_HEADS * MLA_V, t), lambda hp, p, qi, ki: (hp, ki[p])),
        ],
        out_specs=pl.BlockSpec((t, ATTN_HEADS * MLA_V), lambda hp, p, qi, ki: (qi[p], hp)),
        scratch_shapes=[pltpu.VMEM((ATTN_HEADS, 1, t), F32), pltpu.VMEM((ATTN_HEADS, 1, t), F32),
                        pltpu.VMEM((ATTN_HEADS * MLA_V, t), F32)],
    )
    return pl.pallas_call(
        _attn_kernel,
        grid_spec=grid_spec,
        out_shape=jax.ShapeDtypeStruct((rows, MLA_HEADS * MLA_V), BF16),
        compiler_params=_cparams("parallel", "arbitrary"),
        name="mla_prompt_attention",
    )(jnp.asarray(qi), jnp.asarray(ki), q, k, vt)


def _paged_kernel(layer, n_pp, n_steps, n_total, pt_ref, qlat_ref, qpe_ref, cnew_ref, knew_ref, ckv_hbm, kpe_hbm,
                  o_ref, m_ref, l_ref, acc_ref, ckv_buf, kpe_buf, ckv_all_ref, kpe_all_ref, sem):
    j = pl.program_id(1)
    step = pl.program_id(0) * n_steps + j
    slot = lax.rem(step, PAGE_SLOTS)
    ahead = lax.rem(step + PAGE_AHEAD, n_total)
    ahead_slot = lax.rem(step + PAGE_AHEAD, PAGE_SLOTS)
    t_new = cnew_ref.shape[0]
    rows = MLA_HEADS * t_new

    n_grp = max(1, n_pp // PAGE_GROUP)
    per = n_pp // n_grp

    def page_copies(s, dst_slot, k):
        page = pt_ref[s * n_pp + k]
        return (pltpu.make_async_copy(ckv_hbm.at[layer, page], ckv_buf.at[dst_slot, k], sem.at[0, dst_slot]),
                pltpu.make_async_copy(kpe_hbm.at[layer, page], kpe_buf.at[dst_slot, k], sem.at[1, dst_slot]))

    def start_page(s, dst_slot, k):
        for cp in page_copies(s, dst_slot, k):
            cp.start()

    def wait_step(s, dst_slot):
        for k in range(n_pp):
            for cp in page_copies(s, dst_slot, k):
                cp.wait()

    @pl.when(step == 0)
    def _():
        for s in range(PAGE_AHEAD):
            for k in range(n_pp):
                start_page(s, s, k)

    ql = qlat_ref[...].reshape(rows, MLA_KV_RANK).astype(BF16)
    qp = qpe_ref[...].reshape(rows, MLA_ROPE).astype(BF16)

    @pl.when(j == 0)
    def _():
        cn = cnew_ref[...].astype(BF16)
        s = _bdot_nt(ql, cn) + _bdot_nt(qp, knew_ref[...])
        tok = lax.broadcasted_iota(jnp.int32, (rows, t_new), 0) % t_new
        key = lax.broadcasted_iota(jnp.int32, (rows, t_new), 1)
        s = jnp.where(key <= tok, s, -jnp.inf)
        m = jnp.max(s, axis=-1, keepdims=True)
        pm = jnp.exp(s - m)
        m_ref[...] = jnp.broadcast_to(m, m_ref.shape)
        l_ref[...] = jnp.broadcast_to(jnp.sum(pm, axis=-1, keepdims=True), l_ref.shape)
        acc_ref[...] = jnp.dot(pm.astype(BF16), cn, preferred_element_type=F32)

    wait_step(step, slot)
    gkeys = per * PAGE_SIZE

    def scores(g):
        for k in range(g * per, (g + 1) * per):
            start_page(ahead, ahead_slot, k)
            ckv_all_ref[k * PAGE_SIZE:(k + 1) * PAGE_SIZE, :] = ckv_buf[slot, k].astype(BF16)
            kpe_all_ref[:, k * PAGE_SIZE:(k + 1) * PAGE_SIZE] = kpe_buf[slot, k].astype(BF16)
        cg = ckv_all_ref[g * gkeys:(g + 1) * gkeys, :]
        kg = kpe_all_ref[:, g * gkeys:(g + 1) * gkeys]
        return _bdot_nt(ql, cg) + jnp.dot(qp, kg, preferred_element_type=F32), cg

    m = m_ref[...]
    l = l_ref[...]
    acc = acc_ref[...]
    nxt = scores(0)
    for g in range(n_grp):
        s, cg = nxt
        if g + 1 < n_grp:
            nxt = scores(g + 1)
        m_new = jnp.maximum(m, jnp.max(s, axis=-1, keepdims=True))
        alpha = jnp.exp(m - m_new)
        pm = jnp.exp(s - m_new[:, :1])
        l = alpha * l + jnp.sum(pm, axis=-1, keepdims=True)
        acc = alpha[:, :1] * acc + jnp.dot(pm.astype(BF16), cg, preferred_element_type=F32)
        m = m_new
    m_ref[...] = m
    l_ref[...] = l
    acc_ref[...] = acc

    @pl.when(j == n_steps - 1)
    def _():
        o = acc_ref[...] / l_ref[...][:, :1]
        o_ref[...] = o.reshape(MLA_HEADS, t_new, MLA_KV_RANK)

    @pl.when(step == n_total - 1)
    def _():
        for s in range(PAGE_AHEAD):
            wait_step(s, (n_total + s) % PAGE_SLOTS)


def _paged_attention(layer, qlat, qpe, ckv_new, kpe_new, cache_ckv, cache_kpe_t, page_table, t_new):
    n_b, n_pages = page_table.shape
    n_pp = min(PAGES_PER_STEP, n_pages)
    n_steps = n_pages // n_pp
    assert n_b * n_steps >= PAGE_AHEAD, "the page prefetch chain needs at least PAGE_AHEAD grid steps"
    rows = MLA_HEADS * t_new
    grid_spec = pltpu.PrefetchScalarGridSpec(
        num_scalar_prefetch=1,
        grid=(n_b, n_steps),
        in_specs=[
            pl.BlockSpec((MLA_HEADS, t_new, MLA_KV_RANK), lambda b, j, pt: (0, b, 0)),
            pl.BlockSpec((MLA_HEADS, t_new, MLA_ROPE), lambda b, j, pt: (0, b, 0)),
            pl.BlockSpec((t_new, MLA_KV_RANK), lambda b, j, pt: (b, 0)),
            pl.BlockSpec((t_new, MLA_ROPE), lambda b, j, pt: (b, 0)),
            pl.BlockSpec(memory_space=pl.ANY),
            pl.BlockSpec(memory_space=pl.ANY),
        ],
        out_specs=pl.BlockSpec((MLA_HEADS, t_new, MLA_KV_RANK), lambda b, j, pt: (0, b, 0)),
        scratch_shapes=[pltpu.VMEM((rows, HEAD_BLOCK), F32), pltpu.VMEM((rows, HEAD_BLOCK), F32),
                        pltpu.VMEM((rows, MLA_KV_RANK), F32),
                        pltpu.VMEM((PAGE_SLOTS, n_pp, PAGE_SIZE, MLA_KV_RANK), F32),
                        pltpu.VMEM((PAGE_SLOTS, n_pp, MLA_ROPE, PAGE_SIZE), F32),
                        pltpu.VMEM((n_pp * PAGE_SIZE, MLA_KV_RANK), BF16),
                        pltpu.VMEM((MLA_ROPE, n_pp * PAGE_SIZE), BF16),
                        pltpu.SemaphoreType.DMA((2, PAGE_SLOTS))],
    )
    return pl.pallas_call(
        functools.partial(_paged_kernel, layer, n_pp, n_steps, n_b * n_steps),
        grid_spec=grid_spec,
        out_shape=jax.ShapeDtypeStruct((MLA_HEADS, n_b * t_new, MLA_KV_RANK), F32),
        compiler_params=_cparams("arbitrary", "arbitrary"),
        name="mla_paged_attention",
    )(page_table.reshape(-1), qlat, qpe, ckv_new, kpe_new, cache_ckv, cache_kpe_t)


def _gdn_gates(misc, alog, dtb):
    beta = jax.nn.sigmoid(misc)
    g = -jnp.exp(alog) * _softplus(misc + dtb)
    return beta, g


def _l2n(x, scale):
    return x * (lax.rsqrt(jnp.sum(x * x, axis=-1, keepdims=True) + 1e-6) * scale)


def _gdn_prompt_kernel(qkv_ref, z_ref, misc_ref, convw_ref, alog_ref, dtb_ref, gnorm_ref,
                       o_ref, sfin_ref, s_ref, carry_ref):
    i = pl.program_id(0)

    @pl.when(i == 0)
    def _():
        s_ref[...] = jnp.zeros_like(s_ref)
        carry_ref[...] = jnp.zeros_like(carry_ref)

    x = qkv_ref[...]
    tm = x.shape[0]
    w = convw_ref[...]
    row8 = lax.broadcasted_iota(jnp.int32, (8, GDN_QKV), 0)
    cprev = carry_ref[...]
    acc = x * w[GDN_CONV - 1:GDN_CONV]
    for d in range(1, GDN_CONV):
        xr = pltpu.roll(x, d, 0)
        head = jnp.where(row8 < d, pltpu.roll(cprev, d, 0), xr[0:8])
        xs = jnp.concatenate([head, xr[8:]], axis=0)
        acc = acc + xs * w[GDN_CONV - 1 - d:GDN_CONV - d]
    carry_ref[...] = x[tm - 8:tm]
    conv = _silu(acc)

    nq = GDN_HEADS * GDN_DK
    beta_all, g_all = _gdn_gates(misc_ref[...], alog_ref[...], dtb_ref[...])
    z = z_ref[...]
    gnorm = gnorm_ref[...]
    c = GDN_CHUNK
    blk = 2 * c
    ri = lax.broadcasted_iota(jnp.int32, (blk, blk), 0)
    ci = lax.broadcasted_iota(jnp.int32, (blk, blk), 1)
    tri2 = jnp.where((ri >= ci) & ((ri // c) == (ci // c)), 1.0, 0.0).astype(BF16)
    ns = GDN_HEADS * c
    rs_ = lax.broadcasted_iota(jnp.int32, (ns, ns), 0)
    cs_ = lax.broadcasted_iota(jnp.int32, (ns, ns), 1)
    same = (rs_ // c) == (cs_ // c)
    incl = same & (rs_ >= cs_)
    strict = same & (rs_ > cs_)
    heads = range(GDN_HEADS)
    stack = lambda parts: jnp.concatenate(parts, axis=0)

    solved = []
    for b2 in range(tm // blk):
        gblk = g_all[b2 * blk:(b2 + 1) * blk]
        g0, g1, g2 = _split3(gblk)
        d = functools.partial(jnp.dot, preferred_element_type=F32)
        gcum = d(tri2, g0) + d(tri2, g1) + d(tri2, g2)
        gcum_t = gcum.T
        for c2 in range(2):
            r0 = b2 * blk + c2 * c
            cr = slice(c2 * c, (c2 + 1) * c)
            qs = [_l2n(conv[r0:r0 + c, h * GDN_DK:(h + 1) * GDN_DK], GDN_DK ** -0.5) for h in heads]
            ks = [_l2n(conv[r0:r0 + c, nq + h * GDN_DK:nq + (h + 1) * GDN_DK], 1.0) for h in heads]
            vs = [conv[r0:r0 + c, 2 * nq + h * GDN_DV:2 * nq + (h + 1) * GDN_DV] for h in heads]
            gcs = [gcum[cr, GDN_HEADS + h:GDN_HEADS + h + 1] for h in heads]
            q_st, k_st, v_st, gc_st = stack(qs), stack(ks), stack(vs), stack(gcs)
            gr_st = jnp.concatenate([gcum_t[GDN_HEADS + h:GDN_HEADS + h + 1, cr] for h in heads], axis=1)
            beta_st = stack([beta_all[r0:r0 + c, h:h + 1] for h in heads])
            decay = jnp.exp(jnp.where(incl, gc_st - gr_st, -jnp.inf))
            eg_st = jnp.exp(gc_st)
            kk = _bdot_nt(k_st, k_st)
            mk = -jnp.where(strict, beta_st * kk * decay, 0.0)
            rhs = jnp.concatenate([v_st * beta_st, k_st * (beta_st * eg_st)], axis=-1)
            qk = _bdot_nt(q_st, k_st) * decay
            solved.append([r0, rhs, qk, qs, ks, gcs, eg_st, mk])

    n_sq = int(math.log2(c))
    for kq in range(n_sq):
        for item in solved:
            item[1] = item[1] + _dot_split(item[7], item[1])
            if kq < n_sq - 1:
                item[7] = _dot_split(item[7], item[7])

    state = [s_ref[h] for h in heads]
    for r0, rhs, qk, qs, ks, gcs, eg_st, _ in solved:
        outs = []
        for h in heads:
            hr = slice(h * c, (h + 1) * c)
            uu = rhs[hr, :GDN_DV]
            ww = rhs[hr, GDN_DV:]
            gc = gcs[h]
            glast = gc[c - 1:c, :]
            s_h = state[h]
            v_new = uu - _bdot(ww, s_h)
            o_h = _bdot(qs[h] * eg_st[hr], s_h) + _bdot(qk[hr, h * c:(h + 1) * c], v_new)
            kdec = ks[h] * jnp.exp(glast - gc)
            state[h] = s_h * jnp.exp(glast) + lax.dot_general(
                kdec.astype(BF16), v_new.astype(BF16), (((0,), (0,)), ((), ())),
                preferred_element_type=F32)
            zh = z[r0:r0 + c, h * GDN_DV:(h + 1) * GDN_DV]
            outs.append(_rms(o_h, gnorm) * _silu(zh))
        o_ref[r0:r0 + c, :] = jnp.concatenate(outs, axis=-1).astype(o_ref.dtype)
    for h in heads:
        s_ref[h] = state[h]

    @pl.when(i == pl.num_programs(0) - 1)
    def _():
        sfin_ref[...] = s_ref[...]


def _gdn_prompt(qkv, z, misc, w):
    rows = qkv.shape[0]
    tm = min(256, rows)
    row = lambda n: pl.BlockSpec((tm, n), lambda i: (i, 0))
    full = lambda a: pl.BlockSpec(a.shape, lambda i: (0,) * a.ndim)
    st = (GDN_HEADS, GDN_DK, GDN_DV)
    ins = [qkv, z, misc, w["gdn_conv_w"], w["gdn_alog_pad"], w["gdn_dtb_pad"], w["gdn_g_norm"]]
    return pl.pallas_call(
        _gdn_prompt_kernel,
        grid=(rows // tm,),
        in_specs=[row(GDN_QKV), row(GDN_HEADS * GDN_DV), row(HEAD_BLOCK)] + [full(a) for a in ins[3:]],
        out_specs=[row(GDN_HEADS * GDN_DV), pl.BlockSpec(st, lambda i: (0, 0, 0))],
        out_shape=[jax.ShapeDtypeStruct((rows, GDN_HEADS * GDN_DV), BF16), jax.ShapeDtypeStruct(st, F32)],
        scratch_shapes=[pltpu.VMEM(st, F32), pltpu.VMEM((8, GDN_QKV), F32)],
        compiler_params=_cparams("arbitrary"),
        name="gdn_prompt",
    )(*ins)


def _gdn_sample_kernel(t_new, qkv_ref, st_ref, z_ref, misc_ref, convw_ref, alog_ref, dtb_ref, gcol_ref,
                       s0_ref, o_ref, s_ref,
                       rows_ref, zo_ref, gate_ref, qt_ref, kt_ref, vt_ref, gt_ref, ot_ref):
    n_rows = qkv_ref.shape[0]
    n_b = n_rows // t_new
    nq = GDN_HEADS * GDN_DK
    x = qkv_ref[...]
    st = st_ref[...]
    w = convw_ref[...]
    tpos = lax.broadcasted_iota(jnp.int32, (n_rows, GDN_QKV), 0) % t_new
    acc = x * w[GDN_CONV - 1:GDN_CONV]
    for d in range(1, GDN_CONV):
        xr = pltpu.roll(x, d, 0)
        back = GDN_CONV - 1 - d
        sr = st if back == 0 else pltpu.roll(st, n_rows - back, 0)
        acc = acc + jnp.where(tpos < d, sr, xr) * w[GDN_CONV - 1 - d:GDN_CONV - d]
    conv = _silu(acc)
    parts = []
    for h in range(GDN_HEADS):
        parts.append(_l2n(conv[:, h * GDN_DK:(h + 1) * GDN_DK], GDN_DK ** -0.5))
    for h in range(GDN_HEADS):
        parts.append(_l2n(conv[:, nq + h * GDN_DK:nq + (h + 1) * GDN_DK], 1.0))
    parts.append(conv[:, 2 * nq:])
    feats = jnp.concatenate(parts, axis=-1)
    n_chunk = GDN_QKV // HEAD_BLOCK
    per_part = nq // HEAD_BLOCK
    for cc in range(n_chunk):
        rows_ref[cc] = feats[:, cc * HEAD_BLOCK:(cc + 1) * HEAD_BLOCK]
    zsil = _silu(z_ref[...])
    for cc in range(per_part):
        zo_ref[cc] = zsil[:, cc * HEAD_BLOCK:(cc + 1) * HEAD_BLOCK]
    beta_all, g_all = _gdn_gates(misc_ref[...], alog_ref[...], dtb_ref[...])
    lane = lax.broadcasted_iota(jnp.int32, beta_all.shape, 1)
    gate_ref[...] = jnp.where(lane < GDN_HEADS, beta_all, jnp.exp(g_all))

    for t in range(t_new):
        for cc in range(n_chunk):
            blk_t = rows_ref[cc, pl.ds(t, n_b, stride=t_new), :].T
            dst = (qt_ref, kt_ref, vt_ref)[cc // per_part]
            lo = (cc % per_part) * HEAD_BLOCK
            dst[t, lo:lo + HEAD_BLOCK, :] = blk_t
        gt_ref[t] = gate_ref[pl.ds(t, n_b, stride=t_new), :].T

    s_ref[...] = s0_ref[...]

    gcol = gcol_ref[...]
    for t in range(t_new):
        for h in range(GDN_HEADS):
            egr = gt_ref[t, GDN_HEADS + h:GDN_HEADS + h + 1, :]
            betar = gt_ref[t, h:h + 1, :]
            base = h * GDN_DK

            def p1(dk, racc):
                kb = kt_ref[t, pl.ds(base + dk, 1), :]
                return racc + s_ref[base + dk] * kb

            rr = lax.fori_loop(0, GDN_DK, p1, jnp.zeros((GDN_DV, n_b), F32), unroll=8) * egr
            dd = betar * (vt_ref[t, base:base + GDN_DV, :] - rr)

            def p2(dk, oacc):
                kb = kt_ref[t, pl.ds(base + dk, 1), :]
                qb = qt_ref[t, pl.ds(base + dk, 1), :]
                sn = s_ref[base + dk] * egr + kb * dd
                s_ref[base + dk] = sn
                return oacc + sn * qb

            oo = lax.fori_loop(0, GDN_DK, p2, jnp.zeros((GDN_DV, n_b), F32), unroll=8)
            on = oo * lax.rsqrt(jnp.mean(oo * oo, axis=0, keepdims=True) + RMS_EPS) * gcol
            ot_ref[base:base + GDN_DV, :] = on
        on_rows = ot_ref[...].T
        for cc in range(per_part):
            zt = zo_ref[cc, pl.ds(t, n_b, stride=t_new), :]
            zo_ref[cc, pl.ds(t, n_b, stride=t_new), :] = on_rows[:, cc * HEAD_BLOCK:(cc + 1) * HEAD_BLOCK] * zt
    for cc in range(per_part):
        o_ref[:, cc * HEAD_BLOCK:(cc + 1) * HEAD_BLOCK] = zo_ref[cc]


def _gdn_sample(qkv, st_rows, z, misc, w, s0, t_new):
    rows = qkv.shape[0]
    n_b = rows // t_new
    feat = GDN_HEADS * GDN_DK
    ins = [qkv, st_rows, z, misc, w["gdn_conv_w"], w["gdn_alog_pad"], w["gdn_dtb_pad"], w["gdn_g_col"], s0]
    return pl.pallas_call(
        functools.partial(_gdn_sample_kernel, t_new),
        out_shape=[jax.ShapeDtypeStruct((rows, GDN_HEADS * GDN_DV), F32),
                   jax.ShapeDtypeStruct((GDN_HEADS * GDN_DK, GDN_DV, n_b), F32)],
        scratch_shapes=[
            pltpu.VMEM((GDN_QKV // HEAD_BLOCK, rows, HEAD_BLOCK), F32),
            pltpu.VMEM((GDN_HEADS * GDN_DV // HEAD_BLOCK, rows, HEAD_BLOCK), F32),
            pltpu.VMEM((rows, HEAD_BLOCK), F32),
            pltpu.VMEM((t_new, feat, n_b), F32),
            pltpu.VMEM((t_new, feat, n_b), F32),
            pltpu.VMEM((t_new, GDN_HEADS * GDN_DV, n_b), F32),
            pltpu.VMEM((t_new, HEAD_BLOCK, n_b), F32),
            pltpu.VMEM((GDN_HEADS * GDN_DV, n_b), F32),
        ],
        compiler_params=pltpu.CompilerParams(vmem_limit_bytes=VMEM_LIMIT_BYTES),
        name="gdn_sample",
    )(*ins)


def _cmul(ar, ai, br, bi):
    return ar * br - ai * bi, ar * bi + ai * br


def _s5_kernel(per_group_state, u_ref, bbr_ref, bbi_ref, ar_ref, ai_ref, cr_ref, ci_ref, d_ref,
               wglu_ref, bglu_ref, h0r_ref, h0i_ref, o_ref, hr_out_ref, hi_out_ref,
               br_ref, bi_ref, cr_carry_ref, ci_carry_ref):
    i = pl.program_id(0)
    tm = u_ref.shape[0]
    n_grp = tm // 8

    if not per_group_state:
        @pl.when(i == 0)
        def _():
            cr_carry_ref[...] = jnp.zeros_like(cr_carry_ref)
            ci_carry_ref[...] = jnp.zeros_like(ci_carry_ref)

    u = u_ref[...]
    ub = u.astype(BF16)
    br_ref[...] = jnp.dot(ub, bbr_ref[...], preferred_element_type=F32)
    bi_ref[...] = jnp.dot(ub, bbi_ref[...], preferred_element_type=F32)

    ar = ar_ref[...]
    ai = ai_ref[...]
    p1 = (ar, ai)
    p2 = _cmul(*p1, *p1)
    p3 = _cmul(*p2, *p1)
    p4 = _cmul(*p2, *p2)
    p5 = _cmul(*p4, *p1)
    p6 = _cmul(*p4, *p2)
    p7 = _cmul(*p4, *p3)
    p8 = _cmul(*p4, *p4)
    row8 = lax.broadcasted_iota(jnp.int32, (8, S5_LANES), 0)
    pw_r = jnp.zeros((8, S5_LANES), F32)
    pw_i = jnp.zeros((8, S5_LANES), F32)
    for t, pw in enumerate((p1, p2, p3, p4, p5, p6, p7, p8)):
        pw_r = jnp.where(row8 == t, pw[0], pw_r)
        pw_i = jnp.where(row8 == t, pw[1], pw_i)

    steps = [(d, (jnp.where(row8 >= d, pw[0], 0.0), jnp.where(row8 >= d, pw[1], 0.0)))
             for d, pw in ((1, p1), (2, p2), (4, p4))]

    def body(gi, carry):
        r0 = pl.multiple_of(gi * 8, 8)
        a = br_ref[pl.ds(r0, 8), :]
        b = bi_ref[pl.ds(r0, 8), :]
        for d, pw in steps:
            da, db = _cmul(pw[0], pw[1], pltpu.roll(a, d, 0), pltpu.roll(b, d, 0))
            a = a + da
            b = b + db
        if per_group_state:
            c_r = h0r_ref[pl.ds(gi, 1), :]
            c_i = h0i_ref[pl.ds(gi, 1), :]
        else:
            c_r, c_i = carry
        da, db = _cmul(pw_r, pw_i, c_r, c_i)
        a = a + da
        b = b + db
        br_ref[pl.ds(r0, 8), :] = a
        bi_ref[pl.ds(r0, 8), :] = b
        if per_group_state:
            hr_out_ref[pl.ds(gi, 1), :] = a[7:8]
            hi_out_ref[pl.ds(gi, 1), :] = b[7:8]
            return carry
        return a[7:8], b[7:8]

    if per_group_state:
        zero = jnp.zeros((1, S5_LANES), F32)
        lax.fori_loop(0, n_grp, body, (zero, zero))
    else:
        c_fin = lax.fori_loop(0, n_grp, body, (cr_carry_ref[...], ci_carry_ref[...]))
        cr_carry_ref[...] = c_fin[0]
        ci_carry_ref[...] = c_fin[1]
        hr_out_ref[...] = c_fin[0]
        hi_out_ref[...] = c_fin[1]

    y = (jnp.dot(br_ref[...].astype(BF16), cr_ref[...], preferred_element_type=F32)
         - jnp.dot(bi_ref[...].astype(BF16), ci_ref[...], preferred_element_type=F32)
         + d_ref[...] * u)
    zg = jax.nn.gelu(y)
    gate = jax.nn.sigmoid(jnp.dot(zg.astype(BF16), wglu_ref[...], preferred_element_type=F32) + bglu_ref[...])
    o_ref[...] = (zg * gate).astype(o_ref.dtype)


def _s5(su, w, h0r, h0i, per_group_state, out_dtype):
    rows = su.shape[0]
    tm = min(512, rows)
    n_grp = tm // 8
    row = lambda n: pl.BlockSpec((tm, n), lambda i: (i, 0))
    full = lambda a: pl.BlockSpec(a.shape, lambda i: (0,) * a.ndim)
    wlist = [w["s5_bb_re"], w["s5_bb_im"], w["s5_ab_re"], w["s5_ab_im"], w["s5_c_re"], w["s5_c_im"],
             w["s5_d"], w["s5_w_glu"], w["s5_b_glu"]]
    if per_group_state:
        st_spec = pl.BlockSpec((n_grp, S5_LANES), lambda i: (i, 0))
        st_shape = jax.ShapeDtypeStruct((rows // 8, S5_LANES), F32)
    else:
        st_spec = pl.BlockSpec((1, S5_LANES), lambda i: (0, 0))
        st_shape = jax.ShapeDtypeStruct((1, S5_LANES), F32)
    return pl.pallas_call(
        functools.partial(_s5_kernel, per_group_state),
        grid=(rows // tm,),
        in_specs=[row(S5_WIDTH)] + [full(a) for a in wlist] + [st_spec, st_spec],
        out_specs=[row(S5_WIDTH), st_spec, st_spec],
        out_shape=[jax.ShapeDtypeStruct((rows, S5_WIDTH), out_dtype), st_shape, st_shape],
        scratch_shapes=[pltpu.VMEM((tm, S5_LANES), F32), pltpu.VMEM((tm, S5_LANES), F32),
                        pltpu.VMEM((1, S5_LANES), F32), pltpu.VMEM((1, S5_LANES), F32)],
        compiler_params=_cparams("arbitrary"),
        name="s5_sample" if per_group_state else "s5_prompt",
    )(su, *wlist, h0r, h0i)


def _mixout_kernel(x_ref, oa_ref, og_ref, os_ref, wuv_ref, woa_ref, wob_ref, woc_ref,
                   gpost_ref, gxa_ref, wq_ref, x2_ref, q_ref):
    mixed = None
    for h in range(MLA_HEADS):
        o_h = jnp.dot(oa_ref[h].astype(BF16), wuv_ref[h], preferred_element_type=F32)
        t = jnp.dot(o_h.astype(BF16), woa_ref[h * MLA_V:(h + 1) * MLA_V, :], preferred_element_type=F32)
        mixed = t if mixed is None else mixed + t
    mixed = mixed + jnp.dot(og_ref[...].astype(BF16), wob_ref[...], preferred_element_type=F32)
    mixed = mixed + jnp.dot(os_ref[...].astype(BF16), woc_ref[...], preferred_element_type=F32)
    x2 = x_ref[...] + _rms(mixed, gpost_ref[...])
    x2_ref[...] = x2
    hq = _rms(x2, gxa_ref[...]).astype(BF16)
    q_ref[...] = jnp.dot(hq, wq_ref[...], preferred_element_type=F32) * XA_SCALE


def _mixout(x, o_lat, o_gdn, o_s5, w):
    rows, d = x.shape
    tm = min(512, rows)
    row = lambda n: pl.BlockSpec((tm, n), lambda i: (i, 0))
    full = lambda a: pl.BlockSpec(a.shape, lambda i: (0,) * a.ndim)
    oa_spec = pl.BlockSpec((MLA_HEADS, tm, MLA_KV_RANK), lambda i: (0, i, 0))
    wl = [w["wuv_h"], w["wo_a"], w["wo_b"], w["wo_c"], w["mix_g_post"], w["xa_g_pre"], w["xa_w_q"]]
    return pl.pallas_call(
        _mixout_kernel,
        grid=(rows // tm,),
        in_specs=[row(d), oa_spec, row(o_gdn.shape[1]), row(o_s5.shape[1])] + [full(a) for a in wl],
        out_specs=[row(d), row(d)],
        out_shape=[jax.ShapeDtypeStruct((rows, d), F32), jax.ShapeDtypeStruct((rows, d), F32)],
        compiler_params=_cparams("parallel"),
        name="mixout_sample",
    )(x, o_lat, o_gdn, o_s5, *wl)


def _postmix_kernel(x_ref, oa_ref, og_ref, os_ref, woa_ref, wob_ref, woc_ref, gpost_ref, gxa_ref, wq_ref,
                    mk_ref, mv_ref, wo_ref, gxo_ref, y_ref):
    mixed = jnp.dot(oa_ref[...], woa_ref[...], preferred_element_type=F32)
    mixed = mixed + jnp.dot(og_ref[...], wob_ref[...], preferred_element_type=F32)
    mixed = mixed + jnp.dot(os_ref[...], woc_ref[...], preferred_element_type=F32)
    x2 = x_ref[...] + _rms(mixed, gpost_ref[...])
    q = jnp.dot(_rms(x2, gxa_ref[...]).astype(BF16), wq_ref[...], preferred_element_type=F32) * XA_SCALE
    outs = []
    for h in range(XA_HEADS):
        hs = slice(h * XA_HEAD_DIM, (h + 1) * XA_HEAD_DIM)
        s = _bdot_nt(q[:, hs], mk_ref[:, hs])
        m = jnp.max(s, axis=-1, keepdims=True)
        pm = jnp.exp(s - m)
        pr = pm / jnp.sum(pm, axis=-1, keepdims=True)
        outs.append(_bdot(pr, mv_ref[:, hs]).astype(BF16))
    xa = jnp.dot(jnp.concatenate(outs, axis=-1), wo_ref[...], preferred_element_type=F32)
    y_ref[...] = x2 + _rms(xa, gxo_ref[...])


def _postmix(x, o_mla, o_gdn, o_s5, w, mem_k, mem_v, layer):
    rows, d = x.shape
    n_mem = mem_k.shape[2]
    tm = min(512, rows)
    row = lambda n: pl.BlockSpec((tm, n), lambda i: (i, 0))
    full = lambda a: pl.BlockSpec(a.shape, lambda i: (0,) * a.ndim)
    mspec = pl.BlockSpec((None, None, n_mem, d), lambda i: (layer, 0, 0, 0))
    w1 = [w["wo_a"], w["wo_b"], w["wo_c"], w["mix_g_post"], w["xa_g_pre"], w["xa_w_q"]]
    w2 = [w["xa_w_o"], w["xa_g_post"]]
    return pl.pallas_call(
        _postmix_kernel,
        grid=(rows // tm,),
        in_specs=([row(d), row(o_mla.shape[1]), row(o_gdn.shape[1]), row(o_s5.shape[1])]
                  + [full(a) for a in w1] + [mspec, mspec] + [full(a) for a in w2]),
        out_specs=row(d),
        out_shape=jax.ShapeDtypeStruct((rows, d), F32),
        compiler_params=_cparams("parallel"),
        name="postmix_prompt",
    )(x, o_mla, o_gdn, o_s5, *w1, mem_k, mem_v, *w2)


def _xattn_heads_merged_kernel(n_b, q_ref, mk_ref, mv_ref, o_ref):
    rows = q_ref.shape[0] // n_b
    n_mem = mk_ref.shape[1]
    for bi in range(n_b):
        q = q_ref[bi * rows:(bi + 1) * rows, :]
        q_all = jnp.concatenate([q[:, h * XA_HEAD_DIM:(h + 1) * XA_HEAD_DIM] for h in range(XA_HEADS)], axis=0)
        k2 = mk_ref[bi].reshape(n_mem * XA_HEADS, XA_HEAD_DIM)
        v2 = mv_ref[bi].reshape(n_mem * XA_HEADS, XA_HEAD_DIM)
        s = _bdot_nt(q_all, k2)
        q_head = lax.broadcasted_iota(jnp.int32, s.shape, 0) // rows
        m_head = lax.broadcasted_iota(jnp.int32, s.shape, 1) % XA_HEADS
        s = jnp.where(q_head == m_head, s, -jnp.inf)
        m = jnp.max(s, axis=-1, keepdims=True)
        pm = jnp.exp(s - m)
        pr = pm / jnp.sum(pm, axis=-1, keepdims=True)
        o_all = _bdot(pr, v2)
        o_ref[bi * rows:(bi + 1) * rows, :] = jnp.concatenate(
            [o_all[h * rows:(h + 1) * rows] for h in range(XA_HEADS)], axis=-1).astype(o_ref.dtype)


def _xattn_sample(q, mem_k, mem_v, layer, t_new):
    rows, d = q.shape
    n_mem = mem_k.shape[2]
    n_b = min(XA_SEQ_PER_STEP, rows // t_new)
    tm = n_b * t_new
    mspec = pl.BlockSpec((None, n_b, n_mem, XA_HEADS, XA_HEAD_DIM), lambda i: (layer, i, 0, 0, 0))
    return pl.pallas_call(
        functools.partial(_xattn_heads_merged_kernel, n_b),
        grid=(rows // tm,),
        in_specs=[pl.BlockSpec((tm, d), lambda i: (i, 0)), mspec, mspec],
        out_specs=pl.BlockSpec((tm, d), lambda i: (i, 0)),
        out_shape=jax.ShapeDtypeStruct((rows, d), BF16),
        compiler_params=_cparams("parallel"),
        name="xattn_sample",
    )(q, mem_k, mem_v)


def _xaout_kernel(x_ref, o_ref, wo_ref, g_ref, y_ref):
    xa = jnp.dot(o_ref[...], wo_ref[...], preferred_element_type=F32)
    y_ref[...] = x_ref[...] + _rms(xa, g_ref[...])


def _xaout(x, o, w):
    rows, d = x.shape
    tm = min(512, rows)
    row = pl.BlockSpec((tm, d), lambda i: (i, 0))
    return pl.pallas_call(
        _xaout_kernel,
        grid=(rows // tm,),
        in_specs=[row, row, pl.BlockSpec((d, d), lambda i: (0, 0)), pl.BlockSpec((1, d), lambda i: (0, 0))],
        out_specs=row,
        out_shape=jax.ShapeDtypeStruct((rows, d), F32),
        compiler_params=_cparams("parallel"),
        name="xattn_out",
    )(x, o, w["xa_w_o"], w["xa_g_post"])


def _memproj_kernel(m_ref, wk_ref, wv_ref, k_ref, v_ref):
    m = m_ref[...].astype(BF16)
    for l in range(wk_ref.shape[0]):
        k_ref[l] = jnp.dot(m, wk_ref[l], preferred_element_type=F32)
        v_ref[l] = jnp.dot(m, wv_ref[l], preferred_element_type=F32)


def _memproj(mem, wk, wv):
    depth = wk.shape[0]
    shp = jax.ShapeDtypeStruct((depth,) + mem.shape, F32)
    return pl.pallas_call(
        _memproj_kernel,
        out_shape=[shp, shp],
        compiler_params=pltpu.CompilerParams(vmem_limit_bytes=VMEM_LIMIT_BYTES),
        name="mem_kv_proj",
    )(mem, wk, wv)


def _layer_weights(l, p):
    d_model = p["w_in"].shape[1]
    w = {}
    r1 = lambda a: a[l].reshape(1, -1).astype(F32)
    for name in ("ffn1", "ffn2"):
        w[name + "_g_pre"] = r1(p[name + "_g_pre"])
        w[name + "_g_post"] = r1(p[name + "_g_post"])
    for name in ("mix_g_pre", "mix_g_post", "mla_g_q", "mla_g_kv", "xa_g_pre", "xa_g_post", "gdn_g_norm",
                 "s5_d", "s5_b_glu"):
        w[name] = r1(p[name])
    w["gdn_g_col"] = p["gdn_g_norm"][l].reshape(-1, 1).astype(F32)

    w_in = p["w_in"][l]
    offs = np.cumsum([0, MLA_Q_RANK, MLA_KV_RANK, MLA_ROPE, GDN_QKV, GDN_HEADS * GDN_DV, GDN_HEADS, GDN_HEADS,
                      S5_WIDTH])
    w_cq, w_ckv, w_kpe, w_qkv, w_z, w_b, w_a, w_su = [w_in[:, offs[i]:offs[i + 1]] for i in range(8)]
    half = MLA_ROPE // 2
    zeros = lambda n: jnp.zeros((d_model, n), w_in.dtype)
    w_kpe_sw = jnp.concatenate([-w_kpe[:, half:], w_kpe[:, :half]], axis=1)
    tail = HEAD_BLOCK - MLA_NOPE - MLA_ROPE
    w_kpl = jnp.concatenate([zeros(MLA_NOPE), w_kpe, zeros(tail)], axis=1)
    w_kpls = jnp.concatenate([zeros(MLA_NOPE), w_kpe_sw, zeros(tail)], axis=1)
    w_misc = jnp.concatenate([w_b, w_a, zeros(HEAD_BLOCK - 2 * GDN_HEADS)], axis=1)
    w["w_big"] = jnp.concatenate([w_cq, w_ckv, w_kpl, w_kpls, w_misc, w_qkv, w_z, w_su], axis=1).astype(BF16)

    w_uq = p["mla_w_uq"][l].reshape(MLA_Q_RANK, MLA_HEADS, MLA_NOPE + MLA_ROPE)
    nope, x1, x2 = w_uq[..., :MLA_NOPE], w_uq[..., MLA_NOPE:MLA_NOPE + half], w_uq[..., MLA_NOPE + half:]
    zq = lambda n: jnp.zeros((MLA_Q_RANK, MLA_HEADS, n), w_uq.dtype)
    w["wq_a"] = jnp.concatenate([nope, x1, x2, zq(tail)], axis=-1).reshape(MLA_Q_RANK, -1).astype(BF16)
    w["wq_b"] = jnp.concatenate([zq(MLA_NOPE), -x2, x1, zq(tail)], axis=-1).reshape(MLA_Q_RANK, -1).astype(BF16)

    w_uk = p["mla_w_uk"][l]
    w_uv = p["mla_w_uv"][l]
    zk = jnp.zeros((MLA_KV_RANK, MLA_HEADS, HEAD_BLOCK - MLA_NOPE), w_uk.dtype)
    w["wuk_pad"] = jnp.concatenate([w_uk, zk], axis=-1).reshape(MLA_KV_RANK, -1).astype(BF16)
    wuk_t = jnp.transpose(w_uk, (1, 2, 0))
    w["wuk_t"] = jnp.concatenate(
        [wuk_t, jnp.zeros((MLA_HEADS, HEAD_BLOCK - MLA_NOPE, MLA_KV_RANK), w_uk.dtype)], axis=1).astype(BF16)
    w["wuv"] = w_uv.reshape(MLA_KV_RANK, -1).T.astype(BF16)
    w["wuv_h"] = jnp.transpose(w_uv, (1, 0, 2)).astype(BF16)

    w_out = p["w_out"][l]
    n_a = MLA_HEADS * MLA_V
    n_b = n_a + GDN_HEADS * GDN_DV
    w["wo_a"] = w_out[:n_a].astype(BF16)
    w["wo_b"] = w_out[n_a:n_b].astype(BF16)
    w["wo_c"] = w_out[n_b:].astype(BF16)
    w["xa_w_q"] = p["xa_w_q"][l].astype(BF16)
    w["xa_w_o"] = p["xa_w_o"][l].astype(BF16)

    w["gdn_conv_w"] = p["gdn_conv_w"][l].astype(F32)
    pad_gate = lambda v: jnp.zeros((1, HEAD_BLOCK), F32).at[0, GDN_HEADS:2 * GDN_HEADS].set(v.astype(F32))
    w["gdn_alog_pad"] = pad_gate(p["gdn_a_log"][l])
    w["gdn_dtb_pad"] = pad_gate(p["gdn_dt_bias"][l])

    a_re, a_im = p["s5_a_re"][l].astype(F32), p["s5_a_im"][l].astype(F32)
    dt = jnp.exp(p["s5_log_dt"][l].astype(F32))[:, None]
    mag = jnp.exp(a_re * dt)
    ab_re, ab_im = mag * jnp.cos(a_im * dt), mag * jnp.sin(a_im * dt)
    den = a_re * a_re + a_im * a_im
    nr, ni = ab_re - 1.0, ab_im
    coef_re = (nr * a_re + ni * a_im) / den
    coef_im = (ni * a_re - nr * a_im) / den
    b_re, b_im = p["s5_b_re"][l].astype(F32), p["s5_b_im"][l].astype(F32)
    bb_re = coef_re[..., None] * b_re - coef_im[..., None] * b_im
    bb_im = coef_re[..., None] * b_im + coef_im[..., None] * b_re
    eye = jnp.eye(S5_GROUPS, dtype=F32)
    bd_in = lambda bb: jnp.einsum("gnp,gh->gphn", bb, eye).reshape(S5_WIDTH, S5_LANES).astype(BF16)
    bd_out = lambda cc: jnp.einsum("gpn,gh->gnhp", cc.astype(F32), eye).reshape(S5_LANES, S5_WIDTH).astype(BF16)
    w["s5_bb_re"], w["s5_bb_im"] = bd_in(bb_re), bd_in(bb_im)
    w["s5_c_re"], w["s5_c_im"] = bd_out(p["s5_c_re"][l]), bd_out(p["s5_c_im"][l])
    w["s5_ab_re"] = ab_re.reshape(1, S5_LANES)
    w["s5_ab_im"] = ab_im.reshape(1, S5_LANES)
    w["s5_w_glu"] = p["s5_w_glu"][l].astype(BF16)
    return w


def _rope_tables(pos):
    half = MLA_ROPE // 2
    inv = ROPE_THETA ** (-jnp.arange(half, dtype=F32) / half)
    ang = pos.astype(F32)[:, None] * inv[None, :]
    c, s = jnp.cos(ang), jnp.sin(ang)
    n = pos.shape[0]
    z0 = jnp.zeros((n, MLA_NOPE), F32)
    z1 = jnp.zeros((n, HEAD_BLOCK - MLA_NOPE - MLA_ROPE), F32)
    return jnp.concatenate([z0, c, c, z1], axis=1), jnp.concatenate([z0, s, s, z1], axis=1)


def kernel(x_prompt, x_sample, mem_prompt, cache_ckv, cache_kpe, page_table, cache_mem_k, cache_mem_v, state_gdn, state_gdn_conv, state_s5_re, state_s5_im, ffn1_g_pre, ffn1_g_post, ffn1_w_gate, ffn1_w_up, ffn1_w_down, mix_g_pre, mix_g_post, w_in, w_out, mla_g_q, mla_w_uq, mla_g_kv, mla_w_uk, mla_w_uv, gdn_conv_w, gdn_a_log, gdn_dt_bias, gdn_g_norm, s5_a_re, s5_a_im, s5_log_dt, s5_b_re, s5_b_im, s5_c_re, s5_c_im, s5_d, s5_w_glu, s5_b_glu, xa_g_pre, xa_g_post, xa_w_q, xa_w_k, xa_w_v, xa_w_o, ffn2_g_pre, ffn2_g_post, ffn2_w_gate, ffn2_w_up, ffn2_w_down):
    params = dict(
        ffn1_g_pre=ffn1_g_pre, ffn1_g_post=ffn1_g_post, ffn1_w_gate=ffn1_w_gate, ffn1_w_up=ffn1_w_up,
        ffn1_w_down=ffn1_w_down, mix_g_pre=mix_g_pre, mix_g_post=mix_g_post, w_in=w_in, w_out=w_out,
        mla_g_q=mla_g_q, mla_w_uq=mla_w_uq, mla_g_kv=mla_g_kv, mla_w_uk=mla_w_uk, mla_w_uv=mla_w_uv,
        gdn_conv_w=gdn_conv_w, gdn_a_log=gdn_a_log, gdn_dt_bias=gdn_dt_bias, gdn_g_norm=gdn_g_norm,
        s5_a_re=s5_a_re, s5_a_im=s5_a_im, s5_log_dt=s5_log_dt, s5_b_re=s5_b_re, s5_b_im=s5_b_im,
        s5_c_re=s5_c_re, s5_c_im=s5_c_im, s5_d=s5_d, s5_w_glu=s5_w_glu, s5_b_glu=s5_b_glu,
        xa_g_pre=xa_g_pre, xa_g_post=xa_g_post, xa_w_q=xa_w_q, xa_w_o=xa_w_o,
        ffn2_g_pre=ffn2_g_pre, ffn2_g_post=ffn2_g_post, ffn2_w_gate=ffn2_w_gate, ffn2_w_up=ffn2_w_up,
        ffn2_w_down=ffn2_w_down)
    depth = w_in.shape[0]
    bsz, seqlen, d_model = x_prompt.shape
    dec_b, dec_t, _ = x_sample.shape
    n_mem = mem_prompt.shape[1]
    past_len = page_table.shape[1] * PAGE_SIZE
    n_s = dec_b * dec_t

    tc_p, ts_p = _rope_tables(jnp.arange(seqlen))
    tc_s, ts_s = _rope_tables(jnp.tile(past_len + jnp.arange(dec_t), dec_b))

    mem_k, mem_v = _memproj(mem_prompt.reshape(n_mem, d_model), xa_w_k.astype(BF16), xa_w_v.astype(BF16))
    mem_k4 = mem_k.reshape(depth, bsz, n_mem, d_model)
    mem_v4 = mem_v.reshape(depth, bsz, n_mem, d_model)
    cache_kpe_t = jnp.swapaxes(cache_kpe, 2, 3)

    yp = x_prompt.reshape(seqlen, d_model)
    ys = x_sample.reshape(n_s, d_model)
    zero_state = jnp.zeros((1, S5_LANES), F32)
    outs = {k: [] for k in ("p_ckv", "p_kpe", "p_gdn", "p_conv", "p_s5r", "p_s5i",
                            "s_ckv", "s_kpe", "s_gdn", "s_conv", "s_s5r", "s_s5i")}
    weights = [_layer_weights(l, params) for l in range(depth)]
    ffn1_w = [a.astype(BF16) for a in (ffn1_w_gate, ffn1_w_up, ffn1_w_down)]
    ffn2_w = [a.astype(BF16) for a in (ffn2_w_gate, ffn2_w_up, ffn2_w_down)]
    for l, w in enumerate(weights):
        ys = _half_ffn(ys, w["ffn1_g_pre"], w["ffn1_g_post"], *ffn1_w, l)
        ckv, kpe, qkv, z, misc, su, qlat, qpe = _mixprep(True, ys, w, tc_s, ts_s)
        o_lat = _paged_attention(l, qlat, qpe, ckv, kpe, cache_ckv, cache_kpe_t, page_table, dec_t)
        st_rows = jnp.pad(state_gdn_conv[l], ((0, 0), (0, dec_t - (GDN_CONV - 1)), (0, 0))).reshape(n_s, GDN_QKV)
        s0 = jnp.transpose(state_gdn[l], (1, 2, 3, 0)).reshape(GDN_HEADS * GDN_DK, GDN_DV, dec_b)
        o_gdn, gdn_s = _gdn_sample(qkv, st_rows, z, misc, w, s0, dec_t)
        o_s5, h_re, h_im = _s5(su, w, state_s5_re[l].reshape(dec_b, S5_LANES),
                               state_s5_im[l].reshape(dec_b, S5_LANES), True, F32)
        x2, xq = _mixout(ys, o_lat, o_gdn, o_s5, w)
        o_xa = _xattn_sample(xq, cache_mem_k, cache_mem_v, l, dec_t)
        ys = _xaout(x2, o_xa, w)
        ys = _half_ffn(ys, w["ffn2_g_pre"], w["ffn2_g_post"], *ffn2_w, l)
        outs["s_ckv"].append(ckv.reshape(dec_b, dec_t, MLA_KV_RANK))
        outs["s_kpe"].append(kpe.reshape(dec_b, dec_t, MLA_ROPE))
        outs["s_gdn"].append(jnp.transpose(gdn_s.reshape(GDN_HEADS, GDN_DK, GDN_DV, dec_b), (3, 0, 1, 2)))
        outs["s_conv"].append(qkv.reshape(dec_b, dec_t, GDN_QKV)[:, dec_t - (GDN_CONV - 1):])
        outs["s_s5r"].append(h_re.reshape(dec_b, S5_GROUPS, S5_STATE))
        outs["s_s5i"].append(h_im.reshape(dec_b, S5_GROUPS, S5_STATE))

    for l, w in enumerate(weights):
        yp = _half_ffn(yp, w["ffn1_g_pre"], w["ffn1_g_post"], *ffn1_w, l)
        ckv, kpe, qkv, z, misc, su, q, k, v = _mixprep(False, yp, w, tc_p, ts_p)
        o_mla = _prompt_attention(q, k, v)
        o_gdn, gdn_s = _gdn_prompt(qkv, z, misc, w)
        o_s5, h_re, h_im = _s5(su, w, zero_state, zero_state, False, BF16)
        yp = _postmix(yp, o_mla, o_gdn, o_s5, w, mem_k4, mem_v4, l)
        yp = _half_ffn(yp, w["ffn2_g_pre"], w["ffn2_g_post"], *ffn2_w, l)
        outs["p_ckv"].append(ckv.reshape(bsz, seqlen, MLA_KV_RANK))
        outs["p_kpe"].append(kpe.reshape(bsz, seqlen, MLA_ROPE))
        outs["p_gdn"].append(gdn_s.reshape(bsz, GDN_HEADS, GDN_DK, GDN_DV))
        outs["p_conv"].append(qkv[seqlen - (GDN_CONV - 1):].reshape(bsz, GDN_CONV - 1, GDN_QKV))
        outs["p_s5r"].append(h_re.reshape(bsz, S5_GROUPS, S5_STATE))
        outs["p_s5i"].append(h_im.reshape(bsz, S5_GROUPS, S5_STATE))

    st = {k: jnp.stack(v) for k, v in outs.items()}
    p_mem_k = mem_k.reshape(depth, bsz, n_mem, XA_HEADS, XA_HEAD_DIM)
    p_mem_v = mem_v.reshape(depth, bsz, n_mem, XA_HEADS, XA_HEAD_DIM)
    return (yp.reshape(bsz, seqlen, d_model), ys.reshape(dec_b, dec_t, d_model),
            st["p_ckv"], st["p_kpe"], st["p_gdn"], st["p_conv"], st["p_s5r"], st["p_s5i"], p_mem_k, p_mem_v,
            st["s_ckv"], st["s_kpe"], st["s_gdn"], st["s_conv"], st["s_s5r"], st["s_s5i"])
```

```python
import functools
import math

import jax
import jax.numpy as jnp
import numpy as np
from jax import lax
from jax.experimental import pallas as pl
from jax.experimental.pallas import tpu as pltpu

F32 = jnp.float32
BF16 = jnp.bfloat16

RMS_EPS = 1e-6
MLA_HEADS = 8
MLA_NOPE = 64
MLA_ROPE = 32
MLA_V = 64
MLA_Q_RANK = 384
MLA_KV_RANK = 256
ROPE_THETA = 10000.0
HEAD_BLOCK = 128
PAGE_SIZE = 128
FFN_CHUNKS = 4
PAGES_PER_STEP = 64
PAGE_AHEAD = 2
PAGE_SLOTS = PAGE_AHEAD + 1
PAGE_GROUP = 32
ATTN_HEADS = 4
ATTN_TILE = 1024
ATTN_SUB = 512
GDN_HEADS = 4
GDN_DK = 64
GDN_DV = 64
GDN_CONV = 4
GDN_CHUNK = 64
GDN_BLOCKS_PER_STAGE = 2
GDN_QKV = GDN_HEADS * (2 * GDN_DK + GDN_DV)
S5_GROUPS = 16
S5_GROUP = 16
S5_STATE = 64
S5_WIDTH = S5_GROUPS * S5_GROUP
S5_LANES = S5_GROUPS * S5_STATE
XA_HEADS = 4
XA_HEAD_DIM = 256
XA_SEQ_PER_STEP = 4

VMEM_LIMIT_BYTES = 56 * 1024 * 1024

MLA_SCALE = (MLA_NOPE + MLA_ROPE) ** -0.5
LOG2_E = math.log2(math.e)
XA_SCALE = XA_HEAD_DIM ** -0.5


def _cparams(*sem):
    return pltpu.CompilerParams(dimension_semantics=tuple(sem), vmem_limit_bytes=VMEM_LIMIT_BYTES)


def _rms(x, g):
    return x * lax.rsqrt(jnp.mean(x * x, axis=-1, keepdims=True) + RMS_EPS) * g


def _bdot(a, b):
    return jnp.dot(a.astype(BF16), b.astype(BF16), preferred_element_type=F32)


def _bdot_nt(a, b):
    return lax.dot_general(a.astype(BF16), b.astype(BF16), (((1,), (1,)), ((), ())),
                           preferred_element_type=F32)


def _split3(a):
    hi = a.astype(BF16)
    r1 = a - hi.astype(F32)
    mid = r1.astype(BF16)
    lo = (r1 - mid.astype(F32)).astype(BF16)
    return hi, mid, lo


def _dot_split(a, b):
    a0 = a.astype(BF16)
    a1 = (a - a0.astype(F32)).astype(BF16)
    b0 = b.astype(BF16)
    b1 = (b - b0.astype(F32)).astype(BF16)
    d = functools.partial(jnp.dot, preferred_element_type=F32)
    return d(a0, b0) + (d(a0, b1) + d(a1, b0))


def _silu(x):
    return x * jax.nn.sigmoid(x)


def _softplus(x):
    return jnp.maximum(x, 0.0) + jnp.log(1.0 + jnp.exp(-jnp.abs(x)))


def _lane_chunks(n, parts):
    tiles = n // HEAD_BLOCK
    cuts = [round(tiles * i / parts) * HEAD_BLOCK for i in range(parts + 1)]
    return [(a, b) for a, b in zip(cuts[:-1], cuts[1:]) if b > a]


def _ffn_kernel(x_ref, gpre_ref, gpost_ref, wg_ref, wu_ref, wd_ref, o_ref):
    tm = x_ref.shape[0]
    half = tm // 2
    halves = range(2)
    xs = [x_ref[h * half:(h + 1) * half, :] for h in halves]
    xn = [_rms(x, gpre_ref[...]).astype(BF16) for x in xs]
    accs = [None, None]
    for a, b in _lane_chunks(wg_ref.shape[1], FFN_CHUNKS):
        for h in halves:
            g = jnp.dot(xn[h], wg_ref[:, a:b], preferred_element_type=F32)
            u = jnp.dot(xn[h], wu_ref[:, a:b], preferred_element_type=F32)
            t = jnp.dot((_silu(g) * u).astype(BF16), wd_ref[a:b, :], preferred_element_type=F32)
            accs[h] = t if accs[h] is None else accs[h] + t
    for h in halves:
        o_ref[h * half:(h + 1) * half, :] = xs[h] + 0.5 * _rms(accs[h], gpost_ref[...])


def _half_ffn(x, g_pre, g_post, wg, wu, wd, layer):
    rows, d = x.shape
    f = wg.shape[2]
    tm = min(1024, rows)
    wspec = lambda shape: pl.BlockSpec((None,) + shape, lambda i: (layer, 0, 0), pipeline_mode=pl.Buffered(1))
    return pl.pallas_call(
        _ffn_kernel,
        grid=(rows // tm,),
        in_specs=[
            pl.BlockSpec((tm, d), lambda i: (i, 0)),
            pl.BlockSpec((1, d), lambda i: (0, 0)),
            pl.BlockSpec((1, d), lambda i: (0, 0)),
            wspec((d, f)), wspec((d, f)), wspec((f, d)),
        ],
        out_specs=pl.BlockSpec((tm, d), lambda i: (i, 0)),
        out_shape=jax.ShapeDtypeStruct((rows, d), F32),
        compiler_params=_cparams("parallel"),
        name="half_ffn",
    )(x, g_pre, g_post, wg, wu, wd)


C_CQ = (0, 384)
C_CKV = (384, 640)
C_KPL = (640, 768)
C_KPLS = (768, 896)
C_MISC = (896, 1024)
C_QKV = (1024, 1792)
C_Z = (1792, 2048)
C_SU = (2048, 2304)
W_BIG = 2304


def _mixprep_body(sample, x_ref, g_ref, wbig_ref, gq_ref, wqa_ref, wqb_ref, gkv_ref, tc_ref, ts_ref,
                  wk_ref, wv_ref, ckv_ref, kpe_ref, qkv_ref, z_ref, misc_ref, su_ref, a_ref, b_ref, c_ref):
    u = _rms(x_ref[...], g_ref[...]).astype(BF16)
    y = jnp.dot(u, wbig_ref[...], preferred_element_type=F32)
    sl = lambda c: y[:, c[0]:c[1]]
    tc = tc_ref[...]
    ts = ts_ref[...]
    kpe_pl = sl(C_KPL) * tc + sl(C_KPLS) * ts
    kpe_ref[...] = kpe_pl[:, MLA_NOPE:MLA_NOPE + MLA_ROPE]
    ckv_n = _rms(sl(C_CKV), gkv_ref[...])
    ckv_ref[...] = ckv_n
    qkv_ref[...] = sl(C_QKV)
    z_ref[...] = sl(C_Z)
    misc_ref[...] = sl(C_MISC)
    su_ref[...] = sl(C_SU)

    cqn = _rms(sl(C_CQ), gq_ref[...]).astype(BF16)
    qa = jnp.dot(cqn, wqa_ref[...], preferred_element_type=F32)
    qb = jnp.dot(cqn, wqb_ref[...], preferred_element_type=F32)
    lane = lax.broadcasted_iota(jnp.int32, (1, HEAD_BLOCK), 1)
    qmul = jnp.where(lane < MLA_NOPE, 1.0, 0.0) + tc
    ckb = ckv_n.astype(BF16)
    if sample:
        qlat_ref, qpe_ref = a_ref, b_ref
        for h in range(MLA_HEADS):
            hs = slice(h * HEAD_BLOCK, (h + 1) * HEAD_BLOCK)
            qh = qa[:, hs] * qmul + qb[:, hs] * ts
            qlat_ref[h] = jnp.dot(qa[:, hs].astype(BF16), wk_ref[h], preferred_element_type=F32) * MLA_SCALE
            qpe_ref[h] = qh[:, MLA_NOPE:MLA_NOPE + MLA_ROPE] * MLA_SCALE
    else:
        q_ref, k_ref, v_ref = a_ref, b_ref, c_ref
        kn = jnp.dot(ckb, wk_ref[...], preferred_element_type=F32)
        for h in range(MLA_HEADS):
            hs = slice(h * HEAD_BLOCK, (h + 1) * HEAD_BLOCK)
            qh = qa[:, hs] * qmul + qb[:, hs] * ts
            q_ref[h] = (qh * (MLA_SCALE * LOG2_E)).astype(BF16)
            k_ref[h] = (kn[:, hs] + kpe_pl).astype(BF16)
        v_ref[...] = _bdot_nt(wv_ref[...], ckb).astype(BF16)


def _mixprep_prompt_kernel(x_ref, g_ref, wbig_ref, gq_ref, wqa_ref, wqb_ref, gkv_ref, tc_ref, ts_ref,
                           wk_ref, wv_ref, ckv_ref, kpe_ref, qkv_ref, z_ref, misc_ref, su_ref,
                           q_ref, k_ref, v_ref):
    _mixprep_body(False, x_ref, g_ref, wbig_ref, gq_ref, wqa_ref, wqb_ref, gkv_ref, tc_ref, ts_ref,
                  wk_ref, wv_ref, ckv_ref, kpe_ref, qkv_ref, z_ref, misc_ref, su_ref, q_ref, k_ref, v_ref)


def _mixprep_sample_kernel(x_ref, g_ref, wbig_ref, gq_ref, wqa_ref, wqb_ref, gkv_ref, tc_ref, ts_ref,
                           wk_ref, ckv_ref, kpe_ref, qkv_ref, z_ref, misc_ref, su_ref, qlat_ref, qpe_ref):
    _mixprep_body(True, x_ref, g_ref, wbig_ref, gq_ref, wqa_ref, wqb_ref, gkv_ref, tc_ref, ts_ref,
                  wk_ref, None, ckv_ref, kpe_ref, qkv_ref, z_ref, misc_ref, su_ref, qlat_ref, qpe_ref, None)


def _mixprep(sample, x, w, tc, ts):
    rows, d = x.shape
    tm = min(512, rows)
    row = lambda n: pl.BlockSpec((tm, n), lambda i: (i, 0))
    full = lambda a: pl.BlockSpec(a.shape, lambda i: (0,) * a.ndim)
    hrow = lambda n: pl.BlockSpec((MLA_HEADS, tm, n), lambda i: (0, i, 0))
    common_in = [x, w["mix_g_pre"], w["w_big"], w["mla_g_q"], w["wq_a"], w["wq_b"], w["mla_g_kv"], tc, ts]
    common_specs = [row(d)] + [full(a) for a in common_in[1:7]] + [row(HEAD_BLOCK), row(HEAD_BLOCK)]
    common_out = [
        (jax.ShapeDtypeStruct((rows, MLA_KV_RANK), F32), row(MLA_KV_RANK)),
        (jax.ShapeDtypeStruct((rows, MLA_ROPE), F32), row(MLA_ROPE)),
        (jax.ShapeDtypeStruct((rows, GDN_QKV), F32), row(GDN_QKV)),
        (jax.ShapeDtypeStruct((rows, GDN_HEADS * GDN_DV), F32), row(GDN_HEADS * GDN_DV)),
        (jax.ShapeDtypeStruct((rows, HEAD_BLOCK), F32), row(HEAD_BLOCK)),
        (jax.ShapeDtypeStruct((rows, S5_WIDTH), F32), row(S5_WIDTH)),
    ]
    if sample:
        ins = common_in + [w["wuk_t"]]
        specs = common_specs + [full(w["wuk_t"])]
        outs = common_out + [
            (jax.ShapeDtypeStruct((MLA_HEADS, rows, MLA_KV_RANK), F32), hrow(MLA_KV_RANK)),
            (jax.ShapeDtypeStruct((MLA_HEADS, rows, MLA_ROPE), F32), hrow(MLA_ROPE)),
        ]
        body = _mixprep_sample_kernel
    else:
        ins = common_in + [w["wuk_pad"], w["wuv"]]
        specs = common_specs + [full(w["wuk_pad"]), full(w["wuv"])]
        outs = common_out + [
            (jax.ShapeDtypeStruct((MLA_HEADS, rows, HEAD_BLOCK), BF16), hrow(HEAD_BLOCK)),
            (jax.ShapeDtypeStruct((MLA_HEADS, rows, HEAD_BLOCK), BF16), hrow(HEAD_BLOCK)),
            (jax.ShapeDtypeStruct((MLA_HEADS * MLA_V, rows), BF16),
             pl.BlockSpec((MLA_HEADS * MLA_V, tm), lambda i: (0, i))),
        ]
        body = _mixprep_prompt_kernel
    return pl.pallas_call(
        body,
        grid=(rows // tm,),
        in_specs=specs,
        out_specs=[o[1] for o in outs],
        out_shape=[o[0] for o in outs],
        compiler_params=_cparams("parallel"),
        name="mixprep_sample" if sample else "mixprep_prompt",
    )(*ins)


def _attn_kernel(qi_ref, ki_ref, q_ref, k_ref, vt_ref, o_ref, m_ref, l_ref, acc_ref):
    p = pl.program_id(1)
    i = qi_ref[p]
    j = ki_ref[p]
    tq, tk = q_ref.shape[1], k_ref.shape[1]

    @pl.when(j == 0)
    def _():
        m_ref[...] = jnp.full(m_ref.shape, -jnp.inf, F32)
        l_ref[...] = jnp.zeros_like(l_ref)
        acc_ref[...] = jnp.zeros_like(acc_ref)

    sub = min(ATTN_SUB, tk)

    def step(masked):
        hrows = [slice(hh * MLA_V, (hh + 1) * MLA_V) for hh in range(ATTN_HEADS)]
        m = [m_ref[hh] for hh in range(ATTN_HEADS)]
        l = [l_ref[hh] for hh in range(ATTN_HEADS)]
        acc = [acc_ref[hrows[hh], :] for hh in range(ATTN_HEADS)]
        units = [(c, hh) for c in range(tk // sub) for hh in range(ATTN_HEADS)]

        def scores(c, hh):
            return lax.dot_general(k_ref[hh, c * sub:(c + 1) * sub, :], q_ref[hh], (((1,), (1,)), ((), ())),
                                   preferred_element_type=F32)

        st_next = scores(*units[0])
        for n, (c, hh) in enumerate(units):
            st = st_next
            if n + 1 < len(units):
                st_next = scores(*units[n + 1])
            if masked:
                key = lax.broadcasted_iota(jnp.int32, (sub, tq), 0) + c * sub
                qry = lax.broadcasted_iota(jnp.int32, (sub, tq), 1)
                st = jnp.where(key <= qry, st, -jnp.inf)
            m_new = jnp.maximum(m[hh], jnp.max(st, axis=0, keepdims=True))
            alpha = jnp.exp2(m[hh] - m_new)
            pt = jnp.exp2(st - m_new)
            l[hh] = alpha * l[hh] + jnp.sum(pt, axis=0, keepdims=True)
            acc[hh] = alpha * acc[hh] + jnp.dot(vt_ref[hrows[hh], c * sub:(c + 1) * sub], pt.astype(BF16),
                                                preferred_element_type=F32)
            m[hh] = m_new
        for hh in range(ATTN_HEADS):
            m_ref[hh] = m[hh]
            l_ref[hh] = l[hh]
            acc_ref[hrows[hh], :] = acc[hh]

    @pl.when(j < i)
    def _():
        step(False)

    @pl.when(j == i)
    def _():
        step(True)
        inv = jnp.concatenate([jnp.broadcast_to(1.0 / l_ref[hh], (MLA_V, tq)) for hh in range(ATTN_HEADS)], axis=0)
        o_ref[...] = (acc_ref[...] * inv).T.astype(o_ref.dtype)


def _prompt_attention(q, k, vt):
    rows = q.shape[1]
    t = min(ATTN_TILE, rows)
    n = rows // t
    qi = np.array([i for i in range(n) for _ in range(i + 1)], np.int32)
    ki = np.array([j for i in range(n) for j in range(i + 1)], np.int32)
    grid_spec = pltpu.PrefetchScalarGridSpec(
        num_scalar_prefetch=2,
        grid=(MLA_HEADS // ATTN_HEADS, len(qi)),
        in_specs=[
            pl.BlockSpec((ATTN_HEADS, t, HEAD_BLOCK), lambda hp, p, qi, ki: (hp, qi[p], 0)),
            pl.BlockSpec((ATTN_HEADS, t, HEAD_BLOCK), lambda hp, p, qi, ki: (hp, ki[p], 0)),
            pl.BlockSpec((ATTN_HEADS * MLA_V, t), lambda hp, p, qi, ki: (hp, ki[p])),
        ],
        out_specs=pl.BlockSpec((t, ATTN_HEADS * MLA_V), lambda hp, p, qi, ki: (qi[p], hp)),
        scratch_shapes=[pltpu.VMEM((ATTN_HEADS, 1, t), F32), pltpu.VMEM((ATTN_HEADS, 1, t), F32),
                        pltpu.VMEM((ATTN_HEADS * MLA_V, t), F32)],
    )
    return pl.pallas_call(
        _attn_kernel,
        grid_spec=grid_spec,
        out_shape=jax.ShapeDtypeStruct((rows, MLA_HEADS * MLA_V), BF16),
        compiler_params=_cparams("parallel", "arbitrary"),
        name="mla_prompt_attention",
    )(jnp.asarray(qi), jnp.asarray(ki), q, k, vt)


def _paged_kernel(layer, n_pp, n_steps, n_total, pt_ref, qlat_ref, qpe_ref, cnew_ref, knew_ref, ckv_hbm, kpe_hbm,
                  o_ref, m_ref, l_ref, acc_ref, ckv_buf, kpe_buf, ckv_all_ref, kpe_all_ref, sem):
    j = pl.program_id(1)
    step = pl.program_id(0) * n_steps + j
    slot = lax.rem(step, PAGE_SLOTS)
    ahead = lax.rem(step + PAGE_AHEAD, n_total)
    ahead_slot = lax.rem(step + PAGE_AHEAD, PAGE_SLOTS)
    t_new = cnew_ref.shape[0]
    rows = MLA_HEADS * t_new

    n_grp = max(1, n_pp // PAGE_GROUP)
    per = n_pp // n_grp

    def page_copies(s, dst_slot, k):
        page = pt_ref[s * n_pp + k]
        return (pltpu.make_async_copy(ckv_hbm.at[layer, page], ckv_buf.at[dst_slot, k], sem.at[0, dst_slot]),
                pltpu.make_async_copy(kpe_hbm.at[layer, page], kpe_buf.at[dst_slot, k], sem.at[1, dst_slot]))

    def start_page(s, dst_slot, k):
        for cp in page_copies(s, dst_slot, k):
            cp.start()

    def wait_step(s, dst_slot):
        for k in range(n_pp):
            for cp in page_copies(s, dst_slot, k):
                cp.wait()

    @pl.when(step == 0)
    def _():
        for s in range(PAGE_AHEAD):
            for k in range(n_pp):
                start_page(s, s, k)

    ql = qlat_ref[...].reshape(rows, MLA_KV_RANK).astype(BF16)
    qp = qpe_ref[...].reshape(rows, MLA_ROPE).astype(BF16)

    @pl.when(j == 0)
    def _():
        cn = cnew_ref[...].astype(BF16)
        s = _bdot_nt(ql, cn) + _bdot_nt(qp, knew_ref[...])
        tok = lax.broadcasted_iota(jnp.int32, (rows, t_new), 0) % t_new
        key = lax.broadcasted_iota(jnp.int32, (rows, t_new), 1)
        s = jnp.where(key <= tok, s, -jnp.inf)
        m = jnp.max(s, axis=-1, keepdims=True)
        pm = jnp.exp(s - m)
        m_ref[...] = jnp.broadcast_to(m, m_ref.shape)
        l_ref[...] = jnp.broadcast_to(jnp.sum(pm, axis=-1, keepdims=True), l_ref.shape)
        acc_ref[...] = jnp.dot(pm.astype(BF16), cn, preferred_element_type=F32)

    wait_step(step, slot)
    gkeys = per * PAGE_SIZE

    def scores(g):
        for k in range(g * per, (g + 1) * per):
            start_page(ahead, ahead_slot, k)
            ckv_all_ref[k * PAGE_SIZE:(k + 1) * PAGE_SIZE, :] = ckv_buf[slot, k].astype(BF16)
            kpe_all_ref[:, k * PAGE_SIZE:(k + 1) * PAGE_SIZE] = kpe_buf[slot, k].astype(BF16)
        cg = ckv_all_ref[g * gkeys:(g + 1) * gkeys, :]
        kg = kpe_all_ref[:, g * gkeys:(g + 1) * gkeys]
        return _bdot_nt(ql, cg) + jnp.dot(qp, kg, preferred_element_type=F32), cg

    m = m_ref[...]
    l = l_ref[...]
    acc = acc_ref[...]
    nxt = scores(0)
    for g in range(n_grp):
        s, cg = nxt
        if g + 1 < n_grp:
            nxt = scores(g + 1)
        m_new = jnp.maximum(m, jnp.max(s, axis=-1, keepdims=True))
        alpha = jnp.exp(m - m_new)
        pm = jnp.exp(s - m_new[:, :1])
        l = alpha * l + jnp.sum(pm, axis=-1, keepdims=True)
        acc = alpha[:, :1] * acc + jnp.dot(pm.astype(BF16), cg, preferred_element_type=F32)
        m = m_new
    m_ref[...] = m
    l_ref[...] = l
    acc_ref[...] = acc

    @pl.when(j == n_steps - 1)
    def _():
        o = acc_ref[...] / l_ref[...][:, :1]
        o_ref[...] = o.reshape(MLA_HEADS, t_new, MLA_KV_RANK)

    @pl.when(step == n_total - 1)
    def _():
        for s in range(PAGE_AHEAD):
            wait_step(s, (n_total + s) % PAGE_SLOTS)


def _paged_attention(layer, qlat, qpe, ckv_new, kpe_new, cache_ckv, cache_kpe_t, page_table, t_new):
    n_b, n_pages = page_table.shape
    n_pp = min(PAGES_PER_STEP, n_pages)
    n_steps = n_pages // n_pp
    assert n_b * n_steps >= PAGE_AHEAD, "the page prefetch chain needs at least PAGE_AHEAD grid steps"
    rows = MLA_HEADS * t_new
    grid_spec = pltpu.PrefetchScalarGridSpec(
        num_scalar_prefetch=1,
        grid=(n_b, n_steps),
        in_specs=[
            pl.BlockSpec((MLA_HEADS, t_new, MLA_KV_RANK), lambda b, j, pt: (0, b, 0)),
            pl.BlockSpec((MLA_HEADS, t_new, MLA_ROPE), lambda b, j, pt: (0, b, 0)),
            pl.BlockSpec((t_new, MLA_KV_RANK), lambda b, j, pt: (b, 0)),
            pl.BlockSpec((t_new, MLA_ROPE), lambda b, j, pt: (b, 0)),
            pl.BlockSpec(memory_space=pl.ANY),
            pl.BlockSpec(memory_space=pl.ANY),
        ],
        out_specs=pl.BlockSpec((MLA_HEADS, t_new, MLA_KV_RANK), lambda b, j, pt: (0, b, 0)),
        scratch_shapes=[pltpu.VMEM((rows, HEAD_BLOCK), F32), pltpu.VMEM((rows, HEAD_BLOCK), F32),
                        pltpu.VMEM((rows, MLA_KV_RANK), F32),
                        pltpu.VMEM((PAGE_SLOTS, n_pp, PAGE_SIZE, MLA_KV_RANK), F32),
                        pltpu.VMEM((PAGE_SLOTS, n_pp, MLA_ROPE, PAGE_SIZE), F32),
                        pltpu.VMEM((n_pp * PAGE_SIZE, MLA_KV_RANK), BF16),
                        pltpu.VMEM((MLA_ROPE, n_pp * PAGE_SIZE), BF16),
                        pltpu.SemaphoreType.DMA((2, PAGE_SLOTS))],
    )
    return pl.pallas_call(
        functools.partial(_paged_kernel, layer, n_pp, n_steps, n_b * n_steps),
        grid_spec=grid_spec,
        out_shape=jax.ShapeDtypeStruct((MLA_HEADS, n_b * t_new, MLA_KV_RANK), F32),
        compiler_params=_cparams("arbitrary", "arbitrary"),
        name="mla_paged_attention",
    )(page_table.reshape(-1), qlat, qpe, ckv_new, kpe_new, cache_ckv, cache_kpe_t)


def _gdn_gates(misc, alog, dtb):
    beta = jax.nn.sigmoid(misc)
    g = -jnp.exp(alog) * _softplus(misc + dtb)
    return beta, g


def _l2n(x, scale):
    return x * (lax.rsqrt(jnp.sum(x * x, axis=-1, keepdims=True) + 1e-6) * scale)


def _gdn_prompt_kernel(qkv_ref, z_ref, misc_ref, convw_ref, alog_ref, dtb_ref, gnorm_ref,
                       o_ref, sfin_ref, s_ref, carry_ref):
    i = pl.program_id(0)

    @pl.when(i == 0)
    def _():
        s_ref[...] = jnp.zeros_like(s_ref)
        carry_ref[...] = jnp.zeros_like(carry_ref)

    x = qkv_ref[...]
    tm = x.shape[0]
    w = convw_ref[...]
    row8 = lax.broadcasted_iota(jnp.int32, (8, GDN_QKV), 0)
    cprev = carry_ref[...]
    acc = x * w[GDN_CONV - 1:GDN_CONV]
    for d in range(1, GDN_CONV):
        xr = pltpu.roll(x, d, 0)
        head = jnp.where(row8 < d, pltpu.roll(cprev, d, 0), xr[0:8])
        xs = jnp.concatenate([head, xr[8:]], axis=0)
        acc = acc + xs * w[GDN_CONV - 1 - d:GDN_CONV - d]
    carry_ref[...] = x[tm - 8:tm]
    conv = _silu(acc)

    nq = GDN_HEADS * GDN_DK
    beta_all, g_all = _gdn_gates(misc_ref[...], alog_ref[...], dtb_ref[...])
    z = z_ref[...]
    gnorm = gnorm_ref[...]
    c = GDN_CHUNK
    blk = 2 * c
    ri = lax.broadcasted_iota(jnp.int32, (blk, blk), 0)
    ci = lax.broadcasted_iota(jnp.int32, (blk, blk), 1)
    tri2 = jnp.where((ri >= ci) & ((ri // c) == (ci // c)), 1.0, 0.0).astype(BF16)
    ns = GDN_HEADS * c
    rs_ = lax.broadcasted_iota(jnp.int32, (ns, ns), 0)
    cs_ = lax.broadcasted_iota(jnp.int32, (ns, ns), 1)
    same = (rs_ // c) == (cs_ // c)
    incl = same & (rs_ >= cs_)
    strict = same & (rs_ > cs_)
    heads = range(GDN_HEADS)
    stack = lambda parts: jnp.concatenate(parts, axis=0)

    def prep_block(b2):
        gblk = g_all[b2 * blk:(b2 + 1) * blk]
        g0, g1, g2 = _split3(gblk)
        d = functools.partial(jnp.dot, preferred_element_type=F32)
        gcum = d(tri2, g0) + d(tri2, g1) + d(tri2, g2)
        gcum_t = gcum.T
        items = []
        for c2 in range(2):
            r0 = b2 * blk + c2 * c
            cr = slice(c2 * c, (c2 + 1) * c)
            qs = [_l2n(conv[r0:r0 + c, h * GDN_DK:(h + 1) * GDN_DK], GDN_DK ** -0.5) for h in heads]
            ks = [_l2n(conv[r0:r0 + c, nq + h * GDN_DK:nq + (h + 1) * GDN_DK], 1.0) for h in heads]
            vs = [conv[r0:r0 + c, 2 * nq + h * GDN_DV:2 * nq + (h + 1) * GDN_DV] for h in heads]
            gcs = [gcum[cr, GDN_HEADS + h:GDN_HEADS + h + 1] for h in heads]
            q_st, k_st, v_st, gc_st = stack(qs), stack(ks), stack(vs), stack(gcs)
            gr_st = jnp.concatenate([gcum_t[GDN_HEADS + h:GDN_HEADS + h + 1, cr] for h in heads], axis=1)
            beta_st = stack([beta_all[r0:r0 + c, h:h + 1] for h in heads])
            decay = jnp.exp(jnp.where(incl, gc_st - gr_st, -jnp.inf))
            eg_st = jnp.exp(gc_st)
            kk = _bdot_nt(k_st, k_st)
            mk = -jnp.where(strict, beta_st * kk * decay, 0.0)
            rhs = jnp.concatenate([v_st * beta_st, k_st * (beta_st * eg_st)], axis=-1)
            qk = _bdot_nt(q_st, k_st) * decay
            items.append([r0, rhs, qk, qs, ks, gcs, eg_st, mk])
        return items

    n_sq = int(math.log2(c))

    def solve_level(items, kq):
        for item in items:
            item[1] = item[1] + _dot_split(item[7], item[1])
            if kq < n_sq - 1:
                item[7] = _dot_split(item[7], item[7])

    state = [s_ref[h] for h in heads]

    def state_chunk(item):
        r0, rhs, qk, qs, ks, gcs, eg_st, _ = item
        outs = []
        for h in heads:
            hr = slice(h * c, (h + 1) * c)
            uu = rhs[hr, :GDN_DV]
            ww = rhs[hr, GDN_DV:]
            gc = gcs[h]
            glast = gc[c - 1:c, :]
            s_h = state[h]
            v_new = uu - _bdot(ww, s_h)
            o_h = _bdot(qs[h] * eg_st[hr], s_h) + _bdot(qk[hr, h * c:(h + 1) * c], v_new)
            kdec = ks[h] * jnp.exp(glast - gc)
            state[h] = s_h * jnp.exp(glast) + lax.dot_general(
                kdec.astype(BF16), v_new.astype(BF16), (((0,), (0,)), ((), ())),
                preferred_element_type=F32)
            zh = z[r0:r0 + c, h * GDN_DV:(h + 1) * GDN_DV]
            outs.append(_rms(o_h, gnorm) * _silu(zh))
        o_ref[r0:r0 + c, :] = jnp.concatenate(outs, axis=-1).astype(o_ref.dtype)

    n_blk = tm // blk
    groups = [list(range(g, min(g + GDN_BLOCKS_PER_STAGE, n_blk)))
              for g in range(0, n_blk, GDN_BLOCKS_PER_STAGE)]
    prepared = [it for b2 in groups[0] for it in prep_block(b2)]
    finished = []
    for gi in range(len(groups)):
        nxt_blocks = list(groups[gi + 1]) if gi + 1 < len(groups) else []
        nxt, todo = [], list(finished)
        for kq in range(n_sq):
            solve_level(prepared, kq)
            if nxt_blocks:
                nxt += prep_block(nxt_blocks.pop(0))
            for _ in range(2):
                if todo:
                    state_chunk(todo.pop(0))
        for b2 in nxt_blocks:
            nxt += prep_block(b2)
        for it in todo:
            state_chunk(it)
        finished, prepared = prepared, nxt
    for it in finished:
        state_chunk(it)
    for h in heads:
        s_ref[h] = state[h]

    @pl.when(i == pl.num_programs(0) - 1)
    def _():
        sfin_ref[...] = s_ref[...]


def _gdn_prompt(qkv, z, misc, w):
    rows = qkv.shape[0]
    tm = min(512, rows)
    row = lambda n: pl.BlockSpec((tm, n), lambda i: (i, 0))
    full = lambda a: pl.BlockSpec(a.shape, lambda i: (0,) * a.ndim)
    st = (GDN_HEADS, GDN_DK, GDN_DV)
    ins = [qkv, z, misc, w["gdn_conv_w"], w["gdn_alog_pad"], w["gdn_dtb_pad"], w["gdn_g_norm"]]
    return pl.pallas_call(
        _gdn_prompt_kernel,
        grid=(rows // tm,),
        in_specs=[row(GDN_QKV), row(GDN_HEADS * GDN_DV), row(HEAD_BLOCK)] + [full(a) for a in ins[3:]],
        out_specs=[row(GDN_HEADS * GDN_DV), pl.BlockSpec(st, lambda i: (0, 0, 0))],
        out_shape=[jax.ShapeDtypeStruct((rows, GDN_HEADS * GDN_DV), BF16), jax.ShapeDtypeStruct(st, F32)],
        scratch_shapes=[pltpu.VMEM(st, F32), pltpu.VMEM((8, GDN_QKV), F32)],
        compiler_params=_cparams("arbitrary"),
        name="gdn_prompt",
    )(*ins)


def _gdn_sample_kernel(t_new, qkv_ref, st_ref, z_ref, misc_ref, convw_ref, alog_ref, dtb_ref, gcol_ref,
                       s0_ref, o_ref, s_ref,
                       rows_ref, zo_ref, gate_ref, qt_ref, kt_ref, vt_ref, gt_ref, ot_ref):
    n_rows = qkv_ref.shape[0]
    n_b = n_rows // t_new
    nq = GDN_HEADS * GDN_DK
    x = qkv_ref[...]
    st = st_ref[...]
    w = convw_ref[...]
    tpos = lax.broadcasted_iota(jnp.int32, (n_rows, GDN_QKV), 0) % t_new
    acc = x * w[GDN_CONV - 1:GDN_CONV]
    for d in range(1, GDN_CONV):
        xr = pltpu.roll(x, d, 0)
        back = GDN_CONV - 1 - d
        sr = st if back == 0 else pltpu.roll(st, n_rows - back, 0)
        acc = acc + jnp.where(tpos < d, sr, xr) * w[GDN_CONV - 1 - d:GDN_CONV - d]
    conv = _silu(acc)
    parts = []
    for h in range(GDN_HEADS):
        parts.append(_l2n(conv[:, h * GDN_DK:(h + 1) * GDN_DK], GDN_DK ** -0.5))
    for h in range(GDN_HEADS):
        parts.append(_l2n(conv[:, nq + h * GDN_DK:nq + (h + 1) * GDN_DK], 1.0))
    parts.append(conv[:, 2 * nq:])
    feats = jnp.concatenate(parts, axis=-1)
    n_chunk = GDN_QKV // HEAD_BLOCK
    per_part = nq // HEAD_BLOCK
    for cc in range(n_chunk):
        rows_ref[cc] = feats[:, cc * HEAD_BLOCK:(cc + 1) * HEAD_BLOCK]
    zsil = _silu(z_ref[...])
    for cc in range(per_part):
        zo_ref[cc] = zsil[:, cc * HEAD_BLOCK:(cc + 1) * HEAD_BLOCK]
    beta_all, g_all = _gdn_gates(misc_ref[...], alog_ref[...], dtb_ref[...])
    lane = lax.broadcasted_iota(jnp.int32, beta_all.shape, 1)
    gate_ref[...] = jnp.where(lane < GDN_HEADS, beta_all, jnp.exp(g_all))

    for t in range(t_new):
        for cc in range(n_chunk):
            blk_t = rows_ref[cc, pl.ds(t, n_b, stride=t_new), :].T
            dst = (qt_ref, kt_ref, vt_ref)[cc // per_part]
            lo = (cc % per_part) * HEAD_BLOCK
            dst[t, lo:lo + HEAD_BLOCK, :] = blk_t
        gt_ref[t] = gate_ref[pl.ds(t, n_b, stride=t_new), :].T

    s_ref[...] = s0_ref[...]

    gcol = gcol_ref[...]
    for t in range(t_new):
        for h in range(GDN_HEADS):
            egr = gt_ref[t, GDN_HEADS + h:GDN_HEADS + h + 1, :]
            betar = gt_ref[t, h:h + 1, :]
            base = h * GDN_DK

            def p1(dk, racc):
                kb = kt_ref[t, pl.ds(base + dk, 1), :]
                return racc + s_ref[base + dk] * kb

            rr = lax.fori_loop(0, GDN_DK, p1, jnp.zeros((GDN_DV, n_b), F32), unroll=8) * egr
            dd = betar * (vt_ref[t, base:base + GDN_DV, :] - rr)

            def p2(dk, oacc):
                kb = kt_ref[t, pl.ds(base + dk, 1), :]
                qb = qt_ref[t, pl.ds(base + dk, 1), :]
                sn = s_ref[base + dk] * egr + kb * dd
                s_ref[base + dk] = sn
                return oacc + sn * qb

            oo = lax.fori_loop(0, GDN_DK, p2, jnp.zeros((GDN_DV, n_b), F32), unroll=8)
            on = oo * lax.rsqrt(jnp.mean(oo * oo, axis=0, keepdims=True) + RMS_EPS) * gcol
            ot_ref[base:base + GDN_DV, :] = on
        on_rows = ot_ref[...].T
        for cc in range(per_part):
            zt = zo_ref[cc, pl.ds(t, n_b, stride=t_new), :]
            zo_ref[cc, pl.ds(t, n_b, stride=t_new), :] = on_rows[:, cc * HEAD_BLOCK:(cc + 1) * HEAD_BLOCK] * zt
    for cc in range(per_part):
        o_ref[:, cc * HEAD_BLOCK:(cc + 1) * HEAD_BLOCK] = zo_ref[cc]


def _gdn_sample(qkv, st_rows, z, misc, w, s0, t_new):
    rows = qkv.shape[0]
    n_b = rows // t_new
    feat = GDN_HEADS * GDN_DK
    ins = [qkv, st_rows, z, misc, w["gdn_conv_w"], w["gdn_alog_pad"], w["gdn_dtb_pad"], w["gdn_g_col"], s0]
    return pl.pallas_call(
        functools.partial(_gdn_sample_kernel, t_new),
        out_shape=[jax.ShapeDtypeStruct((rows, GDN_HEADS * GDN_DV), F32),
                   jax.ShapeDtypeStruct((GDN_HEADS * GDN_DK, GDN_DV, n_b), F32)],
        scratch_shapes=[
            pltpu.VMEM((GDN_QKV // HEAD_BLOCK, rows, HEAD_BLOCK), F32),
            pltpu.VMEM((GDN_HEADS * GDN_DV // HEAD_BLOCK, rows, HEAD_BLOCK), F32),
            pltpu.VMEM((rows, HEAD_BLOCK), F32),
            pltpu.VMEM((t_new, feat, n_b), F32),
            pltpu.VMEM((t_new, feat, n_b), F32),
            pltpu.VMEM((t_new, GDN_HEADS * GDN_DV, n_b), F32),
            pltpu.VMEM((t_new, HEAD_BLOCK, n_b), F32),
            pltpu.VMEM((GDN_HEADS * GDN_DV, n_b), F32),
        ],
        compiler_params=pltpu.CompilerParams(vmem_limit_bytes=VMEM_LIMIT_BYTES),
        name="gdn_sample",
    )(*ins)


def _cmul(ar, ai, br, bi):
    return ar * br - ai * bi, ar * bi + ai * br


def _s5_kernel(per_group_state, u_ref, bbr_ref, bbi_ref, ar_ref, ai_ref, cr_ref, ci_ref, d_ref,
               wglu_ref, bglu_ref, h0r_ref, h0i_ref, o_ref, hr_out_ref, hi_out_ref,
               br_ref, bi_ref, cr_carry_ref, ci_carry_ref):
    i = pl.program_id(0)
    tm = u_ref.shape[0]
    n_grp = tm // 8

    if not per_group_state:
        @pl.when(i == 0)
        def _():
            cr_carry_ref[...] = jnp.zeros_like(cr_carry_ref)
            ci_carry_ref[...] = jnp.zeros_like(ci_carry_ref)

    u = u_ref[...]
    ub = u.astype(BF16)
    br_ref[...] = jnp.dot(ub, bbr_ref[...], preferred_element_type=F32)
    bi_ref[...] = jnp.dot(ub, bbi_ref[...], preferred_element_type=F32)

    ar = ar_ref[...]
    ai = ai_ref[...]
    p1 = (ar, ai)
    p2 = _cmul(*p1, *p1)
    p3 = _cmul(*p2, *p1)
    p4 = _cmul(*p2, *p2)
    p5 = _cmul(*p4, *p1)
    p6 = _cmul(*p4, *p2)
    p7 = _cmul(*p4, *p3)
    p8 = _cmul(*p4, *p4)
    row8 = lax.broadcasted_iota(jnp.int32, (8, S5_LANES), 0)
    pw_r = jnp.zeros((8, S5_LANES), F32)
    pw_i = jnp.zeros((8, S5_LANES), F32)
    for t, pw in enumerate((p1, p2, p3, p4, p5, p6, p7, p8)):
        pw_r = jnp.where(row8 == t, pw[0], pw_r)
        pw_i = jnp.where(row8 == t, pw[1], pw_i)

    steps = [(d, (jnp.where(row8 >= d, pw[0], 0.0), jnp.where(row8 >= d, pw[1], 0.0)))
             for d, pw in ((1, p1), (2, p2), (4, p4))]

    def body(gi, carry):
        r0 = pl.multiple_of(gi * 8, 8)
        a = br_ref[pl.ds(r0, 8), :]
        b = bi_ref[pl.ds(r0, 8), :]
        for d, pw in steps:
            da, db = _cmul(pw[0], pw[1], pltpu.roll(a, d, 0), pltpu.roll(b, d, 0))
            a = a + da
            b = b + db
        if per_group_state:
            c_r = h0r_ref[pl.ds(gi, 1), :]
            c_i = h0i_ref[pl.ds(gi, 1), :]
        else:
            c_r, c_i = carry
        da, db = _cmul(pw_r, pw_i, c_r, c_i)
        a = a + da
        b = b + db
        br_ref[pl.ds(r0, 8), :] = a
        bi_ref[pl.ds(r0, 8), :] = b
        if per_group_state:
            hr_out_ref[pl.ds(gi, 1), :] = a[7:8]
            hi_out_ref[pl.ds(gi, 1), :] = b[7:8]
            return carry
        return a[7:8], b[7:8]

    if per_group_state:
        zero = jnp.zeros((1, S5_LANES), F32)
        lax.fori_loop(0, n_grp, body, (zero, zero))
    else:
        c_fin = lax.fori_loop(0, n_grp, body, (cr_carry_ref[...], ci_carry_ref[...]))
        cr_carry_ref[...] = c_fin[0]
        ci_carry_ref[...] = c_fin[1]
        hr_out_ref[...] = c_fin[0]
        hi_out_ref[...] = c_fin[1]

    y = (jnp.dot(br_ref[...].astype(BF16), cr_ref[...], preferred_element_type=F32)
         - jnp.dot(bi_ref[...].astype(BF16), ci_ref[...], preferred_element_type=F32)
         + d_ref[...] * u)
    zg = jax.nn.gelu(y)
    gate = jax.nn.sigmoid(jnp.dot(zg.astype(BF16), wglu_ref[...], preferred_element_type=F32) + bglu_ref[...])
    o_ref[...] = (zg * gate).astype(o_ref.dtype)


def _s5(su, w, h0r, h0i, per_group_state, out_dtype):
    rows = su.shape[0]
    tm = min(512, rows)
    n_grp = tm // 8
    row = lambda n: pl.BlockSpec((tm, n), lambda i: (i, 0))
    full = lambda a: pl.BlockSpec(a.shape, lambda i: (0,) * a.ndim)
    wlist = [w["s5_bb_re"], w["s5_bb_im"], w["s5_ab_re"], w["s5_ab_im"], w["s5_c_re"], w["s5_c_im"],
             w["s5_d"], w["s5_w_glu"], w["s5_b_glu"]]
    if per_group_state:
        st_spec = pl.BlockSpec((n_grp, S5_LANES), lambda i: (i, 0))
        st_shape = jax.ShapeDtypeStruct((rows // 8, S5_LANES), F32)
    else:
        st_spec = pl.BlockSpec((1, S5_LANES), lambda i: (0, 0))
        st_shape = jax.ShapeDtypeStruct((1, S5_LANES), F32)
    return pl.pallas_call(
        functools.partial(_s5_kernel, per_group_state),
        grid=(rows // tm,),
        in_specs=[row(S5_WIDTH)] + [full(a) for a in wlist] + [st_spec, st_spec],
        out_specs=[row(S5_WIDTH), st_spec, st_spec],
        out_shape=[jax.ShapeDtypeStruct((rows, S5_WIDTH), out_dtype), st_shape, st_shape],
        scratch_shapes=[pltpu.VMEM((tm, S5_LANES), F32), pltpu.VMEM((tm, S5_LANES), F32),
                        pltpu.VMEM((1, S5_LANES), F32), pltpu.VMEM((1, S5_LANES), F32)],
        compiler_params=_cparams("arbitrary"),
        name="s5_sample" if per_group_state else "s5_prompt",
    )(su, *wlist, h0r, h0i)


def _mixout_kernel(x_ref, oa_ref, og_ref, os_ref, wuv_ref, woa_ref, wob_ref, woc_ref,
                   gpost_ref, gxa_ref, wq_ref, x2_ref, q_ref):
    mixed = None
    for h in range(MLA_HEADS):
        o_h = jnp.dot(oa_ref[h].astype(BF16), wuv_ref[h], preferred_element_type=F32)
        t = jnp.dot(o_h.astype(BF16), woa_ref[h * MLA_V:(h + 1) * MLA_V, :], preferred_element_type=F32)
        mixed = t if mixed is None else mixed + t
    mixed = mixed + jnp.dot(og_ref[...].astype(BF16), wob_ref[...], preferred_element_type=F32)
    mixed = mixed + jnp.dot(os_ref[...].astype(BF16), woc_ref[...], preferred_element_type=F32)
    x2 = x_ref[...] + _rms(mixed, gpost_ref[...])
    x2_ref[...] = x2
    hq = _rms(x2, gxa_ref[...]).astype(BF16)
    q_ref[...] = jnp.dot(hq, wq_ref[...], preferred_element_type=F32) * XA_SCALE


def _mixout(x, o_lat, o_gdn, o_s5, w):
    rows, d = x.shape
    tm = min(512, rows)
    row = lambda n: pl.BlockSpec((tm, n), lambda i: (i, 0))
    full = lambda a: pl.BlockSpec(a.shape, lambda i: (0,) * a.ndim)
    oa_spec = pl.BlockSpec((MLA_HEADS, tm, MLA_KV_RANK), lambda i: (0, i, 0))
    wl = [w["wuv_h"], w["wo_a"], w["wo_b"], w["wo_c"], w["mix_g_post"], w["xa_g_pre"], w["xa_w_q"]]
    return pl.pallas_call(
        _mixout_kernel,
        grid=(rows // tm,),
        in_specs=[row(d), oa_spec, row(o_gdn.shape[1]), row(o_s5.shape[1])] + [full(a) for a in wl],
        out_specs=[row(d), row(d)],
        out_shape=[jax.ShapeDtypeStruct((rows, d), F32), jax.ShapeDtypeStruct((rows, d), F32)],
        compiler_params=_cparams("parallel"),
        name="mixout_sample",
    )(x, o_lat, o_gdn, o_s5, *wl)


def _postmix_kernel(x_ref, oa_ref, og_ref, os_ref, woa_ref, wob_ref, woc_ref, gpost_ref, gxa_ref, wq_ref,
                    mk_ref, mv_ref, wo_ref, gxo_ref, y_ref):
    mixed = jnp.dot(oa_ref[...], woa_ref[...], preferred_element_type=F32)
    mixed = mixed + jnp.dot(og_ref[...], wob_ref[...], preferred_element_type=F32)
    mixed = mixed + jnp.dot(os_ref[...], woc_ref[...], preferred_element_type=F32)
    x2 = x_ref[...] + _rms(mixed, gpost_ref[...])
    q = jnp.dot(_rms(x2, gxa_ref[...]).astype(BF16), wq_ref[...], preferred_element_type=F32) * XA_SCALE
    outs = []
    for h in range(XA_HEADS):
        hs = slice(h * XA_HEAD_DIM, (h + 1) * XA_HEAD_DIM)
        s = _bdot_nt(q[:, hs], mk_ref[:, hs])
        m = jnp.max(s, axis=-1, keepdims=True)
        pm = jnp.exp(s - m)
        pr = pm / jnp.sum(pm, axis=-1, keepdims=True)
        outs.append(_bdot(pr, mv_ref[:, hs]).astype(BF16))
    xa = jnp.dot(jnp.concatenate(outs, axis=-1), wo_ref[...], preferred_element_type=F32)
    y_ref[...] = x2 + _rms(xa, gxo_ref[...])


def _postmix(x, o_mla, o_gdn, o_s5, w, mem_k, mem_v, layer):
    rows, d = x.shape
    n_mem = mem_k.shape[2]
    tm = min(512, rows)
    row = lambda n: pl.BlockSpec((tm, n), lambda i: (i, 0))
    full = lambda a: pl.BlockSpec(a.shape, lambda i: (0,) * a.ndim)
    mspec = pl.BlockSpec((None, None, n_mem, d), lambda i: (layer, 0, 0, 0))
    w1 = [w["wo_a"], w["wo_b"], w["wo_c"], w["mix_g_post"], w["xa_g_pre"], w["xa_w_q"]]
    w2 = [w["xa_w_o"], w["xa_g_post"]]
    return pl.pallas_call(
        _postmix_kernel,
        grid=(rows // tm,),
        in_specs=([row(d), row(o_mla.shape[1]), row(o_gdn.shape[1]), row(o_s5.shape[1])]
                  + [full(a) for a in w1] + [mspec, mspec] + [full(a) for a in w2]),
        out_specs=row(d),
        out_shape=jax.ShapeDtypeStruct((rows, d), F32),
        compiler_params=_cparams("parallel"),
        name="postmix_prompt",
    )(x, o_mla, o_gdn, o_s5, *w1, mem_k, mem_v, *w2)


def _xattn_heads_merged_kernel(n_b, q_ref, mk_ref, mv_ref, o_ref):
    rows = q_ref.shape[0] // n_b
    n_mem = mk_ref.shape[1]
    for bi in range(n_b):
        q = q_ref[bi * rows:(bi + 1) * rows, :]
        q_all = jnp.concatenate([q[:, h * XA_HEAD_DIM:(h + 1) * XA_HEAD_DIM] for h in range(XA_HEADS)], axis=0)
        k2 = mk_ref[bi].reshape(n_mem * XA_HEADS, XA_HEAD_DIM)
        v2 = mv_ref[bi].reshape(n_mem * XA_HEADS, XA_HEAD_DIM)
        s = _bdot_nt(q_all, k2)
        q_head = lax.broadcasted_iota(jnp.int32, s.shape, 0) // rows
        m_head = lax.broadcasted_iota(jnp.int32, s.shape, 1) % XA_HEADS
        s = jnp.where(q_head == m_head, s, -jnp.inf)
        m = jnp.max(s, axis=-1, keepdims=True)
        pm = jnp.exp(s - m)
        pr = pm / jnp.sum(pm, axis=-1, keepdims=True)
        o_all = _bdot(pr, v2)
        o_ref[bi * rows:(bi + 1) * rows, :] = jnp.concatenate(
            [o_all[h * rows:(h + 1) * rows] for h in range(XA_HEADS)], axis=-1).astype(o_ref.dtype)


def _xattn_sample(q, mem_k, mem_v, layer, t_new):
    rows, d = q.shape
    n_mem = mem_k.shape[2]
    n_b = min(XA_SEQ_PER_STEP, rows // t_new)
    tm = n_b * t_new
    mspec = pl.BlockSpec((None, n_b, n_mem, XA_HEADS, XA_HEAD_DIM), lambda i: (layer, i, 0, 0, 0))
    return pl.pallas_call(
        functools.partial(_xattn_heads_merged_kernel, n_b),
        grid=(rows // tm,),
        in_specs=[pl.BlockSpec((tm, d), lambda i: (i, 0)), mspec, mspec],
        out_specs=pl.BlockSpec((tm, d), lambda i: (i, 0)),
        out_shape=jax.ShapeDtypeStruct((rows, d), BF16),
        compiler_params=_cparams("parallel"),
        name="xattn_sample",
    )(q, mem_k, mem_v)


def _xaout_kernel(x_ref, o_ref, wo_ref, g_ref, y_ref):
    xa = jnp.dot(o_ref[...], wo_ref[...], preferred_element_type=F32)
    y_ref[...] = x_ref[...] + _rms(xa, g_ref[...])


def _xaout(x, o, w):
    rows, d = x.shape
    tm = min(512, rows)
    row = pl.BlockSpec((tm, d), lambda i: (i, 0))
    return pl.pallas_call(
        _xaout_kernel,
        grid=(rows // tm,),
        in_specs=[row, row, pl.BlockSpec((d, d), lambda i: (0, 0)), pl.BlockSpec((1, d), lambda i: (0, 0))],
        out_specs=row,
        out_shape=jax.ShapeDtypeStruct((rows, d), F32),
        compiler_params=_cparams("parallel"),
        name="xattn_out",
    )(x, o, w["xa_w_o"], w["xa_g_post"])


def _memproj_kernel(m_ref, wk_ref, wv_ref, k_ref, v_ref):
    m = m_ref[...].astype(BF16)
    for l in range(wk_ref.shape[0]):
        k_ref[l] = jnp.dot(m, wk_ref[l], preferred_element_type=F32)
        v_ref[l] = jnp.dot(m, wv_ref[l], preferred_element_type=F32)


def _memproj(mem, wk, wv):
    depth = wk.shape[0]
    shp = jax.ShapeDtypeStruct((depth,) + mem.shape, F32)
    return pl.pallas_call(
        _memproj_kernel,
        out_shape=[shp, shp],
        compiler_params=pltpu.CompilerParams(vmem_limit_bytes=VMEM_LIMIT_BYTES),
        name="mem_kv_proj",
    )(mem, wk, wv)


def _layer_weights(l, p):
    d_model = p["w_in"].shape[1]
    w = {}
    r1 = lambda a: a[l].reshape(1, -1).astype(F32)
    for name in ("ffn1", "ffn2"):
        w[name + "_g_pre"] = r1(p[name + "_g_pre"])
        w[name + "_g_post"] = r1(p[name + "_g_post"])
    for name in ("mix_g_pre", "mix_g_post", "mla_g_q", "mla_g_kv", "xa_g_pre", "xa_g_post", "gdn_g_norm",
                 "s5_d", "s5_b_glu"):
        w[name] = r1(p[name])
    w["gdn_g_col"] = p["gdn_g_norm"][l].reshape(-1, 1).astype(F32)

    w_in = p["w_in"][l]
    offs = np.cumsum([0, MLA_Q_RANK, MLA_KV_RANK, MLA_ROPE, GDN_QKV, GDN_HEADS * GDN_DV, GDN_HEADS, GDN_HEADS,
                      S5_WIDTH])
    w_cq, w_ckv, w_kpe, w_qkv, w_z, w_b, w_a, w_su = [w_in[:, offs[i]:offs[i + 1]] for i in range(8)]
    half = MLA_ROPE // 2
    zeros = lambda n: jnp.zeros((d_model, n), w_in.dtype)
    w_kpe_sw = jnp.concatenate([-w_kpe[:, half:], w_kpe[:, :half]], axis=1)
    tail = HEAD_BLOCK - MLA_NOPE - MLA_ROPE
    w_kpl = jnp.concatenate([zeros(MLA_NOPE), w_kpe, zeros(tail)], axis=1)
    w_kpls = jnp.concatenate([zeros(MLA_NOPE), w_kpe_sw, zeros(tail)], axis=1)
    w_misc = jnp.concatenate([w_b, w_a, zeros(HEAD_BLOCK - 2 * GDN_HEADS)], axis=1)
    w["w_big"] = jnp.concatenate([w_cq, w_ckv, w_kpl, w_kpls, w_misc, w_qkv, w_z, w_su], axis=1).astype(BF16)

    w_uq = p["mla_w_uq"][l].reshape(MLA_Q_RANK, MLA_HEADS, MLA_NOPE + MLA_ROPE)
    nope, x1, x2 = w_uq[..., :MLA_NOPE], w_uq[..., MLA_NOPE:MLA_NOPE + half], w_uq[..., MLA_NOPE + half:]
    zq = lambda n: jnp.zeros((MLA_Q_RANK, MLA_HEADS, n), w_uq.dtype)
    w["wq_a"] = jnp.concatenate([nope, x1, x2, zq(tail)], axis=-1).reshape(MLA_Q_RANK, -1).astype(BF16)
    w["wq_b"] = jnp.concatenate([zq(MLA_NOPE), -x2, x1, zq(tail)], axis=-1).reshape(MLA_Q_RANK, -1).astype(BF16)

    w_uk = p["mla_w_uk"][l]
    w_uv = p["mla_w_uv"][l]
    zk = jnp.zeros((MLA_KV_RANK, MLA_HEADS, HEAD_BLOCK - MLA_NOPE), w_uk.dtype)
    w["wuk_pad"] = jnp.concatenate([w_uk, zk], axis=-1).reshape(MLA_KV_RANK, -1).astype(BF16)
    wuk_t = jnp.transpose(w_uk, (1, 2, 0))
    w["wuk_t"] = jnp.concatenate(
        [wuk_t, jnp.zeros((MLA_HEADS, HEAD_BLOCK - MLA_NOPE, MLA_KV_RANK), w_uk.dtype)], axis=1).astype(BF16)
    w["wuv"] = w_uv.reshape(MLA_KV_RANK, -1).T.astype(BF16)
    w["wuv_h"] = jnp.transpose(w_uv, (1, 0, 2)).astype(BF16)

    w_out = p["w_out"][l]
    n_a = MLA_HEADS * MLA_V
    n_b = n_a + GDN_HEADS * GDN_DV
    w["wo_a"] = w_out[:n_a].astype(BF16)
    w["wo_b"] = w_out[n_a:n_b].astype(BF16)
    w["wo_c"] = w_out[n_b:].astype(BF16)
    w["xa_w_q"] = p["xa_w_q"][l].astype(BF16)
    w["xa_w_o"] = p["xa_w_o"][l].astype(BF16)

    w["gdn_conv_w"] = p["gdn_conv_w"][l].astype(F32)
    pad_gate = lambda v: jnp.zeros((1, HEAD_BLOCK), F32).at[0, GDN_HEADS:2 * GDN_HEADS].set(v.astype(F32))
    w["gdn_alog_pad"] = pad_gate(p["gdn_a_log"][l])
    w["gdn_dtb_pad"] = pad_gate(p["gdn_dt_bias"][l])

    a_re, a_im = p["s5_a_re"][l].astype(F32), p["s5_a_im"][l].astype(F32)
    dt = jnp.exp(p["s5_log_dt"][l].astype(F32))[:, None]
    mag = jnp.exp(a_re * dt)
    ab_re, ab_im = mag * jnp.cos(a_im * dt), mag * jnp.sin(a_im * dt)
    den = a_re * a_re + a_im * a_im
    nr, ni = ab_re - 1.0, ab_im
    coef_re = (nr * a_re + ni * a_im) / den
    coef_im = (ni * a_re - nr * a_im) / den
    b_re, b_im = p["s5_b_re"][l].astype(F32), p["s5_b_im"][l].astype(F32)
    bb_re = coef_re[..., None] * b_re - coef_im[..., None] * b_im
    bb_im = coef_re[..., None] * b_im + coef_im[..., None] * b_re
    eye = jnp.eye(S5_GROUPS, dtype=F32)
    bd_in = lambda bb: jnp.einsum("gnp,gh->gphn", bb, eye).reshape(S5_WIDTH, S5_LANES).astype(BF16)
    bd_out = lambda cc: jnp.einsum("gpn,gh->gnhp", cc.astype(F32), eye).reshape(S5_LANES, S5_WIDTH).astype(BF16)
    w["s5_bb_re"], w["s5_bb_im"] = bd_in(bb_re), bd_in(bb_im)
    w["s5_c_re"], w["s5_c_im"] = bd_out(p["s5_c_re"][l]), bd_out(p["s5_c_im"][l])
    w["s5_ab_re"] = ab_re.reshape(1, S5_LANES)
    w["s5_ab_im"] = ab_im.reshape(1, S5_LANES)
    w["s5_w_glu"] = p["s5_w_glu"][l].astype(BF16)
    return w


def _rope_tables(pos):
    half = MLA_ROPE // 2
    inv = ROPE_THETA ** (-jnp.arange(half, dtype=F32) / half)
    ang = pos.astype(F32)[:, None] * inv[None, :]
    c, s = jnp.cos(ang), jnp.sin(ang)
    n = pos.shape[0]
    z0 = jnp.zeros((n, MLA_NOPE), F32)
    z1 = jnp.zeros((n, HEAD_BLOCK - MLA_NOPE - MLA_ROPE), F32)
    return jnp.concatenate([z0, c, c, z1], axis=1), jnp.concatenate([z0, s, s, z1], axis=1)


def kernel(x_prompt, x_sample, mem_prompt, cache_ckv, cache_kpe, page_table, cache_mem_k, cache_mem_v, state_gdn, state_gdn_conv, state_s5_re, state_s5_im, ffn1_g_pre, ffn1_g_post, ffn1_w_gate, ffn1_w_up, ffn1_w_down, mix_g_pre, mix_g_post, w_in, w_out, mla_g_q, mla_w_uq, mla_g_kv, mla_w_uk, mla_w_uv, gdn_conv_w, gdn_a_log, gdn_dt_bias, gdn_g_norm, s5_a_re, s5_a_im, s5_log_dt, s5_b_re, s5_b_im, s5_c_re, s5_c_im, s5_d, s5_w_glu, s5_b_glu, xa_g_pre, xa_g_post, xa_w_q, xa_w_k, xa_w_v, xa_w_o, ffn2_g_pre, ffn2_g_post, ffn2_w_gate, ffn2_w_up, ffn2_w_down):
    params = dict(
        ffn1_g_pre=ffn1_g_pre, ffn1_g_post=ffn1_g_post, ffn1_w_gate=ffn1_w_gate, ffn1_w_up=ffn1_w_up,
        ffn1_w_down=ffn1_w_down, mix_g_pre=mix_g_pre, mix_g_post=mix_g_post, w_in=w_in, w_out=w_out,
        mla_g_q=mla_g_q, mla_w_uq=mla_w_uq, mla_g_kv=mla_g_kv, mla_w_uk=mla_w_uk, mla_w_uv=mla_w_uv,
        gdn_conv_w=gdn_conv_w, gdn_a_log=gdn_a_log, gdn_dt_bias=gdn_dt_bias, gdn_g_norm=gdn_g_norm,
        s5_a_re=s5_a_re, s5_a_im=s5_a_im, s5_log_dt=s5_log_dt, s5_b_re=s5_b_re, s5_b_im=s5_b_im,
        s5_c_re=s5_c_re, s5_c_im=s5_c_im, s5_d=s5_d, s5_w_glu=s5_w_glu, s5_b_glu=s5_b_glu,
        xa_g_pre=xa_g_pre, xa_g_post=xa_g_post, xa_w_q=xa_w_q, xa_w_o=xa_w_o,
        ffn2_g_pre=ffn2_g_pre, ffn2_g_post=ffn2_g_post, ffn2_w_gate=ffn2_w_gate, ffn2_w_up=ffn2_w_up,
        ffn2_w_down=ffn2_w_down)
    depth = w_in.shape[0]
    bsz, seqlen, d_model = x_prompt.shape
    dec_b, dec_t, _ = x_sample.shape
    n_mem = mem_prompt.shape[1]
    past_len = page_table.shape[1] * PAGE_SIZE
    n_s = dec_b * dec_t

    tc_p, ts_p = _rope_tables(jnp.arange(seqlen))
    tc_s, ts_s = _rope_tables(jnp.tile(past_len + jnp.arange(dec_t), dec_b))

    mem_k, mem_v = _memproj(mem_prompt.reshape(n_mem, d_model), xa_w_k.astype(BF16), xa_w_v.astype(BF16))
    mem_k4 = mem_k.reshape(depth, bsz, n_mem, d_model)
    mem_v4 = mem_v.reshape(depth, bsz, n_mem, d_model)
    cache_kpe_t = jnp.swapaxes(cache_kpe, 2, 3)

    yp = x_prompt.reshape(seqlen, d_model)
    ys = x_sample.reshape(n_s, d_model)
    zero_state = jnp.zeros((1, S5_LANES), F32)
    outs = {k: [] for k in ("p_ckv", "p_kpe", "p_gdn", "p_conv", "p_s5r", "p_s5i",
                            "s_ckv", "s_kpe", "s_gdn", "s_conv", "s_s5r", "s_s5i")}
    weights = [_layer_weights(l, params) for l in range(depth)]
    ffn1_w = [a.astype(BF16) for a in (ffn1_w_gate, ffn1_w_up, ffn1_w_down)]
    ffn2_w = [a.astype(BF16) for a in (ffn2_w_gate, ffn2_w_up, ffn2_w_down)]
    for l, w in enumerate(weights):
        ys = _half_ffn(ys, w["ffn1_g_pre"], w["ffn1_g_post"], *ffn1_w, l)
        ckv, kpe, qkv, z, misc, su, qlat, qpe = _mixprep(True, ys, w, tc_s, ts_s)
        o_lat = _paged_attention(l, qlat, qpe, ckv, kpe, cache_ckv, cache_kpe_t, page_table, dec_t)
        st_rows = jnp.pad(state_gdn_conv[l], ((0, 0), (0, dec_t - (GDN_CONV - 1)), (0, 0))).reshape(n_s, GDN_QKV)
        s0 = jnp.transpose(state_gdn[l], (1, 2, 3, 0)).reshape(GDN_HEADS * GDN_DK, GDN_DV, dec_b)
        o_gdn, gdn_s = _gdn_sample(qkv, st_rows, z, misc, w, s0, dec_t)
        o_s5, h_re, h_im = _s5(su, w, state_s5_re[l].reshape(dec_b, S5_LANES),
                               state_s5_im[l].reshape(dec_b, S5_LANES), True, F32)
        x2, xq = _mixout(ys, o_lat, o_gdn, o_s5, w)
        o_xa = _xattn_sample(xq, cache_mem_k, cache_mem_v, l, dec_t)
        ys = _xaout(x2, o_xa, w)
        ys = _half_ffn(ys, w["ffn2_g_pre"], w["ffn2_g_post"], *ffn2_w, l)
        outs["s_ckv"].append(ckv.reshape(dec_b, dec_t, MLA_KV_RANK))
        outs["s_kpe"].append(kpe.reshape(dec_b, dec_t, MLA_ROPE))
        outs["s_gdn"].append(jnp.transpose(gdn_s.reshape(GDN_HEADS, GDN_DK, GDN_DV, dec_b), (3, 0, 1, 2)))
        outs["s_conv"].append(qkv.reshape(dec_b, dec_t, GDN_QKV)[:, dec_t - (GDN_CONV - 1):])
        outs["s_s5r"].append(h_re.reshape(dec_b, S5_GROUPS, S5_STATE))
        outs["s_s5i"].append(h_im.reshape(dec_b, S5_GROUPS, S5_STATE))

    for l, w in enumerate(weights):
        yp = _half_ffn(yp, w["ffn1_g_pre"], w["ffn1_g_post"], *ffn1_w, l)
        ckv, kpe, qkv, z, misc, su, q, k, v = _mixprep(False, yp, w, tc_p, ts_p)
        o_mla = _prompt_attention(q, k, v)
        o_gdn, gdn_s = _gdn_prompt(qkv, z, misc, w)
        o_s5, h_re, h_im = _s5(su, w, zero_state, zero_state, False, BF16)
        yp = _postmix(yp, o_mla, o_gdn, o_s5, w, mem_k4, mem_v4, l)
        yp = _half_ffn(yp, w["ffn2_g_pre"], w["ffn2_g_post"], *ffn2_w, l)
        outs["p_ckv"].append(ckv.reshape(bsz, seqlen, MLA_KV_RANK))
        outs["p_kpe"].append(kpe.reshape(bsz, seqlen, MLA_ROPE))
        outs["p_gdn"].append(gdn_s.reshape(bsz, GDN_HEADS, GDN_DK, GDN_DV))
        outs["p_conv"].append(qkv[seqlen - (GDN_CONV - 1):].reshape(bsz, GDN_CONV - 1, GDN_QKV))
        outs["p_s5r"].append(h_re.reshape(bsz, S5_GROUPS, S5_STATE))
        outs["p_s5i"].append(h_im.reshape(bsz, S5_GROUPS, S5_STATE))

    st = {k: jnp.stack(v) for k, v in outs.items()}
    p_mem_k = mem_k.reshape(depth, bsz, n_mem, XA_HEADS, XA_HEAD_DIM)
    p_mem_v = mem_v.reshape(depth, bsz, n_mem, XA_HEADS, XA_HEAD_DIM)
    return (yp.reshape(bsz, seqlen, d_model), ys.reshape(dec_b, dec_t, d_model),
            st["p_ckv"], st["p_kpe"], st["p_gdn"], st["p_conv"], st["p_s5r"], st["p_s5i"], p_mem_k, p_mem_v,
            st["s_ckv"], st["s_kpe"], st["s_gdn"], st["s_conv"], st["s_s5r"], st["s_s5i"])
```

```python
import functools
import math

import jax
import jax.numpy as jnp
import numpy as np
from jax import lax
from jax.experimental import pallas as pl
from jax.experimental.pallas import tpu as pltpu

F32 = jnp.float32
BF16 = jnp.bfloat16

RMS_EPS = 1e-6
MLA_HEADS = 8
MLA_NOPE = 64
MLA_ROPE = 32
MLA_V = 64
MLA_Q_RANK = 384
MLA_KV_RANK = 256
ROPE_THETA = 10000.0
HEAD_BLOCK = 128
PAGE_SIZE = 128
FFN_CHUNKS = 4
PAGES_PER_STEP = 64
PAGE_AHEAD = 2
PAGE_SLOTS = PAGE_AHEAD + 1
PAGE_GROUP = 32
ATTN_HEADS = 4
ATTN_TILE = 1024
ATTN_SUB = 512
GDN_HEADS = 4
GDN_DK = 64
GDN_DV = 64
GDN_CONV = 4
GDN_CHUNK = 64
GDN_BLOCKS_PER_STAGE = 2
GDN_QKV = GDN_HEADS * (2 * GDN_DK + GDN_DV)
S5_GROUPS = 16
S5_GROUP = 16
S5_STATE = 64
S5_WIDTH = S5_GROUPS * S5_GROUP
S5_LANES = S5_GROUPS * S5_STATE
XA_HEADS = 4
XA_HEAD_DIM = 256
XA_SEQ_PER_STEP = 4

VMEM_LIMIT_BYTES = 56 * 1024 * 1024

MLA_SCALE = (MLA_NOPE + MLA_ROPE) ** -0.5
LOG2_E = math.log2(math.e)
XA_SCALE = XA_HEAD_DIM ** -0.5


def _cparams(*sem):
    return pltpu.CompilerParams(dimension_semantics=tuple(sem), vmem_limit_bytes=VMEM_LIMIT_BYTES)


def _rms(x, g):
    return x * lax.rsqrt(jnp.mean(x * x, axis=-1, keepdims=True) + RMS_EPS) * g


def _bdot(a, b):
    return jnp.dot(a.astype(BF16), b.astype(BF16), preferred_element_type=F32)


def _bdot_nt(a, b):
    return lax.dot_general(a.astype(BF16), b.astype(BF16), (((1,), (1,)), ((), ())),
                           preferred_element_type=F32)


def _split3(a):
    hi = a.astype(BF16)
    r1 = a - hi.astype(F32)
    mid = r1.astype(BF16)
    lo = (r1 - mid.astype(F32)).astype(BF16)
    return hi, mid, lo


def _dot_split(a, b):
    a0 = a.astype(BF16)
    a1 = (a - a0.astype(F32)).astype(BF16)
    b0 = b.astype(BF16)
    b1 = (b - b0.astype(F32)).astype(BF16)
    d = functools.partial(jnp.dot, preferred_element_type=F32)
    return d(a0, b0) + (d(a0, b1) + d(a1, b0))


def _silu(x):
    return x * jax.nn.sigmoid(x)


def _softplus(x):
    return jnp.maximum(x, 0.0) + jnp.log(1.0 + jnp.exp(-jnp.abs(x)))


def _lane_chunks(n, parts):
    tiles = n // HEAD_BLOCK
    cuts = [round(tiles * i / parts) * HEAD_BLOCK for i in range(parts + 1)]
    return [(a, b) for a, b in zip(cuts[:-1], cuts[1:]) if b > a]


def _ffn_kernel(x_ref, gpre_ref, gpost_ref, wg_ref, wu_ref, wd_ref, o_ref):
    tm = x_ref.shape[0]
    half = tm // 2
    halves = range(2)
    xs = [x_ref[h * half:(h + 1) * half, :] for h in halves]
    xn = [_rms(x, gpre_ref[...]).astype(BF16) for x in xs]
    accs = [None, None]
    for a, b in _lane_chunks(wg_ref.shape[1], FFN_CHUNKS):
        for h in halves:
            g = jnp.dot(xn[h], wg_ref[:, a:b], preferred_element_type=F32)
            u = jnp.dot(xn[h], wu_ref[:, a:b], preferred_element_type=F32)
            t = jnp.dot((_silu(g) * u).astype(BF16), wd_ref[a:b, :], preferred_element_type=F32)
            accs[h] = t if accs[h] is None else accs[h] + t
    for h in halves:
        o_ref[h * half:(h + 1) * half, :] = xs[h] + 0.5 * _rms(accs[h], gpost_ref[...])


def _half_ffn(x, g_pre, g_post, wg, wu, wd, layer):
    rows, d = x.shape
    f = wg.shape[2]
    tm = min(1024, rows)
    wspec = lambda shape: pl.BlockSpec((None,) + shape, lambda i: (layer, 0, 0), pipeline_mode=pl.Buffered(1))
    return pl.pallas_call(
        _ffn_kernel,
        grid=(rows // tm,),
        in_specs=[
            pl.BlockSpec((tm, d), lambda i: (i, 0)),
            pl.BlockSpec((1, d), lambda i: (0, 0)),
            pl.BlockSpec((1, d), lambda i: (0, 0)),
            wspec((d, f)), wspec((d, f)), wspec((f, d)),
        ],
        out_specs=pl.BlockSpec((tm, d), lambda i: (i, 0)),
        out_shape=jax.ShapeDtypeStruct((rows, d), F32),
        compiler_params=_cparams("parallel"),
        name="half_ffn",
    )(x, g_pre, g_post, wg, wu, wd)


C_CQ = (0, 384)
C_CKV = (384, 640)
C_KPL = (640, 768)
C_KPLS = (768, 896)
C_MISC = (896, 1024)
C_QKV = (1024, 1792)
C_Z = (1792, 2048)
C_SU = (2048, 2304)
W_BIG = 2304


def _mixprep_body(sample, x_ref, g_ref, wbig_ref, gq_ref, wqa_ref, wqb_ref, gkv_ref, tc_ref, ts_ref,
                  wk_ref, wv_ref, ckv_ref, kpe_ref, qkv_ref, z_ref, misc_ref, su_ref, a_ref, b_ref, c_ref):
    u = _rms(x_ref[...], g_ref[...]).astype(BF16)
    y = jnp.dot(u, wbig_ref[...], preferred_element_type=F32)
    sl = lambda c: y[:, c[0]:c[1]]
    tc = tc_ref[...]
    ts = ts_ref[...]
    kpe_pl = sl(C_KPL) * tc + sl(C_KPLS) * ts
    kpe_ref[...] = kpe_pl[:, MLA_NOPE:MLA_NOPE + MLA_ROPE]
    ckv_n = _rms(sl(C_CKV), gkv_ref[...])
    ckv_ref[...] = ckv_n
    qkv_ref[...] = sl(C_QKV)
    z_ref[...] = sl(C_Z)
    misc_ref[...] = sl(C_MISC)
    su_ref[...] = sl(C_SU)

    cqn = _rms(sl(C_CQ), gq_ref[...]).astype(BF16)
    qa = jnp.dot(cqn, wqa_ref[...], preferred_element_type=F32)
    qb = jnp.dot(cqn, wqb_ref[...], preferred_element_type=F32)
    lane = lax.broadcasted_iota(jnp.int32, (1, HEAD_BLOCK), 1)
    qmul = jnp.where(lane < MLA_NOPE, 1.0, 0.0) + tc
    ckb = ckv_n.astype(BF16)
    if sample:
        qlat_ref, qpe_ref = a_ref, b_ref
        for h in range(MLA_HEADS):
            hs = slice(h * HEAD_BLOCK, (h + 1) * HEAD_BLOCK)
            qh = qa[:, hs] * qmul + qb[:, hs] * ts
            qlat_ref[h] = jnp.dot(qa[:, hs].astype(BF16), wk_ref[h], preferred_element_type=F32) * MLA_SCALE
            qpe_ref[h] = qh[:, MLA_NOPE:MLA_NOPE + MLA_ROPE] * MLA_SCALE
    else:
        q_ref, k_ref, v_ref = a_ref, b_ref, c_ref
        kn = jnp.dot(ckb, wk_ref[...], preferred_element_type=F32)
        for h in range(MLA_HEADS):
            hs = slice(h * HEAD_BLOCK, (h + 1) * HEAD_BLOCK)
            qh = qa[:, hs] * qmul + qb[:, hs] * ts
            q_ref[h] = (qh * (MLA_SCALE * LOG2_E)).astype(BF16)
            k_ref[h] = (kn[:, hs] + kpe_pl).astype(BF16)
        v_ref[...] = _bdot_nt(wv_ref[...], ckb).astype(BF16)


def _mixprep_prompt_kernel(x_ref, g_ref, wbig_ref, gq_ref, wqa_ref, wqb_ref, gkv_ref, tc_ref, ts_ref,
                           wk_ref, wv_ref, ckv_ref, kpe_ref, qkv_ref, z_ref, misc_ref, su_ref,
                           q_ref, k_ref, v_ref):
    _mixprep_body(False, x_ref, g_ref, wbig_ref, gq_ref, wqa_ref, wqb_ref, gkv_ref, tc_ref, ts_ref,
                  wk_ref, wv_ref, ckv_ref, kpe_ref, qkv_ref, z_ref, misc_ref, su_ref, q_ref, k_ref, v_ref)


def _mixprep_sample_kernel(x_ref, g_ref, wbig_ref, gq_ref, wqa_ref, wqb_ref, gkv_ref, tc_ref, ts_ref,
                           wk_ref, ckv_ref, kpe_ref, qkv_ref, z_ref, misc_ref, su_ref, qlat_ref, qpe_ref):
    _mixprep_body(True, x_ref, g_ref, wbig_ref, gq_ref, wqa_ref, wqb_ref, gkv_ref, tc_ref, ts_ref,
                  wk_ref, None, ckv_ref, kpe_ref, qkv_ref, z_ref, misc_ref, su_ref, qlat_ref, qpe_ref, None)


def _mixprep(sample, x, w, tc, ts):
    rows, d = x.shape
    tm = min(512, rows)
    row = lambda n: pl.BlockSpec((tm, n), lambda i: (i, 0))
    full = lambda a: pl.BlockSpec(a.shape, lambda i: (0,) * a.ndim)
    hrow = lambda n: pl.BlockSpec((MLA_HEADS, tm, n), lambda i: (0, i, 0))
    common_in = [x, w["mix_g_pre"], w["w_big"], w["mla_g_q"], w["wq_a"], w["wq_b"], w["mla_g_kv"], tc, ts]
    common_specs = [row(d)] + [full(a) for a in common_in[1:7]] + [row(HEAD_BLOCK), row(HEAD_BLOCK)]
    common_out = [
        (jax.ShapeDtypeStruct((rows, MLA_KV_RANK), F32), row(MLA_KV_RANK)),
        (jax.ShapeDtypeStruct((rows, MLA_ROPE), F32), row(MLA_ROPE)),
        (jax.ShapeDtypeStruct((rows, GDN_QKV), F32), row(GDN_QKV)),
        (jax.ShapeDtypeStruct((rows, GDN_HEADS * GDN_DV), F32), row(GDN_HEADS * GDN_DV)),
        (jax.ShapeDtypeStruct((rows, HEAD_BLOCK), F32), row(HEAD_BLOCK)),
        (jax.ShapeDtypeStruct((rows, S5_WIDTH), F32), row(S5_WIDTH)),
    ]
    if sample:
        ins = common_in + [w["wuk_t"]]
        specs = common_specs + [full(w["wuk_t"])]
        outs = common_out + [
            (jax.ShapeDtypeStruct((MLA_HEADS, rows, MLA_KV_RANK), F32), hrow(MLA_KV_RANK)),
            (jax.ShapeDtypeStruct((MLA_HEADS, rows, MLA_ROPE), F32), hrow(MLA_ROPE)),
        ]
        body = _mixprep_sample_kernel
    else:
        ins = common_in + [w["wuk_pad"], w["wuv"]]
        specs = common_specs + [full(w["wuk_pad"]), full(w["wuv"])]
        outs = common_out + [
            (jax.ShapeDtypeStruct((MLA_HEADS, rows, HEAD_BLOCK), BF16), hrow(HEAD_BLOCK)),
            (jax.ShapeDtypeStruct((MLA_HEADS, rows, HEAD_BLOCK), BF16), hrow(HEAD_BLOCK)),
            (jax.ShapeDtypeStruct((MLA_HEADS * MLA_V, rows), BF16),
             pl.BlockSpec((MLA_HEADS * MLA_V, tm), lambda i: (0, i))),
        ]
        body = _mixprep_prompt_kernel
    return pl.pallas_call(
        body,
        grid=(rows // tm,),
        in_specs=specs,
        out_specs=[o[1] for o in outs],
        out_shape=[o[0] for o in outs],
        compiler_params=_cparams("parallel"),
        name="mixprep_sample" if sample else "mixprep_prompt",
    )(*ins)


def _attn_kernel(qi_ref, ki_ref, q_ref, k_ref, vt_ref, o_ref, m_ref, l_ref, acc_ref):
    p = pl.program_id(1)
    i = qi_ref[p]
    j = ki_ref[p]
    tq, tk = q_ref.shape[1], k_ref.shape[1]

    @pl.when(j == 0)
    def _():
        m_ref[...] = jnp.full(m_ref.shape, -jnp.inf, F32)
        l_ref[...] = jnp.zeros_like(l_ref)
        acc_ref[...] = jnp.zeros_like(acc_ref)

    sub = min(ATTN_SUB, tk)

    def step(masked):
        hrows = [slice(hh * MLA_V, (hh + 1) * MLA_V) for hh in range(ATTN_HEADS)]
        m = [m_ref[hh] for hh in range(ATTN_HEADS)]
        l = [l_ref[hh] for hh in range(ATTN_HEADS)]
        acc = [acc_ref[hrows[hh], :] for hh in range(ATTN_HEADS)]
        units = [(c, hh) for c in range(tk // sub) for hh in range(ATTN_HEADS)]

        def scores(c, hh):
            return lax.dot_general(k_ref[hh, c * sub:(c + 1) * sub, :], q_ref[hh], (((1,), (1,)), ((), ())),
                                   preferred_element_type=F32)

        st_next = scores(*units[0])
        for n, (c, hh) in enumerate(units):
            st = st_next
            if n + 1 < len(units):
                st_next = scores(*units[n + 1])
            if masked:
                key = lax.broadcasted_iota(jnp.int32, (sub, tq), 0) + c * sub
                qry = lax.broadcasted_iota(jnp.int32, (sub, tq), 1)
                st = jnp.where(key <= qry, st, -jnp.inf)
            m_new = jnp.maximum(m[hh], jnp.max(st, axis=0, keepdims=True))
            alpha = jnp.exp2(m[hh] - m_new)
            pt = jnp.exp2(st - m_new)
            l[hh] = alpha * l[hh] + jnp.sum(pt, axis=0, keepdims=True)
            acc[hh] = alpha * acc[hh] + jnp.dot(vt_ref[hrows[hh], c * sub:(c + 1) * sub], pt.astype(BF16),
                                                preferred_element_type=F32)
            m[hh] = m_new
        for hh in range(ATTN_HEADS):
            m_ref[hh] = m[hh]
            l_ref[hh] = l[hh]
            acc_ref[hrows[hh], :] = acc[hh]

    @pl.when(j < i)
    def _():
        step(False)

    @pl.when(j == i)
    def _():
        step(True)
        inv = jnp.concatenate([jnp.broadcast_to(1.0 / l_ref[hh], (MLA_V, tq)) for hh in range(ATTN_HEADS)], axis=0)
        o_ref[...] = (acc_ref[...] * inv).T.astype(o_ref.dtype)


def _prompt_attention(q, k, vt):
    rows = q.shape[1]
    t = min(ATTN_TILE, rows)
    n = rows // t
    qi = np.array([i for i in range(n) for _ in range(i + 1)], np.int32)
    ki = np.array([j for i in range(n) for j in range(i + 1)], np.int32)
    grid_spec = pltpu.PrefetchScalarGridSpec(
        num_scalar_prefetch=2,
        grid=(MLA_HEADS // ATTN_HEADS, len(qi)),
        in_specs=[
            pl.BlockSpec((ATTN_HEADS, t, HEAD_BLOCK), lambda hp, p, qi, ki: (hp, qi[p], 0)),
            pl.BlockSpec((ATTN_HEADS, t, HEAD_BLOCK), lambda hp, p, qi, ki: (hp, ki[p], 0)),
            pl.BlockSpec((ATTN_HEADS * MLA_V, t), lambda hp, p, qi, ki: (hp, ki[p])),
        ],
        out_specs=pl.BlockSpec((t, ATTN_HEADS * MLA_V), lambda hp, p, qi, ki: (qi[p], hp)),
        scratch_shapes=[pltpu.VMEM((ATTN_HEADS, 1, t), F32), pltpu.VMEM((ATTN_HEADS, 1, t), F32),
                        pltpu.VMEM((ATTN_HEADS * MLA_V, t), F32)],
    )
    return pl.pallas_call(
        _attn_kernel,
        grid_spec=grid_spec,
        out_shape=jax.ShapeDtypeStruct((rows, MLA_HEADS * MLA_V), BF16),
        compiler_params=_cparams("parallel", "arbitrary"),
        name="mla_prompt_attention",
    )(jnp.asarray(qi), jnp.asarray(ki), q, k, vt)


def _paged_kernel(layer, n_pp, n_steps, n_total, pt_ref, qlat_ref, qpe_ref, cnew_ref, knew_ref, ckv_hbm, kpe_hbm,
                  o_ref, m_ref, l_ref, acc_ref, ckv_buf, kpe_buf, ckv_all_ref, kpe_all_ref, sem):
    j = pl.program_id(1)
    step = pl.program_id(0) * n_steps + j
    slot = lax.rem(step, PAGE_SLOTS)
    ahead = lax.rem(step + PAGE_AHEAD, n_total)
    ahead_slot = lax.rem(step + PAGE_AHEAD, PAGE_SLOTS)
    t_new = cnew_ref.shape[0]
    rows = MLA_HEADS * t_new

    n_grp = max(1, n_pp // PAGE_GROUP)
    per = n_pp // n_grp

    def page_copies(s, dst_slot, k):
        page = pt_ref[s * n_pp + k]
        return (pltpu.make_async_copy(ckv_hbm.at[layer, page], ckv_buf.at[dst_slot, k], sem.at[0, dst_slot]),
                pltpu.make_async_copy(kpe_hbm.at[layer, page], kpe_buf.at[dst_slot, k], sem.at[1, dst_slot]))

    def start_page(s, dst_slot, k):
        ckv_cp, kpe_cp = page_copies(s, dst_slot, k)
        ckv_cp.start(priority=k % 2)
        kpe_cp.start(priority=(k + 1) % 2)

    def wait_step(s, dst_slot):
        for k in range(n_pp):
            for cp in page_copies(s, dst_slot, k):
                cp.wait()

    @pl.when(step == 0)
    def _():
        for s in range(PAGE_AHEAD):
            for k in range(n_pp):
                start_page(s, s, k)

    ql = qlat_ref[...].reshape(rows, MLA_KV_RANK).astype(BF16)
    qp = qpe_ref[...].reshape(rows, MLA_ROPE).astype(BF16)

    @pl.when(j == 0)
    def _():
        cn = cnew_ref[...].astype(BF16)
        s = _bdot_nt(ql, cn) + _bdot_nt(qp, knew_ref[...])
        tok = lax.broadcasted_iota(jnp.int32, (rows, t_new), 0) % t_new
        key = lax.broadcasted_iota(jnp.int32, (rows, t_new), 1)
        s = jnp.where(key <= tok, s, -jnp.inf)
        m = jnp.max(s, axis=-1, keepdims=True)
        pm = jnp.exp(s - m)
        m_ref[...] = jnp.broadcast_to(m, m_ref.shape)
        l_ref[...] = jnp.broadcast_to(jnp.sum(pm, axis=-1, keepdims=True), l_ref.shape)
        acc_ref[...] = jnp.dot(pm.astype(BF16), cn, preferred_element_type=F32)

    wait_step(step, slot)
    gkeys = per * PAGE_SIZE

    def scores(g):
        for k in range(g * per, (g + 1) * per):
            start_page(ahead, ahead_slot, k)
            ckv_all_ref[k * PAGE_SIZE:(k + 1) * PAGE_SIZE, :] = ckv_buf[slot, k].astype(BF16)
            kpe_all_ref[:, k * PAGE_SIZE:(k + 1) * PAGE_SIZE] = kpe_buf[slot, k].astype(BF16)
        cg = ckv_all_ref[g * gkeys:(g + 1) * gkeys, :]
        kg = kpe_all_ref[:, g * gkeys:(g + 1) * gkeys]
        return _bdot_nt(ql, cg) + jnp.dot(qp, kg, preferred_element_type=F32), cg

    m = m_ref[...]
    l = l_ref[...]
    acc = acc_ref[...]
    nxt = scores(0)
    for g in range(n_grp):
        s, cg = nxt
        if g + 1 < n_grp:
            nxt = scores(g + 1)
        m_new = jnp.maximum(m, jnp.max(s, axis=-1, keepdims=True))
        alpha = jnp.exp(m - m_new)
        pm = jnp.exp(s - m_new[:, :1])
        l = alpha * l + jnp.sum(pm, axis=-1, keepdims=True)
        acc = alpha[:, :1] * acc + jnp.dot(pm.astype(BF16), cg, preferred_element_type=F32)
        m = m_new
    m_ref[...] = m
    l_ref[...] = l
    acc_ref[...] = acc

    @pl.when(j == n_steps - 1)
    def _():
        o = acc_ref[...] / l_ref[...][:, :1]
        o_ref[...] = o.reshape(MLA_HEADS, t_new, MLA_KV_RANK)

    @pl.when(step == n_total - 1)
    def _():
        for s in range(PAGE_AHEAD):
            wait_step(s, (n_total + s) % PAGE_SLOTS)


def _paged_attention(layer, qlat, qpe, ckv_new, kpe_new, cache_ckv, cache_kpe_t, page_table, t_new):
    n_b, n_pages = page_table.shape
    n_pp = min(PAGES_PER_STEP, n_pages)
    n_steps = n_pages // n_pp
    assert n_b * n_steps >= PAGE_AHEAD, "the page prefetch chain needs at least PAGE_AHEAD grid steps"
    rows = MLA_HEADS * t_new
    grid_spec = pltpu.PrefetchScalarGridSpec(
        num_scalar_prefetch=1,
        grid=(n_b, n_steps),
        in_specs=[
            pl.BlockSpec((MLA_HEADS, t_new, MLA_KV_RANK), lambda b, j, pt: (0, b, 0)),
            pl.BlockSpec((MLA_HEADS, t_new, MLA_ROPE), lambda b, j, pt: (0, b, 0)),
            pl.BlockSpec((t_new, MLA_KV_RANK), lambda b, j, pt: (b, 0)),
            pl.BlockSpec((t_new, MLA_ROPE), lambda b, j, pt: (b, 0)),
            pl.BlockSpec(memory_space=pl.ANY),
            pl.BlockSpec(memory_space=pl.ANY),
        ],
        out_specs=pl.BlockSpec((MLA_HEADS, t_new, MLA_KV_RANK), lambda b, j, pt: (0, b, 0)),
        scratch_shapes=[pltpu.VMEM((rows, HEAD_BLOCK), F32), pltpu.VMEM((rows, HEAD_BLOCK), F32),
                        pltpu.VMEM((rows, MLA_KV_RANK), F32),
                        pltpu.VMEM((PAGE_SLOTS, n_pp, PAGE_SIZE, MLA_KV_RANK), F32),
                        pltpu.VMEM((PAGE_SLOTS, n_pp, MLA_ROPE, PAGE_SIZE), F32),
                        pltpu.VMEM((n_pp * PAGE_SIZE, MLA_KV_RANK), BF16),
                        pltpu.VMEM((MLA_ROPE, n_pp * PAGE_SIZE), BF16),
                        pltpu.SemaphoreType.DMA((2, PAGE_SLOTS))],
    )
    return pl.pallas_call(
        functools.partial(_paged_kernel, layer, n_pp, n_steps, n_b * n_steps),
        grid_spec=grid_spec,
        out_shape=jax.ShapeDtypeStruct((MLA_HEADS, n_b * t_new, MLA_KV_RANK), F32),
        compiler_params=_cparams("arbitrary", "arbitrary"),
        name="mla_paged_attention",
    )(page_table.reshape(-1), qlat, qpe, ckv_new, kpe_new, cache_ckv, cache_kpe_t)


def _gdn_gates(misc, alog, dtb):
    beta = jax.nn.sigmoid(misc)
    g = -jnp.exp(alog) * _softplus(misc + dtb)
    return beta, g


def _l2n(x, scale):
    return x * (lax.rsqrt(jnp.sum(x * x, axis=-1, keepdims=True) + 1e-6) * scale)


def _gdn_prompt_kernel(qkv_ref, z_ref, misc_ref, convw_ref, alog_ref, dtb_ref, gnorm_ref,
                       o_ref, sfin_ref, s_ref, carry_ref):
    i = pl.program_id(0)

    @pl.when(i == 0)
    def _():
        s_ref[...] = jnp.zeros_like(s_ref)
        carry_ref[...] = jnp.zeros_like(carry_ref)

    x = qkv_ref[...]
    tm = x.shape[0]
    w = convw_ref[...]
    row8 = lax.broadcasted_iota(jnp.int32, (8, GDN_QKV), 0)
    cprev = carry_ref[...]
    acc = x * w[GDN_CONV - 1:GDN_CONV]
    for d in range(1, GDN_CONV):
        xr = pltpu.roll(x, d, 0)
        head = jnp.where(row8 < d, pltpu.roll(cprev, d, 0), xr[0:8])
        xs = jnp.concatenate([head, xr[8:]], axis=0)
        acc = acc + xs * w[GDN_CONV - 1 - d:GDN_CONV - d]
    carry_ref[...] = x[tm - 8:tm]
    conv = _silu(acc)

    nq = GDN_HEADS * GDN_DK
    beta_all, g_all = _gdn_gates(misc_ref[...], alog_ref[...], dtb_ref[...])
    z = z_ref[...]
    gnorm = gnorm_ref[...]
    c = GDN_CHUNK
    blk = 2 * c
    ri = lax.broadcasted_iota(jnp.int32, (blk, blk), 0)
    ci = lax.broadcasted_iota(jnp.int32, (blk, blk), 1)
    tri2 = jnp.where((ri >= ci) & ((ri // c) == (ci // c)), 1.0, 0.0).astype(BF16)
    ns = GDN_HEADS * c
    rs_ = lax.broadcasted_iota(jnp.int32, (ns, ns), 0)
    cs_ = lax.broadcasted_iota(jnp.int32, (ns, ns), 1)
    same = (rs_ // c) == (cs_ // c)
    incl = same & (rs_ >= cs_)
    strict = same & (rs_ > cs_)
    heads = range(GDN_HEADS)
    stack = lambda parts: jnp.concatenate(parts, axis=0)

    def prep_block(b2):
        gblk = g_all[b2 * blk:(b2 + 1) * blk]
        g0, g1, g2 = _split3(gblk)
        d = functools.partial(jnp.dot, preferred_element_type=F32)
        gcum = d(tri2, g0) + d(tri2, g1) + d(tri2, g2)
        gcum_t = gcum.T
        items = []
        for c2 in range(2):
            r0 = b2 * blk + c2 * c
            cr = slice(c2 * c, (c2 + 1) * c)
            qs = [_l2n(conv[r0:r0 + c, h * GDN_DK:(h + 1) * GDN_DK], GDN_DK ** -0.5) for h in heads]
            ks = [_l2n(conv[r0:r0 + c, nq + h * GDN_DK:nq + (h + 1) * GDN_DK], 1.0) for h in heads]
            vs = [conv[r0:r0 + c, 2 * nq + h * GDN_DV:2 * nq + (h + 1) * GDN_DV] for h in heads]
            gcs = [gcum[cr, GDN_HEADS + h:GDN_HEADS + h + 1] for h in heads]
            q_st, k_st, v_st, gc_st = stack(qs), stack(ks), stack(vs), stack(gcs)
            gr_st = jnp.concatenate([gcum_t[GDN_HEADS + h:GDN_HEADS + h + 1, cr] for h in heads], axis=1)
            beta_st = stack([beta_all[r0:r0 + c, h:h + 1] for h in heads])
            decay = jnp.exp(jnp.where(incl, gc_st - gr_st, -jnp.inf))
            eg_st = jnp.exp(gc_st)
            kk = _bdot_nt(k_st, k_st)
            mk = -jnp.where(strict, beta_st * kk * decay, 0.0)
            rhs = jnp.concatenate([v_st * beta_st, k_st * (beta_st * eg_st)], axis=-1)
            qk = _bdot_nt(q_st, k_st) * decay
            items.append([r0, rhs, qk, qs, ks, gcs, eg_st, mk])
        return items

    n_sq = int(math.log2(c))

    def solve_level(items, kq):
        for item in items:
            item[1] = item[1] + _dot_split(item[7], item[1])
            if kq < n_sq - 1:
                item[7] = _dot_split(item[7], item[7])

    state = [s_ref[h] for h in heads]

    def state_chunk(item):
        r0, rhs, qk, qs, ks, gcs, eg_st, _ = item
        outs = []
        for h in heads:
            hr = slice(h * c, (h + 1) * c)
            uu = rhs[hr, :GDN_DV]
            ww = rhs[hr, GDN_DV:]
            gc = gcs[h]
            glast = gc[c - 1:c, :]
            s_h = state[h]
            v_new = uu - _bdot(ww, s_h)
            o_h = _bdot(qs[h] * eg_st[hr], s_h) + _bdot(qk[hr, h * c:(h + 1) * c], v_new)
            kdec = ks[h] * jnp.exp(glast - gc)
            state[h] = s_h * jnp.exp(glast) + lax.dot_general(
                kdec.astype(BF16), v_new.astype(BF16), (((0,), (0,)), ((), ())),
                preferred_element_type=F32)
            zh = z[r0:r0 + c, h * GDN_DV:(h + 1) * GDN_DV]
            outs.append(_rms(o_h, gnorm) * _silu(zh))
        o_ref[r0:r0 + c, :] = jnp.concatenate(outs, axis=-1).astype(o_ref.dtype)

    n_blk = tm // blk
    groups = [list(range(g, min(g + GDN_BLOCKS_PER_STAGE, n_blk)))
              for g in range(0, n_blk, GDN_BLOCKS_PER_STAGE)]
    prepared = [it for b2 in groups[0] for it in prep_block(b2)]
    finished = []
    for gi in range(len(groups)):
        nxt_blocks = list(groups[gi + 1]) if gi + 1 < len(groups) else []
        nxt, todo = [], list(finished)
        for kq in range(n_sq):
            solve_level(prepared, kq)
            if nxt_blocks:
                nxt += prep_block(nxt_blocks.pop(0))
            for _ in range(2):
                if todo:
                    state_chunk(todo.pop(0))
        for b2 in nxt_blocks:
            nxt += prep_block(b2)
        for it in todo:
            state_chunk(it)
        finished, prepared = prepared, nxt
    for it in finished:
        state_chunk(it)
    for h in heads:
        s_ref[h] = state[h]

    @pl.when(i == pl.num_programs(0) - 1)
    def _():
        sfin_ref[...] = s_ref[...]


def _gdn_prompt(qkv, z, misc, w):
    rows = qkv.shape[0]
    tm = min(512, rows)
    row = lambda n: pl.BlockSpec((tm, n), lambda i: (i, 0))
    full = lambda a: pl.BlockSpec(a.shape, lambda i: (0,) * a.ndim)
    st = (GDN_HEADS, GDN_DK, GDN_DV)
    ins = [qkv, z, misc, w["gdn_conv_w"], w["gdn_alog_pad"], w["gdn_dtb_pad"], w["gdn_g_norm"]]
    return pl.pallas_call(
        _gdn_prompt_kernel,
        grid=(rows // tm,),
        in_specs=[row(GDN_QKV), row(GDN_HEADS * GDN_DV), row(HEAD_BLOCK)] + [full(a) for a in ins[3:]],
        out_specs=[row(GDN_HEADS * GDN_DV), pl.BlockSpec(st, lambda i: (0, 0, 0))],
        out_shape=[jax.ShapeDtypeStruct((rows, GDN_HEADS * GDN_DV), BF16), jax.ShapeDtypeStruct(st, F32)],
        scratch_shapes=[pltpu.VMEM(st, F32), pltpu.VMEM((8, GDN_QKV), F32)],
        compiler_params=_cparams("arbitrary"),
        name="gdn_prompt",
    )(*ins)


def _gdn_sample_kernel(t_new, qkv_ref, st_ref, z_ref, misc_ref, convw_ref, alog_ref, dtb_ref, gcol_ref,
                       s0_ref, o_ref, s_ref,
                       rows_ref, zo_ref, gate_ref, qt_ref, kt_ref, vt_ref, gt_ref, ot_ref):
    n_rows = qkv_ref.shape[0]
    n_b = n_rows // t_new
    nq = GDN_HEADS * GDN_DK
    x = qkv_ref[...]
    st = st_ref[...]
    w = convw_ref[...]
    tpos = lax.broadcasted_iota(jnp.int32, (n_rows, GDN_QKV), 0) % t_new
    acc = x * w[GDN_CONV - 1:GDN_CONV]
    for d in range(1, GDN_CONV):
        xr = pltpu.roll(x, d, 0)
        back = GDN_CONV - 1 - d
        sr = st if back == 0 else pltpu.roll(st, n_rows - back, 0)
        acc = acc + jnp.where(tpos < d, sr, xr) * w[GDN_CONV - 1 - d:GDN_CONV - d]
    conv = _silu(acc)
    parts = []
    for h in range(GDN_HEADS):
        parts.append(_l2n(conv[:, h * GDN_DK:(h + 1) * GDN_DK], GDN_DK ** -0.5))
    for h in range(GDN_HEADS):
        parts.append(_l2n(conv[:, nq + h * GDN_DK:nq + (h + 1) * GDN_DK], 1.0))
    parts.append(conv[:, 2 * nq:])
    feats = jnp.concatenate(parts, axis=-1)
    n_chunk = GDN_QKV // HEAD_BLOCK
    per_part = nq // HEAD_BLOCK
    for cc in range(n_chunk):
        rows_ref[cc] = feats[:, cc * HEAD_BLOCK:(cc + 1) * HEAD_BLOCK]
    zsil = _silu(z_ref[...])
    for cc in range(per_part):
        zo_ref[cc] = zsil[:, cc * HEAD_BLOCK:(cc + 1) * HEAD_BLOCK]
    beta_all, g_all = _gdn_gates(misc_ref[...], alog_ref[...], dtb_ref[...])
    lane = lax.broadcasted_iota(jnp.int32, beta_all.shape, 1)
    gate_ref[...] = jnp.where(lane < GDN_HEADS, beta_all, jnp.exp(g_all))

    for t in range(t_new):
        for cc in range(n_chunk):
            blk_t = rows_ref[cc, pl.ds(t, n_b, stride=t_new), :].T
            dst = (qt_ref, kt_ref, vt_ref)[cc // per_part]
            lo = (cc % per_part) * HEAD_BLOCK
            dst[t, lo:lo + HEAD_BLOCK, :] = blk_t
        gt_ref[t] = gate_ref[pl.ds(t, n_b, stride=t_new), :].T

    s_ref[...] = s0_ref[...]

    gcol = gcol_ref[...]
    for t in range(t_new):
        for h in range(GDN_HEADS):
            egr = gt_ref[t, GDN_HEADS + h:GDN_HEADS + h + 1, :]
            betar = gt_ref[t, h:h + 1, :]
            base = h * GDN_DK

            def p1(dk, racc):
                kb = kt_ref[t, pl.ds(base + dk, 1), :]
                return racc + s_ref[base + dk] * kb

            rr = lax.fori_loop(0, GDN_DK, p1, jnp.zeros((GDN_DV, n_b), F32), unroll=8) * egr
            dd = betar * (vt_ref[t, base:base + GDN_DV, :] - rr)

            def p2(dk, oacc):
                kb = kt_ref[t, pl.ds(base + dk, 1), :]
                qb = qt_ref[t, pl.ds(base + dk, 1), :]
                sn = s_ref[base + dk] * egr + kb * dd
                s_ref[base + dk] = sn
                return oacc + sn * qb

            oo = lax.fori_loop(0, GDN_DK, p2, jnp.zeros((GDN_DV, n_b), F32), unroll=8)
            on = oo * lax.rsqrt(jnp.mean(oo * oo, axis=0, keepdims=True) + RMS_EPS) * gcol
            ot_ref[base:base + GDN_DV, :] = on
        on_rows = ot_ref[...].T
        for cc in range(per_part):
            zt = zo_ref[cc, pl.ds(t, n_b, stride=t_new), :]
            zo_ref[cc, pl.ds(t, n_b, stride=t_new), :] = on_rows[:, cc * HEAD_BLOCK:(cc + 1) * HEAD_BLOCK] * zt
    for cc in range(per_part):
        o_ref[:, cc * HEAD_BLOCK:(cc + 1) * HEAD_BLOCK] = zo_ref[cc]


def _gdn_sample(qkv, st_rows, z, misc, w, s0, t_new):
    rows = qkv.shape[0]
    n_b = rows // t_new
    feat = GDN_HEADS * GDN_DK
    ins = [qkv, st_rows, z, misc, w["gdn_conv_w"], w["gdn_alog_pad"], w["gdn_dtb_pad"], w["gdn_g_col"], s0]
    return pl.pallas_call(
        functools.partial(_gdn_sample_kernel, t_new),
        out_shape=[jax.ShapeDtypeStruct((rows, GDN_HEADS * GDN_DV), F32),
                   jax.ShapeDtypeStruct((GDN_HEADS * GDN_DK, GDN_DV, n_b), F32)],
        scratch_shapes=[
            pltpu.VMEM((GDN_QKV // HEAD_BLOCK, rows, HEAD_BLOCK), F32),
            pltpu.VMEM((GDN_HEADS * GDN_DV // HEAD_BLOCK, rows, HEAD_BLOCK), F32),
            pltpu.VMEM((rows, HEAD_BLOCK), F32),
            pltpu.VMEM((t_new, feat, n_b), F32),
            pltpu.VMEM((t_new, feat, n_b), F32),
            pltpu.VMEM((t_new, GDN_HEADS * GDN_DV, n_b), F32),
            pltpu.VMEM((t_new, HEAD_BLOCK, n_b), F32),
            pltpu.VMEM((GDN_HEADS * GDN_DV, n_b), F32),
        ],
        compiler_params=pltpu.CompilerParams(vmem_limit_bytes=VMEM_LIMIT_BYTES),
        name="gdn_sample",
    )(*ins)


def _cmul(ar, ai, br, bi):
    return ar * br - ai * bi, ar * bi + ai * br


def _s5_kernel(per_group_state, u_ref, bbr_ref, bbi_ref, ar_ref, ai_ref, cr_ref, ci_ref, d_ref,
               wglu_ref, bglu_ref, h0r_ref, h0i_ref, o_ref, hr_out_ref, hi_out_ref,
               br_ref, bi_ref, cr_carry_ref, ci_carry_ref):
    i = pl.program_id(0)
    tm = u_ref.shape[0]
    n_grp = tm // 8

    if not per_group_state:
        @pl.when(i == 0)
        def _():
            cr_carry_ref[...] = jnp.zeros_like(cr_carry_ref)
            ci_carry_ref[...] = jnp.zeros_like(ci_carry_ref)

    u = u_ref[...]
    ub = u.astype(BF16)
    br_ref[...] = jnp.dot(ub, bbr_ref[...], preferred_element_type=F32)
    bi_ref[...] = jnp.dot(ub, bbi_ref[...], preferred_element_type=F32)

    ar = ar_ref[...]
    ai = ai_ref[...]
    p1 = (ar, ai)
    p2 = _cmul(*p1, *p1)
    p3 = _cmul(*p2, *p1)
    p4 = _cmul(*p2, *p2)
    p5 = _cmul(*p4, *p1)
    p6 = _cmul(*p4, *p2)
    p7 = _cmul(*p4, *p3)
    p8 = _cmul(*p4, *p4)
    row8 = lax.broadcasted_iota(jnp.int32, (8, S5_LANES), 0)
    pw_r = jnp.zeros((8, S5_LANES), F32)
    pw_i = jnp.zeros((8, S5_LANES), F32)
    for t, pw in enumerate((p1, p2, p3, p4, p5, p6, p7, p8)):
        pw_r = jnp.where(row8 == t, pw[0], pw_r)
        pw_i = jnp.where(row8 == t, pw[1], pw_i)

    steps = [(d, (jnp.where(row8 >= d, pw[0], 0.0), jnp.where(row8 >= d, pw[1], 0.0)))
             for d, pw in ((1, p1), (2, p2), (4, p4))]

    def body(gi, carry):
        r0 = pl.multiple_of(gi * 8, 8)
        a = br_ref[pl.ds(r0, 8), :]
        b = bi_ref[pl.ds(r0, 8), :]
        for d, pw in steps:
            da, db = _cmul(pw[0], pw[1], pltpu.roll(a, d, 0), pltpu.roll(b, d, 0))
            a = a + da
            b = b + db
        if per_group_state:
            c_r = h0r_ref[pl.ds(gi, 1), :]
            c_i = h0i_ref[pl.ds(gi, 1), :]
        else:
            c_r, c_i = carry
        da, db = _cmul(pw_r, pw_i, c_r, c_i)
        a = a + da
        b = b + db
        br_ref[pl.ds(r0, 8), :] = a
        bi_ref[pl.ds(r0, 8), :] = b
        if per_group_state:
            hr_out_ref[pl.ds(gi, 1), :] = a[7:8]
            hi_out_ref[pl.ds(gi, 1), :] = b[7:8]
            return carry
        return a[7:8], b[7:8]

    if per_group_state:
        zero = jnp.zeros((1, S5_LANES), F32)
        lax.fori_loop(0, n_grp, body, (zero, zero))
    else:
        c_fin = lax.fori_loop(0, n_grp, body, (cr_carry_ref[...], ci_carry_ref[...]))
        cr_carry_ref[...] = c_fin[0]
        ci_carry_ref[...] = c_fin[1]
        hr_out_ref[...] = c_fin[0]
        hi_out_ref[...] = c_fin[1]

    y = (jnp.dot(br_ref[...].astype(BF16), cr_ref[...], preferred_element_type=F32)
         - jnp.dot(bi_ref[...].astype(BF16), ci_ref[...], preferred_element_type=F32)
         + d_ref[...] * u)
    zg = jax.nn.gelu(y)
    gate = jax.nn.sigmoid(jnp.dot(zg.astype(BF16), wglu_ref[...], preferred_element_type=F32) + bglu_ref[...])
    o_ref[...] = (zg * gate).astype(o_ref.dtype)


def _s5(su, w, h0r, h0i, per_group_state, out_dtype):
    rows = su.shape[0]
    tm = min(512, rows)
    n_grp = tm // 8
    row = lambda n: pl.BlockSpec((tm, n), lambda i: (i, 0))
    full = lambda a: pl.BlockSpec(a.shape, lambda i: (0,) * a.ndim)
    wlist = [w["s5_bb_re"], w["s5_bb_im"], w["s5_ab_re"], w["s5_ab_im"], w["s5_c_re"], w["s5_c_im"],
             w["s5_d"], w["s5_w_glu"], w["s5_b_glu"]]
    if per_group_state:
        st_spec = pl.BlockSpec((n_grp, S5_LANES), lambda i: (i, 0))
        st_shape = jax.ShapeDtypeStruct((rows // 8, S5_LANES), F32)
    else:
        st_spec = pl.BlockSpec((1, S5_LANES), lambda i: (0, 0))
        st_shape = jax.ShapeDtypeStruct((1, S5_LANES), F32)
    return pl.pallas_call(
        functools.partial(_s5_kernel, per_group_state),
        grid=(rows // tm,),
        in_specs=[row(S5_WIDTH)] + [full(a) for a in wlist] + [st_spec, st_spec],
        out_specs=[row(S5_WIDTH), st_spec, st_spec],
        out_shape=[jax.ShapeDtypeStruct((rows, S5_WIDTH), out_dtype), st_shape, st_shape],
        scratch_shapes=[pltpu.VMEM((tm, S5_LANES), F32), pltpu.VMEM((tm, S5_LANES), F32),
                        pltpu.VMEM((1, S5_LANES), F32), pltpu.VMEM((1, S5_LANES), F32)],
        compiler_params=_cparams("arbitrary"),
        name="s5_sample" if per_group_state else "s5_prompt",
    )(su, *wlist, h0r, h0i)


def _mixout_kernel(x_ref, oa_ref, og_ref, os_ref, wuv_ref, woa_ref, wob_ref, woc_ref,
                   gpost_ref, gxa_ref, wq_ref, x2_ref, q_ref):
    mixed = None
    for h in range(MLA_HEADS):
        o_h = jnp.dot(oa_ref[h].astype(BF16), wuv_ref[h], preferred_element_type=F32)
        t = jnp.dot(o_h.astype(BF16), woa_ref[h * MLA_V:(h + 1) * MLA_V, :], preferred_element_type=F32)
        mixed = t if mixed is None else mixed + t
    mixed = mixed + jnp.dot(og_ref[...].astype(BF16), wob_ref[...], preferred_element_type=F32)
    mixed = mixed + jnp.dot(os_ref[...].astype(BF16), woc_ref[...], preferred_element_type=F32)
    x2 = x_ref[...] + _rms(mixed, gpost_ref[...])
    x2_ref[...] = x2
    hq = _rms(x2, gxa_ref[...]).astype(BF16)
    q_ref[...] = jnp.dot(hq, wq_ref[...], preferred_element_type=F32) * XA_SCALE


def _mixout(x, o_lat, o_gdn, o_s5, w):
    rows, d = x.shape
    tm = min(512, rows)
    row = lambda n: pl.BlockSpec((tm, n), lambda i: (i, 0))
    full = lambda a: pl.BlockSpec(a.shape, lambda i: (0,) * a.ndim)
    oa_spec = pl.BlockSpec((MLA_HEADS, tm, MLA_KV_RANK), lambda i: (0, i, 0))
    wl = [w["wuv_h"], w["wo_a"], w["wo_b"], w["wo_c"], w["mix_g_post"], w["xa_g_pre"], w["xa_w_q"]]
    return pl.pallas_call(
        _mixout_kernel,
        grid=(rows // tm,),
        in_specs=[row(d), oa_spec, row(o_gdn.shape[1]), row(o_s5.shape[1])] + [full(a) for a in wl],
        out_specs=[row(d), row(d)],
        out_shape=[jax.ShapeDtypeStruct((rows, d), F32), jax.ShapeDtypeStruct((rows, d), F32)],
        compiler_params=_cparams("parallel"),
        name="mixout_sample",
    )(x, o_lat, o_gdn, o_s5, *wl)


def _postmix_kernel(x_ref, oa_ref, og_ref, os_ref, woa_ref, wob_ref, woc_ref, gpost_ref, gxa_ref, wq_ref,
                    mk_ref, mv_ref, wo_ref, gxo_ref, y_ref):
    mixed = jnp.dot(oa_ref[...], woa_ref[...], preferred_element_type=F32)
    mixed = mixed + jnp.dot(og_ref[...], wob_ref[...], preferred_element_type=F32)
    mixed = mixed + jnp.dot(os_ref[...], woc_ref[...], preferred_element_type=F32)
    x2 = x_ref[...] + _rms(mixed, gpost_ref[...])
    q = jnp.dot(_rms(x2, gxa_ref[...]).astype(BF16), wq_ref[...], preferred_element_type=F32) * XA_SCALE
    outs = []
    for h in range(XA_HEADS):
        hs = slice(h * XA_HEAD_DIM, (h + 1) * XA_HEAD_DIM)
        s = _bdot_nt(q[:, hs], mk_ref[:, hs])
        m = jnp.max(s, axis=-1, keepdims=True)
        pm = jnp.exp(s - m)
        pr = pm / jnp.sum(pm, axis=-1, keepdims=True)
        outs.append(_bdot(pr, mv_ref[:, hs]).astype(BF16))
    xa = jnp.dot(jnp.concatenate(outs, axis=-1), wo_ref[...], preferred_element_type=F32)
    y_ref[...] = x2 + _rms(xa, gxo_ref[...])


def _postmix(x, o_mla, o_gdn, o_s5, w, mem_k, mem_v, layer):
    rows, d = x.shape
    n_mem = mem_k.shape[2]
    tm = min(512, rows)
    row = lambda n: pl.BlockSpec((tm, n), lambda i: (i, 0))
    full = lambda a: pl.BlockSpec(a.shape, lambda i: (0,) * a.ndim)
    mspec = pl.BlockSpec((None, None, n_mem, d), lambda i: (layer, 0, 0, 0))
    w1 = [w["wo_a"], w["wo_b"], w["wo_c"], w["mix_g_post"], w["xa_g_pre"], w["xa_w_q"]]
    w2 = [w["xa_w_o"], w["xa_g_post"]]
    return pl.pallas_call(
        _postmix_kernel,
        grid=(rows // tm,),
        in_specs=([row(d), row(o_mla.shape[1]), row(o_gdn.shape[1]), row(o_s5.shape[1])]
                  + [full(a) for a in w1] + [mspec, mspec] + [full(a) for a in w2]),
        out_specs=row(d),
        out_shape=jax.ShapeDtypeStruct((rows, d), F32),
        compiler_params=_cparams("parallel"),
        name="postmix_prompt",
    )(x, o_mla, o_gdn, o_s5, *w1, mem_k, mem_v, *w2)


def _xattn_heads_merged_kernel(n_b, q_ref, mk_ref, mv_ref, o_ref):
    rows = q_ref.shape[0] // n_b
    n_mem = mk_ref.shape[1]
    for bi in range(n_b):
        q = q_ref[bi * rows:(bi + 1) * rows, :]
        q_all = jnp.concatenate([q[:, h * XA_HEAD_DIM:(h + 1) * XA_HEAD_DIM] for h in range(XA_HEADS)], axis=0)
        k2 = mk_ref[bi].reshape(n_mem * XA_HEADS, XA_HEAD_DIM)
        v2 = mv_ref[bi].reshape(n_mem * XA_HEADS, XA_HEAD_DIM)
        s = _bdot_nt(q_all, k2)
        q_head = lax.broadcasted_iota(jnp.int32, s.shape, 0) // rows
        m_head = lax.broadcasted_iota(jnp.int32, s.shape, 1) % XA_HEADS
        s = jnp.where(q_head == m_head, s, -jnp.inf)
        m = jnp.max(s, axis=-1, keepdims=True)
        pm = jnp.exp(s - m)
        pr = pm / jnp.sum(pm, axis=-1, keepdims=True)
        o_all = _bdot(pr, v2)
        o_ref[bi * rows:(bi + 1) * rows, :] = jnp.concatenate(
            [o_all[h * rows:(h + 1) * rows] for h in range(XA_HEADS)], axis=-1).astype(o_ref.dtype)


def _xattn_sample(q, mem_k, mem_v, layer, t_new):
    rows, d = q.shape
    n_mem = mem_k.shape[2]
    n_b = min(XA_SEQ_PER_STEP, rows // t_new)
    tm = n_b * t_new
    mspec = pl.BlockSpec((None, n_b, n_mem, XA_HEADS, XA_HEAD_DIM), lambda i: (layer, i, 0, 0, 0))
    return pl.pallas_call(
        functools.partial(_xattn_heads_merged_kernel, n_b),
        grid=(rows // tm,),
        in_specs=[pl.BlockSpec((tm, d), lambda i: (i, 0)), mspec, mspec],
        out_specs=pl.BlockSpec((tm, d), lambda i: (i, 0)),
        out_shape=jax.ShapeDtypeStruct((rows, d), BF16),
        compiler_params=_cparams("parallel"),
        name="xattn_sample",
    )(q, mem_k, mem_v)


def _xaout_kernel(x_ref, o_ref, wo_ref, g_ref, y_ref):
    xa = jnp.dot(o_ref[...], wo_ref[...], preferred_element_type=F32)
    y_ref[...] = x_ref[...] + _rms(xa, g_ref[...])


def _xaout(x, o, w):
    rows, d = x.shape
    tm = min(512, rows)
    row = pl.BlockSpec((tm, d), lambda i: (i, 0))
    return pl.pallas_call(
        _xaout_kernel,
        grid=(rows // tm,),
        in_specs=[row, row, pl.BlockSpec((d, d), lambda i: (0, 0)), pl.BlockSpec((1, d), lambda i: (0, 0))],
        out_specs=row,
        out_shape=jax.ShapeDtypeStruct((rows, d), F32),
        compiler_params=_cparams("parallel"),
        name="xattn_out",
    )(x, o, w["xa_w_o"], w["xa_g_post"])


def _memproj_kernel(m_ref, wk_ref, wv_ref, k_ref, v_ref):
    m = m_ref[...].astype(BF16)
    for l in range(wk_ref.shape[0]):
        k_ref[l] = jnp.dot(m, wk_ref[l], preferred_element_type=F32)
        v_ref[l] = jnp.dot(m, wv_ref[l], preferred_element_type=F32)


def _memproj(mem, wk, wv):
    depth = wk.shape[0]
    shp = jax.ShapeDtypeStruct((depth,) + mem.shape, F32)
    return pl.pallas_call(
        _memproj_kernel,
        out_shape=[shp, shp],
        compiler_params=pltpu.CompilerParams(vmem_limit_bytes=VMEM_LIMIT_BYTES),
        name="mem_kv_proj",
    )(mem, wk, wv)


def _layer_weights(l, p):
    d_model = p["w_in"].shape[1]
    w = {}
    r1 = lambda a: a[l].reshape(1, -1).astype(F32)
    for name in ("ffn1", "ffn2"):
        w[name + "_g_pre"] = r1(p[name + "_g_pre"])
        w[name + "_g_post"] = r1(p[name + "_g_post"])
    for name in ("mix_g_pre", "mix_g_post", "mla_g_q", "mla_g_kv", "xa_g_pre", "xa_g_post", "gdn_g_norm",
                 "s5_d", "s5_b_glu"):
        w[name] = r1(p[name])
    w["gdn_g_col"] = p["gdn_g_norm"][l].reshape(-1, 1).astype(F32)

    w_in = p["w_in"][l]
    offs = np.cumsum([0, MLA_Q_RANK, MLA_KV_RANK, MLA_ROPE, GDN_QKV, GDN_HEADS * GDN_DV, GDN_HEADS, GDN_HEADS,
                      S5_WIDTH])
    w_cq, w_ckv, w_kpe, w_qkv, w_z, w_b, w_a, w_su = [w_in[:, offs[i]:offs[i + 1]] for i in range(8)]
    half = MLA_ROPE // 2
    zeros = lambda n: jnp.zeros((d_model, n), w_in.dtype)
    w_kpe_sw = jnp.concatenate([-w_kpe[:, half:], w_kpe[:, :half]], axis=1)
    tail = HEAD_BLOCK - MLA_NOPE - MLA_ROPE
    w_kpl = jnp.concatenate([zeros(MLA_NOPE), w_kpe, zeros(tail)], axis=1)
    w_kpls = jnp.concatenate([zeros(MLA_NOPE), w_kpe_sw, zeros(tail)], axis=1)
    w_misc = jnp.concatenate([w_b, w_a, zeros(HEAD_BLOCK - 2 * GDN_HEADS)], axis=1)
    w["w_big"] = jnp.concatenate([w_cq, w_ckv, w_kpl, w_kpls, w_misc, w_qkv, w_z, w_su], axis=1).astype(BF16)

    w_uq = p["mla_w_uq"][l].reshape(MLA_Q_RANK, MLA_HEADS, MLA_NOPE + MLA_ROPE)
    nope, x1, x2 = w_uq[..., :MLA_NOPE], w_uq[..., MLA_NOPE:MLA_NOPE + half], w_uq[..., MLA_NOPE + half:]
    zq = lambda n: jnp.zeros((MLA_Q_RANK, MLA_HEADS, n), w_uq.dtype)
    w["wq_a"] = jnp.concatenate([nope, x1, x2, zq(tail)], axis=-1).reshape(MLA_Q_RANK, -1).astype(BF16)
    w["wq_b"] = jnp.concatenate([zq(MLA_NOPE), -x2, x1, zq(tail)], axis=-1).reshape(MLA_Q_RANK, -1).astype(BF16)

    w_uk = p["mla_w_uk"][l]
    w_uv = p["mla_w_uv"][l]
    zk = jnp.zeros((MLA_KV_RANK, MLA_HEADS, HEAD_BLOCK - MLA_NOPE), w_uk.dtype)
    w["wuk_pad"] = jnp.concatenate([w_uk, zk], axis=-1).reshape(MLA_KV_RANK, -1).astype(BF16)
    wuk_t = jnp.transpose(w_uk, (1, 2, 0))
    w["wuk_t"] = jnp.concatenate(
        [wuk_t, jnp.zeros((MLA_HEADS, HEAD_BLOCK - MLA_NOPE, MLA_KV_RANK), w_uk.dtype)], axis=1).astype(BF16)
    w["wuv"] = w_uv.reshape(MLA_KV_RANK, -1).T.astype(BF16)
    w["wuv_h"] = jnp.transpose(w_uv, (1, 0, 2)).astype(BF16)

    w_out = p["w_out"][l]
    n_a = MLA_HEADS * MLA_V
    n_b = n_a + GDN_HEADS * GDN_DV
    w["wo_a"] = w_out[:n_a].astype(BF16)
    w["wo_b"] = w_out[n_a:n_b].astype(BF16)
    w["wo_c"] = w_out[n_b:].astype(BF16)
    w["xa_w_q"] = p["xa_w_q"][l].astype(BF16)
    w["xa_w_o"] = p["xa_w_o"][l].astype(BF16)

    w["gdn_conv_w"] = p["gdn_conv_w"][l].astype(F32)
    pad_gate = lambda v: jnp.zeros((1, HEAD_BLOCK), F32).at[0, GDN_HEADS:2 * GDN_HEADS].set(v.astype(F32))
    w["gdn_alog_pad"] = pad_gate(p["gdn_a_log"][l])
    w["gdn_dtb_pad"] = pad_gate(p["gdn_dt_bias"][l])

    a_re, a_im = p["s5_a_re"][l].astype(F32), p["s5_a_im"][l].astype(F32)
    dt = jnp.exp(p["s5_log_dt"][l].astype(F32))[:, None]
    mag = jnp.exp(a_re * dt)
    ab_re, ab_im = mag * jnp.cos(a_im * dt), mag * jnp.sin(a_im * dt)
    den = a_re * a_re + a_im * a_im
    nr, ni = ab_re - 1.0, ab_im
    coef_re = (nr * a_re + ni * a_im) / den
    coef_im = (ni * a_re - nr * a_im) / den
    b_re, b_im = p["s5_b_re"][l].astype(F32), p["s5_b_im"][l].astype(F32)
    bb_re = coef_re[..., None] * b_re - coef_im[..., None] * b_im
    bb_im = coef_re[..., None] * b_im + coef_im[..., None] * b_re
    eye = jnp.eye(S5_GROUPS, dtype=F32)
    bd_in = lambda bb: jnp.einsum("gnp,gh->gphn", bb, eye).reshape(S5_WIDTH, S5_LANES).astype(BF16)
    bd_out = lambda cc: jnp.einsum("gpn,gh->gnhp", cc.astype(F32), eye).reshape(S5_LANES, S5_WIDTH).astype(BF16)
    w["s5_bb_re"], w["s5_bb_im"] = bd_in(bb_re), bd_in(bb_im)
    w["s5_c_re"], w["s5_c_im"] = bd_out(p["s5_c_re"][l]), bd_out(p["s5_c_im"][l])
    w["s5_ab_re"] = ab_re.reshape(1, S5_LANES)
    w["s5_ab_im"] = ab_im.reshape(1, S5_LANES)
    w["s5_w_glu"] = p["s5_w_glu"][l].astype(BF16)
    return w


def _rope_tables(pos):
    half = MLA_ROPE // 2
    inv = ROPE_THETA ** (-jnp.arange(half, dtype=F32) / half)
    ang = pos.astype(F32)[:, None] * inv[None, :]
    c, s = jnp.cos(ang), jnp.sin(ang)
    n = pos.shape[0]
    z0 = jnp.zeros((n, MLA_NOPE), F32)
    z1 = jnp.zeros((n, HEAD_BLOCK - MLA_NOPE - MLA_ROPE), F32)
    return jnp.concatenate([z0, c, c, z1], axis=1), jnp.concatenate([z0, s, s, z1], axis=1)


def kernel(x_prompt, x_sample, mem_prompt, cache_ckv, cache_kpe, page_table, cache_mem_k, cache_mem_v, state_gdn, state_gdn_conv, state_s5_re, state_s5_im, ffn1_g_pre, ffn1_g_post, ffn1_w_gate, ffn1_w_up, ffn1_w_down, mix_g_pre, mix_g_post, w_in, w_out, mla_g_q, mla_w_uq, mla_g_kv, mla_w_uk, mla_w_uv, gdn_conv_w, gdn_a_log, gdn_dt_bias, gdn_g_norm, s5_a_re, s5_a_im, s5_log_dt, s5_b_re, s5_b_im, s5_c_re, s5_c_im, s5_d, s5_w_glu, s5_b_glu, xa_g_pre, xa_g_post, xa_w_q, xa_w_k, xa_w_v, xa_w_o, ffn2_g_pre, ffn2_g_post, ffn2_w_gate, ffn2_w_up, ffn2_w_down):
    params = dict(
        ffn1_g_pre=ffn1_g_pre, ffn1_g_post=ffn1_g_post, ffn1_w_gate=ffn1_w_gate, ffn1_w_up=ffn1_w_up,
        ffn1_w_down=ffn1_w_down, mix_g_pre=mix_g_pre, mix_g_post=mix_g_post, w_in=w_in, w_out=w_out,
        mla_g_q=mla_g_q, mla_w_uq=mla_w_uq, mla_g_kv=mla_g_kv, mla_w_uk=mla_w_uk, mla_w_uv=mla_w_uv,
        gdn_conv_w=gdn_conv_w, gdn_a_log=gdn_a_log, gdn_dt_bias=gdn_dt_bias, gdn_g_norm=gdn_g_norm,
        s5_a_re=s5_a_re, s5_a_im=s5_a_im, s5_log_dt=s5_log_dt, s5_b_re=s5_b_re, s5_b_im=s5_b_im,
        s5_c_re=s5_c_re, s5_c_im=s5_c_im, s5_d=s5_d, s5_w_glu=s5_w_glu, s5_b_glu=s5_b_glu,
        xa_g_pre=xa_g_pre, xa_g_post=xa_g_post, xa_w_q=xa_w_q, xa_w_o=xa_w_o,
        ffn2_g_pre=ffn2_g_pre, ffn2_g_post=ffn2_g_post, ffn2_w_gate=ffn2_w_gate, ffn2_w_up=ffn2_w_up,
        ffn2_w_down=ffn2_w_down)
    depth = w_in.shape[0]
    bsz, seqlen, d_model = x_prompt.shape
    dec_b, dec_t, _ = x_sample.shape
    n_mem = mem_prompt.shape[1]
    past_len = page_table.shape[1] * PAGE_SIZE
    n_s = dec_b * dec_t

    tc_p, ts_p = _rope_tables(jnp.arange(seqlen))
    tc_s, ts_s = _rope_tables(jnp.tile(past_len + jnp.arange(dec_t), dec_b))

    mem_k, mem_v = _memproj(mem_prompt.reshape(n_mem, d_model), xa_w_k.astype(BF16), xa_w_v.astype(BF16))
    mem_k4 = mem_k.reshape(depth, bsz, n_mem, d_model)
    mem_v4 = mem_v.reshape(depth, bsz, n_mem, d_model)
    cache_kpe_t = jnp.swapaxes(cache_kpe, 2, 3)

    yp = x_prompt.reshape(seqlen, d_model)
    ys = x_sample.reshape(n_s, d_model)
    zero_state = jnp.zeros((1, S5_LANES), F32)
    outs = {k: [] for k in ("p_ckv", "p_kpe", "p_gdn", "p_conv", "p_s5r", "p_s5i",
                            "s_ckv", "s_kpe", "s_gdn", "s_conv", "s_s5r", "s_s5i")}
    weights = [_layer_weights(l, params) for l in range(depth)]
    ffn1_w = [a.astype(BF16) for a in (ffn1_w_gate, ffn1_w_up, ffn1_w_down)]
    ffn2_w = [a.astype(BF16) for a in (ffn2_w_gate, ffn2_w_up, ffn2_w_down)]
    for l, w in enumerate(weights):
        ys = _half_ffn(ys, w["ffn1_g_pre"], w["ffn1_g_post"], *ffn1_w, l)
        ckv, kpe, qkv, z, misc, su, qlat, qpe = _mixprep(True, ys, w, tc_s, ts_s)
        o_lat = _paged_attention(l, qlat, qpe, ckv, kpe, cache_ckv, cache_kpe_t, page_table, dec_t)
        st_rows = jnp.pad(state_gdn_conv[l], ((0, 0), (0, dec_t - (GDN_CONV - 1)), (0, 0))).reshape(n_s, GDN_QKV)
        s0 = jnp.transpose(state_gdn[l], (1, 2, 3, 0)).reshape(GDN_HEADS * GDN_DK, GDN_DV, dec_b)
        o_gdn, gdn_s = _gdn_sample(qkv, st_rows, z, misc, w, s0, dec_t)
        o_s5, h_re, h_im = _s5(su, w, state_s5_re[l].reshape(dec_b, S5_LANES),
                               state_s5_im[l].reshape(dec_b, S5_LANES), True, F32)
        x2, xq = _mixout(ys, o_lat, o_gdn, o_s5, w)
        o_xa = _xattn_sample(xq, cache_mem_k, cache_mem_v, l, dec_t)
        ys = _xaout(x2, o_xa, w)
        ys = _half_ffn(ys, w["ffn2_g_pre"], w["ffn2_g_post"], *ffn2_w, l)
        outs["s_ckv"].append(ckv.reshape(dec_b, dec_t, MLA_KV_RANK))
        outs["s_kpe"].append(kpe.reshape(dec_b, dec_t, MLA_ROPE))
        outs["s_gdn"].append(jnp.transpose(gdn_s.reshape(GDN_HEADS, GDN_DK, GDN_DV, dec_b), (3, 0, 1, 2)))
        outs["s_conv"].append(qkv.reshape(dec_b, dec_t, GDN_QKV)[:, dec_t - (GDN_CONV - 1):])
        outs["s_s5r"].append(h_re.reshape(dec_b, S5_GROUPS, S5_STATE))
        outs["s_s5i"].append(h_im.reshape(dec_b, S5_GROUPS, S5_STATE))

    for l, w in enumerate(weights):
        yp = _half_ffn(yp, w["ffn1_g_pre"], w["ffn1_g_post"], *ffn1_w, l)
        ckv, kpe, qkv, z, misc, su, q, k, v = _mixprep(False, yp, w, tc_p, ts_p)
        o_mla = _prompt_attention(q, k, v)
        o_gdn, gdn_s = _gdn_prompt(qkv, z, misc, w)
        o_s5, h_re, h_im = _s5(su, w, zero_state, zero_state, False, BF16)
        yp = _postmix(yp, o_mla, o_gdn, o_s5, w, mem_k4, mem_v4, l)
        yp = _half_ffn(yp, w["ffn2_g_pre"], w["ffn2_g_post"], *ffn2_w, l)
        outs["p_ckv"].append(ckv.reshape(bsz, seqlen, MLA_KV_RANK))
        outs["p_kpe"].append(kpe.reshape(bsz, seqlen, MLA_ROPE))
        outs["p_gdn"].append(gdn_s.reshape(bsz, GDN_HEADS, GDN_DK, GDN_DV))
        outs["p_conv"].append(qkv[seqlen - (GDN_CONV - 1):].reshape(bsz, GDN_CONV - 1, GDN_QKV))
        outs["p_s5r"].append(h_re.reshape(bsz, S5_GROUPS, S5_STATE))
        outs["p_s5i"].append(h_im.reshape(bsz, S5_GROUPS, S5_STATE))

    st = {k: jnp.stack(v) for k, v in outs.items()}
    p_mem_k = mem_k.reshape(depth, bsz, n_mem, XA_HEADS, XA_HEAD_DIM)
    p_mem_v = mem_v.reshape(depth, bsz, n_mem, XA_HEADS, XA_HEAD_DIM)
    return (yp.reshape(bsz, seqlen, d_model), ys.reshape(dec_b, dec_t, d_model),
            st["p_ckv"], st["p_kpe"], st["p_gdn"], st["p_conv"], st["p_s5r"], st["p_s5i"], p_mem_k, p_mem_v,
            st["s_ckv"], st["s_kpe"], st["s_gdn"], st["s_conv"], st["s_s5r"], st["s_s5i"])
```
